```python
import jax, jax.numpy as jnp
from jax import lax
import numpy as np

D_MODEL = 2048
BATCH = 2
SEQ = 8192
DEPTH = 4

HEAD_DIM = 128
N_HEADS_TOTAL = D_MODEL // HEAD_DIM
N_HEADS_MOBA = N_HEADS_TOTAL // 4
N_HEADS_NSA = N_HEADS_TOTAL // 4
N_HEADS_FOX = N_HEADS_TOTAL - N_HEADS_MOBA - N_HEADS_NSA
MOBA_WIDTH = N_HEADS_MOBA * HEAD_DIM
NSA_WIDTH = N_HEADS_NSA * HEAD_DIM
FOX_WIDTH = N_HEADS_FOX * HEAD_DIM
MIX_WIDTH = MOBA_WIDTH + NSA_WIDTH + FOX_WIDTH
D_FF = 4 * D_MODEL

ROPE_THETA = 500000.0
ROPE_DIMS = HEAD_DIM // 4
Q_BLOCK = 128
MOBA_Q_BLOCK = 64
MOBA_BLOCK = 256
MOBA_TOPK = 3
NSA_CMP_LEN = 32
NSA_CMP_STRIDE = 16
NSA_CMP_HIDDEN = 256
NSA_SEL_BLOCK = 64
NSA_N_SEL = 16
NSA_WINDOW = 512
NSA_N_BRANCH = 3
NSA_N_KV = 6
NORM_EPS = 1e-6
NEG_INF = -1e30

PROJ_SIZES = (3 * MOBA_WIDTH, NSA_WIDTH, NSA_N_KV * HEAD_DIM,
              N_HEADS_NSA * NSA_N_BRANCH, 3 * FOX_WIDTH, N_HEADS_FOX)
PROJ_DIM = sum(PROJ_SIZES)
PROJ_OFFSETS = tuple(int(o) for o in np.cumsum(PROJ_SIZES)[:-1])

kernel_name = "hymba_moba_nsa_fox_sandwich_adaln"


def rms_norm(x, g):
    xf = x.astype(jnp.float32)
    y = xf * lax.rsqrt(jnp.mean(xf * xf, axis=-1, keepdims=True) + NORM_EPS)
    return (y * g.astype(jnp.float32)).astype(x.dtype)


def modulate(h, shift, scale):
    return h * (1.0 + scale) + shift


def rope_partial(x, positions):
    half = ROPE_DIMS // 2
    inv_freq = ROPE_THETA ** (-jnp.arange(half, dtype=jnp.float32) / half)
    ang = positions.astype(jnp.float32)[:, None, :, None] * inv_freq
    cos, sin = jnp.cos(ang), jnp.sin(ang)
    x1 = x[..., :half].astype(jnp.float32)
    x2 = x[..., half:ROPE_DIMS].astype(jnp.float32)
    rot = jnp.concatenate([x1 * cos - x2 * sin, x2 * cos + x1 * sin], axis=-1).astype(x.dtype)
    return jnp.concatenate([rot, x[..., ROPE_DIMS:]], axis=-1)


def masked_softmax(logits, mask):
    logits = jnp.where(mask, logits.astype(jnp.float32), NEG_INF)
    p = jax.nn.softmax(logits, axis=-1)
    return jnp.where(jnp.any(mask, axis=-1, keepdims=True), p, 0.0)


def to_heads(t, n):
    B, S, _ = t.shape
    return t.reshape(B, S, n, HEAD_DIM).transpose(0, 2, 1, 3)


def from_heads(t):
    B, H, S, Dh = t.shape
    return t.transpose(0, 2, 1, 3).reshape(B, S, H * Dh)


def merge_query_blocks(o):
    nq, B, H, qb, Dh = o.shape
    return jnp.moveaxis(o, 0, 2).reshape(B, H, nq * qb, Dh)


def moba_attention(q, k, v):
    B, H, S, Dh = q.shape
    scale = Dh ** -0.5
    nb = -(-S // MOBA_BLOCK)
    pad = nb * MOBA_BLOCK - S
    k_pad = jnp.pad(k, ((0, 0), (0, 0), (0, pad), (0, 0)))
    v_pad = jnp.pad(v, ((0, 0), (0, 0), (0, pad), (0, 0)))
    k_blk = k_pad.reshape(B, H, nb, MOBA_BLOCK, Dh)
    v_blk = v_pad.reshape(B, H, nb, MOBA_BLOCK, Dh)
    k_mean = jnp.mean(k_blk.astype(jnp.float32), axis=3)
    top = min(MOBA_TOPK, max(nb - 1, 1))
    n_sel = top * MOBA_BLOCK
    b_idx = jnp.arange(B)[:, None, None, None]
    h_idx = jnp.arange(H)[None, :, None, None]
    blk_ids = jnp.arange(nb)
    in_blk = jnp.arange(MOBA_BLOCK)

    def block_fn(i):
        t0 = i * MOBA_Q_BLOCK
        t = t0 + jnp.arange(MOBA_Q_BLOCK)
        own = t0 // MOBA_BLOCK
        qc = lax.dynamic_slice_in_dim(q, t0, MOBA_Q_BLOCK, axis=2)
        gate = jnp.einsum('bhqd,bhnd->bhqn', qc.astype(jnp.float32), k_mean)
        gate = jnp.where(blk_ids < own, gate, NEG_INF)
        _, idx = lax.top_k(gate, top)
        k_sel = k_blk[b_idx, h_idx, idx].reshape(B, H, MOBA_Q_BLOCK, n_sel, Dh)
        v_sel = v_blk[b_idx, h_idx, idx].reshape(B, H, MOBA_Q_BLOCK, n_sel, Dh)
        sel_mask = jnp.repeat(idx < own, MOBA_BLOCK, axis=-1)
        k_own = lax.dynamic_slice_in_dim(k_pad, own * MOBA_BLOCK, MOBA_BLOCK, axis=2)
        v_own = lax.dynamic_slice_in_dim(v_pad, own * MOBA_BLOCK, MOBA_BLOCK, axis=2)
        own_mask = (own * MOBA_BLOCK + in_blk)[None, :] <= t[:, None]
        logits = jnp.concatenate([
            jnp.einsum('bhqd,bhqkd->bhqk', qc, k_sel),
            jnp.einsum('bhqd,bhkd->bhqk', qc, k_own)], axis=-1).astype(jnp.float32) * scale
        mask = jnp.concatenate(
            [sel_mask, jnp.broadcast_to(own_mask, (B, H, MOBA_Q_BLOCK, MOBA_BLOCK))], axis=-1)
        p = masked_softmax(logits, mask)
        return (jnp.einsum('bhqk,bhqkd->bhqd', p[..., :n_sel], v_sel)
                + jnp.einsum('bhqk,bhkd->bhqd', p[..., n_sel:], v_own))

    return merge_query_blocks(lax.map(block_fn, jnp.arange(S // MOBA_Q_BLOCK)))


def compress_blocks(t, pe, w1, w2):
    B, S, Dh = t.shape
    n_cmp = (S - NSA_CMP_LEN) // NSA_CMP_STRIDE + 1
    tok = np.arange(n_cmp)[:, None] * NSA_CMP_STRIDE + np.arange(NSA_CMP_LEN)[None, :]
    blocks = t[:, tok] + pe
    flat = blocks.reshape(B, n_cmp, NSA_CMP_LEN * Dh)
    return jax.nn.gelu(flat @ w1) @ w2


def nsa_attention(q, kc, vc, ks, vs, kw, vw, gates, pe_k, pe_v, w1_k, w2_k, w1_v, w2_v):
    B, H, S, Dh = q.shape
    scale = Dh ** -0.5
    pos = jnp.arange(S)
    k_cmp = compress_blocks(kc, pe_k, w1_k, w2_k)
    v_cmp = compress_blocks(vc, pe_v, w1_v, w2_v)
    n_cmp = k_cmp.shape[1]
    cmp_end = jnp.arange(n_cmp) * NSA_CMP_STRIDE + NSA_CMP_LEN - 1
    cmp_mask = cmp_end[None, :] <= pos[:, None]
    p_cmp = masked_softmax(jnp.einsum('bhtd,bcd->bhtc', q, k_cmp) * scale, cmp_mask)
    o_cmp = jnp.einsum('bhtc,bcd->bhtd', p_cmp, v_cmp)
    n_sb = S // NSA_SEL_BLOCK
    c_first = (np.arange(n_cmp) * NSA_CMP_STRIDE) // NSA_SEL_BLOCK
    c_last = (np.arange(n_cmp) * NSA_CMP_STRIDE + NSA_CMP_LEN - 1) // NSA_SEL_BLOCK
    sb = np.arange(n_sb)
    overlap = ((sb[None, :] >= c_first[:, None]) & (sb[None, :] <= c_last[:, None])).astype(np.float32)
    imp = jnp.einsum('bhtc,cn->btn', p_cmp, jnp.asarray(overlap))
    n_sel = min(NSA_N_SEL, n_sb)
    ks_blk = ks.reshape(B, n_sb, NSA_SEL_BLOCK, Dh)
    vs_blk = vs.reshape(B, n_sb, NSA_SEL_BLOCK, Dh)
    kw_pad = jnp.pad(kw, ((0, 0), (NSA_WINDOW, 0), (0, 0)))
    vw_pad = jnp.pad(vw, ((0, 0), (NSA_WINDOW, 0), (0, 0)))
    b_idx = jnp.arange(B)[:, None, None]
    blk_ids = jnp.arange(n_sb)
    in_blk = jnp.arange(NSA_SEL_BLOCK)
    win_off = jnp.arange(NSA_WINDOW + Q_BLOCK)

    def block_fn(i):
        t0 = i * Q_BLOCK
        t = t0 + jnp.arange(Q_BLOCK)
        qc = lax.dynamic_slice_in_dim(q, t0, Q_BLOCK, axis=2)
        imp_c = lax.dynamic_slice_in_dim(imp, t0, Q_BLOCK, axis=1)
        cur = t // NSA_SEL_BLOCK
        forced = ((blk_ids[None, :] == 0) | (blk_ids[None, :] == cur[:, None])
                  | (blk_ids[None, :] == cur[:, None] - 1))
        allowed = blk_ids[None, :] <= cur[:, None]
        score = jnp.where(forced, jnp.inf, jnp.where(allowed, imp_c, -jnp.inf))
        _, idx = lax.top_k(score, n_sel)
        k_sel = ks_blk[b_idx, idx].reshape(B, Q_BLOCK, n_sel * NSA_SEL_BLOCK, Dh)
        v_sel = vs_blk[b_idx, idx].reshape(B, Q_BLOCK, n_sel * NSA_SEL_BLOCK, Dh)
        kpos = (idx[..., None] * NSA_SEL_BLOCK + in_blk).reshape(B, Q_BLOCK, -1)
        sel_mask = (kpos <= t[None, :, None])[:, None]
        p = masked_softmax(jnp.einsum('bhqd,bqkd->bhqk', qc, k_sel) * scale, sel_mask)
        o_sel = jnp.einsum('bhqk,bqkd->bhqd', p, v_sel)
        k_win = lax.dynamic_slice_in_dim(kw_pad, t0, NSA_WINDOW + Q_BLOCK, axis=1)
        v_win = lax.dynamic_slice_in_dim(vw_pad, t0, NSA_WINDOW + Q_BLOCK, axis=1)
        wpos = t0 - NSA_WINDOW + win_off
        win_mask = ((wpos[None, :] <= t[:, None]) & (wpos[None, :] > t[:, None] - NSA_WINDOW)
                    & (wpos[None, :] >= 0))
        p = masked_softmax(jnp.einsum('bhqd,bkd->bhqk', qc, k_win) * scale, win_mask)
        o_win = jnp.einsum('bhqk,bkd->bhqd', p, v_win)
        return o_sel, o_win

    o_sel, o_win = lax.map(block_fn, jnp.arange(S // Q_BLOCK))
    o_sel = merge_query_blocks(o_sel)
    o_win = merge_query_blocks(o_win)
    return gates[..., 0:1] * o_cmp + gates[..., 1:2] * o_sel + gates[..., 2:3] * o_win


def fox_attention(q, k, v, log_f):
    B, H, S, Dh = q.shape
    scale = Dh ** -0.5
    cum = jnp.cumsum(log_f, axis=-1)
    kpos = jnp.arange(S)

    def block_fn(i):
        t0 = i * Q_BLOCK
        t = t0 + jnp.arange(Q_BLOCK)
        qc = lax.dynamic_slice_in_dim(q, t0, Q_BLOCK, axis=2)
        cum_q = lax.dynamic_slice_in_dim(cum, t0, Q_BLOCK, axis=2)
        logits = (jnp.einsum('bhqd,bhkd->bhqk', qc, k).astype(jnp.float32) * scale
                  + (cum_q[..., None] - cum[:, :, None, :]))
        p = masked_softmax(logits, kpos[None, :] <= t[:, None])
        return jnp.einsum('bhqk,bhkd->bhqd', p, v)

    return merge_query_blocks(lax.map(block_fn, jnp.arange(S // Q_BLOCK)))


def hybrid_mixer(h, positions, w_in, b_forget, pe_k, pe_v, w1_k, w2_k, w1_v, w2_v, w_out):
    B, S, _ = h.shape
    proj = h @ w_in
    p_moba, p_nsa_q, p_nsa_kv, p_nsa_g, p_fox, p_fox_f = jnp.split(proj, PROJ_OFFSETS, axis=-1)
    qm, km, vm = (to_heads(t, N_HEADS_MOBA) for t in jnp.split(p_moba, 3, axis=-1))
    o_moba = moba_attention(rope_partial(qm, positions), rope_partial(km, positions), vm)
    qn = rope_partial(to_heads(p_nsa_q, N_HEADS_NSA), positions)
    kc, vc, ks, vs, kw, vw = jnp.split(p_nsa_kv, NSA_N_KV, axis=-1)
    rope_kv = lambda t: rope_partial(t[:, None], positions)[:, 0]
    gates = jax.nn.sigmoid(p_nsa_g.astype(jnp.float32)).reshape(
        B, S, N_HEADS_NSA, NSA_N_BRANCH).transpose(0, 2, 1, 3)
    o_nsa = nsa_attention(qn, rope_kv(kc), vc, rope_kv(ks), vs, rope_kv(kw), vw, gates,
                          pe_k, pe_v, w1_k, w2_k, w1_v, w2_v)
    qf, kf, vf = (to_heads(t, N_HEADS_FOX) for t in jnp.split(p_fox, 3, axis=-1))
    log_f = jax.nn.log_sigmoid((p_fox_f + b_forget).astype(jnp.float32)).transpose(0, 2, 1)
    o_fox = fox_attention(qf, kf, vf, log_f)
    o = jnp.concatenate([from_heads(o_moba), from_heads(o_nsa), from_heads(o_fox)],
                        axis=-1).astype(h.dtype)
    return o @ w_out


def sq_relu_mlp(h, w_up, w_down):
    return jnp.square(jax.nn.relu(h @ w_up)) @ w_down


def setup_inputs(seed: int = 0) -> dict:
    key = jax.random.key(seed)
    ks = jax.random.split(key, 20)
    f32 = jnp.float32

    def dense(k, shape, fan_in):
        return jax.random.normal(k, shape, f32) * fan_in ** -0.5

    def gain(k):
        return 1.0 + 0.05 * jax.random.normal(k, (DEPTH, D_MODEL), f32)

    cmp_in = NSA_CMP_LEN * HEAD_DIM
    return {
        "x": jax.random.normal(ks[0], (BATCH, SEQ, D_MODEL), f32),
        "c": jax.random.normal(ks[1], (BATCH, D_MODEL), f32),
        "positions": jnp.broadcast_to(jnp.arange(SEQ, dtype=jnp.int32), (BATCH, SEQ)),
        "w_mod": dense(ks[2], (DEPTH, D_MODEL, 6 * D_MODEL), D_MODEL),
        "b_mod": 0.02 * jax.random.normal(ks[3], (DEPTH, 6 * D_MODEL), f32),
        "g_pre_mix": gain(ks[4]),
        "g_post_mix": gain(ks[5]),
        "g_pre_mlp": gain(ks[6]),
        "g_post_mlp": gain(ks[7]),
        "w_in": dense(ks[8], (DEPTH, D_MODEL, PROJ_DIM), D_MODEL),
        "b_forget": 3.0 + 0.5 * jax.random.normal(ks[9], (DEPTH, N_HEADS_FOX), f32),
        "cmp_pe_k": 0.02 * jax.random.normal(ks[10], (DEPTH, NSA_CMP_LEN, HEAD_DIM), f32),
        "cmp_pe_v": 0.02 * jax.random.normal(ks[11], (DEPTH, NSA_CMP_LEN, HEAD_DIM), f32),
        "cmp_w1_k": dense(ks[12], (DEPTH, cmp_in, NSA_CMP_HIDDEN), cmp_in),
        "cmp_w2_k": dense(ks[13], (DEPTH, NSA_CMP_HIDDEN, HEAD_DIM), NSA_CMP_HIDDEN),
        "cmp_w1_v": dense(ks[14], (DEPTH, cmp_in, NSA_CMP_HIDDEN), cmp_in),
        "cmp_w2_v": dense(ks[15], (DEPTH, NSA_CMP_HIDDEN, HEAD_DIM), NSA_CMP_HIDDEN),
        "w_out": dense(ks[16], (DEPTH, MIX_WIDTH, D_MODEL), MIX_WIDTH),
        "w_up": dense(ks[17], (DEPTH, D_MODEL, D_FF), D_MODEL),
        "w_down": dense(ks[18], (DEPTH, D_FF, D_MODEL), D_FF),
    }


def reference(x, c, positions, w_mod, b_mod, g_pre_mix, g_post_mix, g_pre_mlp, g_post_mlp,
              w_in, b_forget, cmp_pe_k, cmp_pe_v, cmp_w1_k, cmp_w2_k, cmp_w1_v, cmp_w2_v,
              w_out, w_up, w_down):
    c_act = jax.nn.silu(c)
    for l in range(DEPTH):
        mod = (c_act @ w_mod[l] + b_mod[l])[:, None, :]
        shift_a, scale_a, gate_a, shift_m, scale_m, gate_m = jnp.split(mod, 6, axis=-1)
        h = modulate(rms_norm(x, g_pre_mix[l]), shift_a, scale_a)
        y = hybrid_mixer(h, positions, w_in[l], b_forget[l], cmp_pe_k[l], cmp_pe_v[l],
                         cmp_w1_k[l], cmp_w2_k[l], cmp_w1_v[l], cmp_w2_v[l], w_out[l])
        x = x + gate_a * rms_norm(y, g_post_mix[l])
        h = modulate(rms_norm(x, g_pre_mlp[l]), shift_m, scale_m)
        x = x + gate_m * rms_norm(sq_relu_mlp(h, w_up[l], w_down[l]), g_post_mlp[l])
    return x
```

```python
import functools

import jax
import jax.numpy as jnp
import numpy as np
from jax import lax
from jax.experimental import pallas as pl
from jax.experimental.pallas import tpu as pltpu

HEAD_DIM = 128
ROPE_THETA = 500000.0
ROPE_DIMS = HEAD_DIM // 4
ROPE_HALF = ROPE_DIMS // 2
MOBA_BLOCK = 256
MOBA_TOPK = 3
NSA_CMP_LEN = 32
NSA_CMP_STRIDE = 16
NSA_CMP_HIDDEN = 256
NSA_SEL_BLOCK = 64
NSA_N_SEL = 16
NSA_WINDOW = 512
NSA_N_BRANCH = 3
NSA_N_KV = 6
NORM_EPS = 1e-6
NEG_INF = -1e30
ATTN_SCALE = HEAD_DIM ** -0.5

LANES = 128
VMEM_LIMIT_BYTES = 56 * 1024 * 1024

F32 = jnp.float32
BF16 = jnp.bfloat16


def _params(n_axes):
    return pltpu.CompilerParams(dimension_semantics=("arbitrary",) * n_axes,
                                vmem_limit_bytes=VMEM_LIMIT_BYTES)


def _resident(block_shape, index_map):
    return pl.BlockSpec(block_shape, index_map, pipeline_mode=pl.Buffered(1))


def _split3(x):
    p1 = x.astype(BF16)
    r1 = x - p1.astype(F32)
    p2 = r1.astype(BF16)
    p3 = (r1 - p2.astype(F32)).astype(BF16)
    return p1, p2, p3


def _dot(a, b):
    return jnp.dot(a, b, preferred_element_type=F32)


def _dot_nt(a, b):
    return lax.dot_general(a, b, (((1,), (1,)), ((), ())), preferred_element_type=F32)


def _rms(x, g):
    return x * lax.rsqrt(jnp.mean(x * x, axis=-1, keepdims=True) + NORM_EPS) * g


def _sigmoid(z):
    return 1.0 / (1.0 + jnp.exp(-z))


def _lane_col(x, c):
    return jnp.broadcast_to(x[:, c:c + 1], x.shape)


def _mod_kernel(c_ref, w_ref, b_ref, o_ref):
    c = c_ref[...]
    ca = c * _sigmoid(c)
    o_ref[0] = jnp.dot(ca, w_ref[0], precision=lax.Precision.HIGHEST,
                       preferred_element_type=F32) + b_ref[0]


def _mod_call(c_pad, w_mod, b_mod):
    depth, d, n = w_mod.shape
    rows = c_pad.shape[0]
    tn = 1024
    assert n % tn == 0
    return pl.pallas_call(
        _mod_kernel,
        grid=(depth, n // tn),
        in_specs=[pl.BlockSpec((rows, d), lambda l, j: (0, 0)),
                  pl.BlockSpec((1, d, tn), lambda l, j: (l, 0, j)),
                  pl.BlockSpec((1, 1, tn), lambda l, j: (l, 0, j))],
        out_specs=pl.BlockSpec((1, rows, tn), lambda l, j: (l, 0, j)),
        out_shape=jax.ShapeDtypeStruct((depth, rows, n), F32),
        compiler_params=_params(2),
        name="adaln_mod",
    )(c_pad, w_mod, b_mod.reshape(depth, 1, n))


def _inproj_kernel(x_ref, g_ref, sh_ref, sc_ref, cos_ref, sin_ref, w_ref,
                   mq_ref, mk_ref, mv_ref, kmean_ref, nq_ref, cv_ref,
                   ks_ref, vs_ref, kw_ref, vw_ref, small_ref, fq_ref, fk_ref, fv_ref,
                   *, hm, hn, hf):
    x = x_ref[0]
    h = _rms(x, g_ref[...]) * (1.0 + sc_ref[0]) + sh_ref[0]
    hb = h.astype(BF16)
    tm = x.shape[0]
    cos = cos_ref[0]
    sin = sin_ref[0]
    lane = lax.broadcasted_iota(jnp.int32, (tm, LANES), 1)

    def rope(y):
        swapped = jnp.where(lane < ROPE_HALF, pltpu.roll(y, LANES - ROPE_HALF, 1),
                            pltpu.roll(y, ROPE_HALF, 1))
        return y * cos + swapped * sin

    def heads(col0, n):
        outs = []
        j = 0
        while j < n:
            w = 2 if j + 1 < n else 1
            y = _dot(hb, w_ref[:, (col0 + j) * LANES:(col0 + j + w) * LANES])
            for t in range(w):
                outs.append(y[:, t * LANES:(t + 1) * LANES])
            j += w
        return outs

    col = 0
    for hh, y in enumerate(heads(col, hm)):
        mq_ref[0, hh] = (rope(y) * ATTN_SCALE).astype(BF16)
    col += hm
    nblk = tm // MOBA_BLOCK
    means = []
    for hh, y in enumerate(heads(col, hm)):
        yr = rope(y)
        mk_ref[0, hh] = yr.astype(BF16)
        means.append(jnp.mean(yr.reshape(nblk, MOBA_BLOCK, LANES), axis=1))
    kmean_ref[0, 0] = jnp.concatenate(means, axis=0)
    col += hm
    for hh, y in enumerate(heads(col, hm)):
        mv_ref[0, hh] = y.astype(BF16)
    col += hm
    for hh, y in enumerate(heads(col, hn)):
        nq_ref[0, hh] = (rope(y) * ATTN_SCALE).astype(BF16)
    col += hn
    kc, vc, ks, vs, kw, vw = heads(col, NSA_N_KV)
    cv_ref[0, 0] = rope(kc)
    cv_ref[0, 1] = vc
    ks_ref[0] = rope(ks).astype(BF16)
    vs_ref[0] = vs.astype(BF16)
    kw_ref[0] = rope(kw).astype(BF16)
    vw_ref[0] = vw.astype(BF16)
    col += NSA_N_KV
    for hh, y in enumerate(heads(col, hf)):
        fq_ref[0, hh] = (y * ATTN_SCALE).astype(BF16)
    col += hf
    for hh, y in enumerate(heads(col, hf)):
        fk_ref[0, hh] = y.astype(BF16)
    col += hf
    for hh, y in enumerate(heads(col, hf)):
        fv_ref[0, hh] = y.astype(BF16)
    col += hf
    small_ref[0] = heads(col, 1)[0]


def _inproj_call(x, g, shift, scale, cos_t, sin_t, w_cat, *, hm, hn, hf, tm):
    b, s, d = x.shape
    npad = w_cat.shape[1]
    nblk = tm // MOBA_BLOCK
    row = lambda bb, i: (bb, i, 0)
    vec = lambda bb, i: (bb, 0, 0)
    head_spec = lambda n: pl.BlockSpec((1, n, tm, LANES), lambda bb, i: (bb, 0, i, 0))
    head_shape = lambda n, dt=BF16: jax.ShapeDtypeStruct((b, n, s, LANES), dt)
    tok_spec = pl.BlockSpec((1, tm, LANES), row)
    tok_shape = lambda dt=BF16: jax.ShapeDtypeStruct((b, s, LANES), dt)
    out_specs = [head_spec(hm), head_spec(hm), head_spec(hm),
                 pl.BlockSpec((1, 1, hm * nblk, LANES), lambda bb, i: (bb, i, 0, 0)),
                 head_spec(hn), head_spec(2),
                 tok_spec, tok_spec, tok_spec, tok_spec, tok_spec,
                 head_spec(hf), head_spec(hf), head_spec(hf)]
    out_shape = [head_shape(hm), head_shape(hm), head_shape(hm),
                 jax.ShapeDtypeStruct((b, s // tm, hm * nblk, LANES), F32),
                 head_shape(hn), head_shape(2, F32),
                 tok_shape(), tok_shape(), tok_shape(), tok_shape(), tok_shape(F32),
                 head_shape(hf), head_shape(hf), head_shape(hf)]
    return pl.pallas_call(
        functools.partial(_inproj_kernel, hm=hm, hn=hn, hf=hf),
        grid=(b, s // tm),
        in_specs=[pl.BlockSpec((1, tm, d), row),
                  pl.BlockSpec((1, d), lambda bb, i: (0, 0)),
                  pl.BlockSpec((1, 1, d), vec),
                  pl.BlockSpec((1, 1, d), vec),
                  pl.BlockSpec((1, tm, LANES), row),
                  pl.BlockSpec((1, tm, LANES), row),
                  _resident((d, npad), lambda bb, i: (0, 0))],
        out_specs=out_specs,
        out_shape=out_shape,
        compiler_params=_params(2),
        name="inproj",
    )(x, g, shift, scale, cos_t, sin_t, w_cat)


def _fox_prep_kernel(small_ref, bf_ref, qa_ref, ka_ref, carry_ref, *, hf, col0):
    i = pl.program_id(1)

    @pl.when(i == 0)
    def _():
        carry_ref[...] = jnp.zeros_like(carry_ref)

    z = small_ref[0] + bf_ref[...]
    logf = jnp.minimum(z, 0.0) - jnp.log1p(jnp.exp(-jnp.abs(z)))
    ts = z.shape[0]
    r = lax.broadcasted_iota(jnp.int32, (ts, ts), 0)
    c = lax.broadcasted_iota(jnp.int32, (ts, ts), 1)
    tri = jnp.where(r >= c, 1.0, 0.0).astype(BF16)
    p1, p2, p3 = _split3(logf)
    cum = _dot(tri, p1) + _dot(tri, p2) + _dot(tri, p3) + carry_ref[...]
    carry_ref[...] = cum[ts - 1:ts, :]
    lane = lax.broadcasted_iota(jnp.int32, (ts, LANES), 1)
    for hh in range(hf):
        cb = _lane_col(cum, col0 + hh)
        c1, c2, c3 = (p.astype(F32) for p in _split3(cb))
        qa = jnp.where(lane == 0, c1, jnp.where(lane == 1, c2, jnp.where(
            lane == 2, c3, jnp.where(lane < 6, 1.0, 0.0))))
        ka = jnp.where(lane < 3, 1.0, jnp.where(lane == 3, -c1, jnp.where(
            lane == 4, -c2, jnp.where(lane == 5, -c3, 0.0))))
        qa_ref[0, hh] = qa.astype(BF16)
        ka_ref[0, hh] = ka.astype(BF16)


def _fox_prep_call(small, bf_row, *, hf, col0, ts):
    b, s, _ = small.shape
    spec = pl.BlockSpec((1, hf, ts, LANES), lambda bb, i: (bb, 0, i, 0))
    shape = jax.ShapeDtypeStruct((b, hf, s, LANES), BF16)
    return pl.pallas_call(
        functools.partial(_fox_prep_kernel, hf=hf, col0=col0),
        grid=(b, s // ts),
        in_specs=[pl.BlockSpec((1, ts, LANES), lambda bb, i: (bb, i, 0)),
                  pl.BlockSpec((1, LANES), lambda bb, i: (0, 0))],
        out_specs=[spec, spec],
        out_shape=[shape, shape],
        scratch_shapes=[pltpu.VMEM((1, LANES), F32)],
        compiler_params=_params(2),
        name="fox_prep",
    )(small, bf_row)


def _first_max_pick(score, lane):
    m = jnp.max(score, axis=-1, keepdims=True)
    lane_f = lane.astype(F32)
    idx = jnp.min(jnp.where(score == m, lane_f, float(LANES)), axis=-1, keepdims=True)
    return lane_f == idx, m


def _moba_select_kernel(q_ref, km_ref, o_ref, *, nb, top):
    i = pl.program_id(1)
    q = q_ref[0, 0]
    tq = q.shape[0]
    k1, k2, k3 = _split3(km_ref[0])
    gate = _dot_nt(q, k1) + _dot_nt(q, k2) + _dot_nt(q, k3)
    lane = lax.broadcasted_iota(jnp.int32, (tq, LANES), 1)
    tok = i * tq + lax.broadcasted_iota(jnp.int32, (tq, LANES), 0)
    own = tok >> (MOBA_BLOCK.bit_length() - 1)
    past = lane < own
    score = jnp.where(lane < nb, jnp.where(past, gate, NEG_INF), -jnp.inf)
    sel = lane == own
    for _ in range(top):
        pick, _m = _first_max_pick(score, lane)
        sel = sel | (pick & past)
        score = jnp.where(pick, -jnp.inf, score)
    o_ref[0] = jnp.where(sel | (lane >= nb), 0.0, NEG_INF).astype(BF16)


def _moba_select_call(q, kmean_pad, *, nb, tq):
    g, _, s, _ = q.shape
    top = min(MOBA_TOPK, max(nb - 1, 1))
    return pl.pallas_call(
        functools.partial(_moba_select_kernel, nb=nb, top=top),
        grid=(g, s // tq),
        in_specs=[pl.BlockSpec((1, 1, tq, LANES), lambda gg, i: (gg, 0, i, 0)),
                  pl.BlockSpec((1, LANES, LANES), lambda gg, i: (gg, 0, 0))],
        out_specs=pl.BlockSpec((1, tq, LANES), lambda gg, i: (gg, i, 0)),
        out_shape=jax.ShapeDtypeStruct((g, s, LANES), BF16),
        compiler_params=_params(2),
        name="moba_select",
    )(q, kmean_pad)


def _attn_kernel(q_ref, qa_ref, k_ref, ka_ref, v_ref, o_ref, m_ref, l_ref, acc_ref,
                 *, hs, tq, tk):
    i = pl.program_id(1)
    rows = hs * tq
    qc = jnp.concatenate([q_ref[0].reshape(rows, LANES),
                          jnp.concatenate([qa_ref[0]] * hs, axis=0)], axis=-1)
    m_ref[...] = jnp.full_like(m_ref, -jnp.inf)
    l_ref[...] = jnp.zeros_like(l_ref)
    acc_ref[...] = jnp.zeros_like(acc_ref)

    def step(j, masked):
        start = pl.multiple_of(j * tk, tk)
        kc = jnp.concatenate([k_ref[0, pl.ds(start, tk), :],
                              ka_ref[0, pl.ds(start, tk), :]], axis=-1)
        s = _dot_nt(qc, kc)
        if masked:
            r = lax.broadcasted_iota(jnp.int32, (rows, tk), 0)
            tok = i * tq + (r & (tq - 1))
            kpos = j * tk + lax.broadcasted_iota(jnp.int32, (rows, tk), 1)
            s = jnp.where(kpos <= tok, s, NEG_INF)
        m_prev = m_ref[...]
        m_new = jnp.maximum(m_prev, jnp.max(s, axis=-1, keepdims=True))
        alpha = jnp.exp(m_prev - m_new)
        p = jnp.exp(s - jnp.tile(m_new, (1, tk // LANES)))
        l_ref[...] = alpha * l_ref[...] + jnp.sum(p, axis=-1, keepdims=True)
        acc_ref[...] = alpha * acc_ref[...] + _dot(p.astype(BF16), v_ref[0, pl.ds(start, tk), :])
        m_ref[...] = m_new

    n_full = (i * tq) // tk

    def body(j, carry):
        step(j, False)
        return carry

    lax.fori_loop(0, n_full, body, 0)
    for dd in range(max(1, tq // tk)):
        step(n_full + dd, True)
    out = acc_ref[...] / l_ref[...]
    o_ref[0] = out.reshape(hs, tq, LANES).astype(o_ref.dtype)


def _attn_call(q, qa, k, ka, v, *, tq, tk, out_dtype, ka_shared=False):
    g, hs, s, _ = q.shape
    assert tq & (tq - 1) == 0 and (tq % tk == 0 or tk % tq == 0) and s % tq == 0 and s % tk == 0
    rows = hs * tq
    seq = lambda gg, i: (gg, 0, 0)
    ka_map = (lambda gg, i: (0, 0, 0)) if ka_shared else seq
    return pl.pallas_call(
        functools.partial(_attn_kernel, hs=hs, tq=tq, tk=tk),
        grid=(g, s // tq),
        in_specs=[pl.BlockSpec((1, hs, tq, LANES), lambda gg, i: (gg, 0, i, 0)),
                  pl.BlockSpec((1, tq, LANES), lambda gg, i: (gg, i, 0)),
                  pl.BlockSpec((1, s, LANES), seq),
                  pl.BlockSpec((1, s, LANES), ka_map),
                  pl.BlockSpec((1, s, LANES), seq)],
        out_specs=pl.BlockSpec((1, hs, tq, LANES), lambda gg, i: (gg, 0, i, 0)),
        out_shape=jax.ShapeDtypeStruct((g, hs, s, LANES), out_dtype),
        scratch_shapes=[pltpu.VMEM((rows, LANES), F32), pltpu.VMEM((rows, LANES), F32),
                        pltpu.VMEM((rows, LANES), F32)],
        compiler_params=_params(2),
        name="flash_attn",
    )(q, qa, k, ka, v)


def _compress_kernel(x_ref, pe_ref, w1_ref, w2_ref, o_ref):
    x = x_ref[0, 0]
    half = x.shape[1]
    pe = pe_ref[0]
    top = _dot((x + pe[:, :half]).astype(BF16), w1_ref[0, :half, :])
    bot = _dot((x + pe[:, half:]).astype(BF16), w1_ref[0, half:, :])
    n = x.shape[0]
    hid = top + pltpu.roll(bot, n - 1, 0)
    act = 0.5 * hid * (1.0 + jnp.tanh(0.7978845608028654 * (hid + 0.044715 * hid * hid * hid)))
    o_ref[0, 0] = _dot(act.astype(BF16), w2_ref[0]).astype(o_ref.dtype)


def _compress_call(cv, pe, w1, w2):
    b, _, s, _ = cv.shape
    n = s // NSA_CMP_STRIDE
    width = NSA_CMP_STRIDE * LANES
    x2 = cv.reshape(b, 2, n, width)
    return pl.pallas_call(
        _compress_kernel,
        grid=(b, 2),
        in_specs=[pl.BlockSpec((1, 1, n, width), lambda bb, t: (bb, t, 0, 0)),
                  pl.BlockSpec((1, 1, 2 * width), lambda bb, t: (t, 0, 0)),
                  pl.BlockSpec((1, 2 * width, NSA_CMP_HIDDEN), lambda bb, t: (t, 0, 0)),
                  pl.BlockSpec((1, NSA_CMP_HIDDEN, LANES), lambda bb, t: (t, 0, 0))],
        out_specs=pl.BlockSpec((1, 1, n, LANES), lambda bb, t: (bb, t, 0, 0)),
        out_shape=jax.ShapeDtypeStruct((b, 2, n, LANES), BF16),
        compiler_params=_params(2),
        name="nsa_compress",
    )(x2, pe, w1, w2)


def _nsa_local_kernel(q_ref, kc_ref, vc_ref, ov_ref, kw_ref, vw_ref, small_ref,
                      ocw_ref, bias_ref, *, hn, tq, n_sb):
    i = pl.program_id(1)
    rows = hn * tq
    q = q_ref[0].reshape(rows, LANES)
    gates = _sigmoid(small_ref[0])
    nc = kc_ref.shape[2]
    s = _dot_nt(q, kc_ref[0, 0])
    r = lax.broadcasted_iota(jnp.int32, (rows, nc), 0)
    tok = i * tq + (r & (tq - 1))
    cend = lax.broadcasted_iota(jnp.int32, (rows, nc), 1) * NSA_CMP_STRIDE + (NSA_CMP_LEN - 1)
    s = jnp.where(cend <= tok, s, NEG_INF)
    p = jnp.exp(s - jnp.max(s, axis=-1, keepdims=True))
    p = p / jnp.sum(p, axis=-1, keepdims=True)
    p = jnp.where(tok >= NSA_CMP_LEN - 1, p, 0.0)
    o_cmp = _dot(p.astype(BF16), vc_ref[0, 0]).reshape(hn, tq, LANES)
    psum = jnp.sum(p.reshape(hn, tq, nc), axis=0)
    hi = psum.astype(BF16)
    lo = (psum - hi.astype(F32)).astype(BF16)
    imp = _dot(hi, ov_ref[...]) + _dot(lo, ov_ref[...])
    lane = lax.broadcasted_iota(jnp.int32, (tq, LANES), 1)
    cur = (i * tq + lax.broadcasted_iota(jnp.int32, (tq, LANES), 0)) >> (NSA_SEL_BLOCK.bit_length() - 1)
    forced = (lane == 0) | (lane == cur) | (lane == cur - 1)
    score = jnp.where((lane <= cur) & jnp.logical_not(forced), imp, -jnp.inf)
    sel = forced
    for _ in range(min(NSA_N_SEL, n_sb) - 3):
        pick, m = _first_max_pick(score, lane)
        pick = pick & (m > -jnp.inf)
        sel = sel | pick
        score = jnp.where(pick, -jnp.inf, score)
    bias_ref[0] = jnp.where(sel, 0.0, NEG_INF).astype(BF16)
    span = NSA_WINDOW + tq
    start = pl.multiple_of(jnp.maximum(i * tq - NSA_WINDOW, 0), tq)
    sw = _dot_nt(q, kw_ref[0, pl.ds(start, span), :])
    rw = lax.broadcasted_iota(jnp.int32, (rows, span), 0)
    tokw = i * tq + (rw & (tq - 1))
    kpos = start + lax.broadcasted_iota(jnp.int32, (rows, span), 1)
    sw = jnp.where((kpos <= tokw) & (kpos > tokw - NSA_WINDOW), sw, NEG_INF)
    pw = jnp.exp(sw - jnp.max(sw, axis=-1, keepdims=True))
    pw = pw / jnp.sum(pw, axis=-1, keepdims=True)
    o_win = _dot(pw.astype(BF16), vw_ref[0, pl.ds(start, span), :]).reshape(hn, tq, LANES)
    for hh in range(hn):
        ocw_ref[0, hh] = (_lane_col(gates, hh * NSA_N_BRANCH) * o_cmp[hh]
                          + _lane_col(gates, hh * NSA_N_BRANCH + 2) * o_win[hh])


def _nsa_local_call(q, kvc, overlap, kw, vw, small, *, tq):
    b, hn, s, _ = q.shape
    nc = kvc.shape[2]
    n_sb = s // NSA_SEL_BLOCK
    assert n_sb <= LANES and tq & (tq - 1) == 0 and NSA_WINDOW % tq == 0
    seq = lambda bb, i: (bb, 0, 0)
    return pl.pallas_call(
        functools.partial(_nsa_local_kernel, hn=hn, tq=tq, n_sb=n_sb),
        grid=(b, s // tq),
        in_specs=[pl.BlockSpec((1, hn, tq, LANES), lambda bb, i: (bb, 0, i, 0)),
                  pl.BlockSpec((1, 1, nc, LANES), lambda bb, i: (bb, 0, 0, 0)),
                  pl.BlockSpec((1, 1, nc, LANES), lambda bb, i: (bb, 1, 0, 0)),
                  pl.BlockSpec((nc, LANES), lambda bb, i: (0, 0)),
                  pl.BlockSpec((1, s, LANES), seq),
                  pl.BlockSpec((1, s, LANES), seq),
                  pl.BlockSpec((1, tq, LANES), lambda bb, i: (bb, i, 0))],
        out_specs=[pl.BlockSpec((1, hn, tq, LANES), lambda bb, i: (bb, 0, i, 0)),
                   pl.BlockSpec((1, tq, LANES), lambda bb, i: (bb, i, 0))],
        out_shape=[jax.ShapeDtypeStruct((b, hn, s, LANES), F32),
                   jax.ShapeDtypeStruct((b, s, LANES), BF16)],
        compiler_params=_params(2),
        name="nsa_local",
    )(q, kvc, kvc, overlap, kw, vw, small)


def _outproj_kernel(om_ref, ocw_ref, osl_ref, small_ref, of_ref, w_ref, x_ref,
                    gpost_ref, gate_ref, gpre_ref, sh_ref, sc_ref, xo_ref, h_ref,
                    *, hm, hn, hf):
    gates = _sigmoid(small_ref[0])
    parts = [om_ref[0, hh] for hh in range(hm)]
    for hh in range(hn):
        o = ocw_ref[0, hh] + _lane_col(gates, hh * NSA_N_BRANCH + 1) * osl_ref[0, hh]
        parts.append(o.astype(BF16))
    parts += [of_ref[0, hh] for hh in range(hf)]
    y = _dot(jnp.concatenate(parts, axis=-1), w_ref[...])
    xn = x_ref[0] + gate_ref[0] * _rms(y, gpost_ref[...])
    xo_ref[0] = xn
    h_ref[0] = (_rms(xn, gpre_ref[...]) * (1.0 + sc_ref[0]) + sh_ref[0]).astype(BF16)


def _outproj_call(o_moba, o_cw, o_slc, small, o_fox, w_out, x, g_post, gate, g_pre, shift, scale,
                  *, tm):
    b, s, d = x.shape
    hm, hn, hf = o_moba.shape[1], o_cw.shape[1], o_fox.shape[1]
    head_spec = lambda n: pl.BlockSpec((1, n, tm, LANES), lambda bb, i: (bb, 0, i, 0))
    row = lambda bb, i: (bb, i, 0)
    vec = pl.BlockSpec((1, 1, d), lambda bb, i: (bb, 0, 0))
    gain = pl.BlockSpec((1, d), lambda bb, i: (0, 0))
    return pl.pallas_call(
        functools.partial(_outproj_kernel, hm=hm, hn=hn, hf=hf),
        grid=(b, s // tm),
        in_specs=[head_spec(hm), head_spec(hn), head_spec(hn),
                  pl.BlockSpec((1, tm, LANES), row), head_spec(hf),
                  _resident(w_out.shape, lambda bb, i: (0, 0)),
                  pl.BlockSpec((1, tm, d), row), gain, vec, gain, vec, vec],
        out_specs=[pl.BlockSpec((1, tm, d), row), pl.BlockSpec((1, tm, d), row)],
        out_shape=[jax.ShapeDtypeStruct((b, s, d), F32), jax.ShapeDtypeStruct((b, s, d), BF16)],
        compiler_params=_params(2),
        name="outproj",
    )(o_moba, o_cw, o_slc, small, o_fox, w_out, x, g_post, gate, g_pre, shift, scale)


def _mlp_kernel(h_ref, wu_ref, wd_ref, x_ref, g_ref, gate_ref, o_ref, acc_ref):
    f = pl.program_id(2)
    u = jnp.maximum(_dot(h_ref[0], wu_ref[...]), 0.0)
    part = _dot((u * u).astype(BF16), wd_ref[...])

    @pl.when(f == 0)
    def _():
        acc_ref[...] = part

    @pl.when(f > 0)
    def _():
        acc_ref[...] += part

    @pl.when(f == pl.num_programs(2) - 1)
    def _():
        o_ref[0] = x_ref[0] + gate_ref[0] * _rms(acc_ref[...], g_ref[...])


def _mlp_call(h, w_up, w_down, x, g_post, gate, *, tm, tf):
    b, s, d = x.shape
    dff = w_up.shape[1]
    row = lambda bb, i, f: (bb, i, 0)
    return pl.pallas_call(
        _mlp_kernel,
        grid=(b, s // tm, dff // tf),
        in_specs=[pl.BlockSpec((1, tm, d), row),
                  pl.BlockSpec((d, tf), lambda bb, i, f: (0, f)),
                  pl.BlockSpec((tf, d), lambda bb, i, f: (f, 0)),
                  pl.BlockSpec((1, tm, d), row),
                  pl.BlockSpec((1, d), lambda bb, i, f: (0, 0)),
                  pl.BlockSpec((1, 1, d), lambda bb, i, f: (bb, 0, 0))],
        out_specs=pl.BlockSpec((1, tm, d), row),
        out_shape=jax.ShapeDtypeStruct((b, s, d), F32),
        scratch_shapes=[pltpu.VMEM((tm, d), F32)],
        compiler_params=_params(3),
        name="mlp",
    )(h, w_up, w_down, x, g_post, gate)


def _tile(n, pref):
    t = min(n, pref)
    assert n % t == 0
    return t


def _rope_tables(positions):
    inv_freq = ROPE_THETA ** (-jnp.arange(ROPE_HALF, dtype=F32) / ROPE_HALF)
    ang = positions.astype(F32)[..., None] * inv_freq
    cos, sin = jnp.cos(ang), jnp.sin(ang)
    ones = jnp.ones(ang.shape[:-1] + (LANES - ROPE_DIMS,), F32)
    cos_t = jnp.concatenate([cos, cos, ones], axis=-1)
    sin_t = jnp.concatenate([-sin, sin, 0.0 * ones], axis=-1)
    return cos_t, sin_t


def _block_onehot(s, block):
    ids = np.arange(s) // block
    return jnp.asarray((ids[:, None] == np.arange(LANES)[None, :]).astype(np.float32), dtype=BF16)[None]


def _overlap_matrix(s):
    nc = s // NSA_CMP_STRIDE
    n_cmp = (s - NSA_CMP_LEN) // NSA_CMP_STRIDE + 1
    c = np.arange(nc)
    first = (c * NSA_CMP_STRIDE) // NSA_SEL_BLOCK
    last = (c * NSA_CMP_STRIDE + NSA_CMP_LEN - 1) // NSA_SEL_BLOCK
    sb = np.arange(LANES)
    ov = (sb[None, :] >= first[:, None]) & (sb[None, :] <= last[:, None]) & (c[:, None] < n_cmp)
    return jnp.asarray(ov.astype(np.float32), dtype=BF16)


def kernel(x, c, positions, w_mod, b_mod, g_pre_mix, g_post_mix, g_pre_mlp, g_post_mlp, w_in,
           b_forget, cmp_pe_k, cmp_pe_v, cmp_w1_k, cmp_w2_k, cmp_w1_v, cmp_w2_v, w_out, w_up, w_down):
    b, s, d = x.shape
    depth = w_mod.shape[0]
    n_heads = d // HEAD_DIM
    hm = n_heads // 4
    hn = n_heads // 4
    hf = n_heads - hm - hn
    mw, nw, fw = hm * HEAD_DIM, hn * HEAD_DIM, hf * HEAD_DIM
    n_gate = hn * NSA_N_BRANCH
    o_g = 3 * mw + nw + NSA_N_KV * HEAD_DIM
    o_f = o_g + n_gate
    o_ff = o_f + 3 * fw
    assert n_gate + hf <= LANES and s % MOBA_BLOCK == 0 and s // MOBA_BLOCK <= LANES

    c_pad = jnp.zeros((8, d), F32).at[:b].set(c)
    mod = _mod_call(c_pad, w_mod, b_mod)
    cos_t, sin_t = _rope_tables(positions)
    moba_ka = _block_onehot(s, MOBA_BLOCK)
    slc_ka = _block_onehot(s, NSA_SEL_BLOCK)
    overlap = _overlap_matrix(s)
    nb = s // MOBA_BLOCK

    tm_in = _tile(s, 512)
    for l in range(depth):
        sh_a, sc_a, gt_a, sh_m, sc_m, gt_m = (
            mod[l, :b, k * d:(k + 1) * d].reshape(b, 1, d) for k in range(6))
        w = w_in[l]
        w_cat = jnp.concatenate(
            [w[:, :o_g], w[:, o_f:o_ff], w[:, o_g:o_f], w[:, o_ff:],
             jnp.zeros((d, LANES - n_gate - hf), F32)], axis=1).astype(BF16)
        (mq, mk, mv, kmean, nq, cv, ks, vs, kw, vw, small, fq, fk, fv) = _inproj_call(
            x, g_pre_mix[l][None], sh_a, sc_a, cos_t, sin_t, w_cat, hm=hm, hn=hn, hf=hf, tm=tm_in)

        bf_row = jnp.zeros((1, LANES), F32).at[0, n_gate:n_gate + hf].set(b_forget[l])
        fqa, fka = _fox_prep_call(small, bf_row, hf=hf, col0=n_gate, ts=_tile(s, 512))
        fold = lambda t: t.reshape(b * t.shape[1], 1, s, LANES)
        flat = lambda t: t.reshape(b * t.shape[1], s, LANES)
        o_fox = _attn_call(fold(fq), flat(fqa), flat(fk), flat(fka), flat(fv),
                           tq=_tile(s, 512), tk=_tile(s, 512), out_dtype=BF16).reshape(b, hf, s, LANES)

        kmean = kmean.reshape(b, s // tm_in, hm, tm_in // MOBA_BLOCK, LANES)
        kmean = kmean.transpose(0, 2, 1, 3, 4).reshape(b * hm, nb, LANES)
        kmean = jnp.pad(kmean, ((0, 0), (0, LANES - nb), (0, 0)))
        mqa = _moba_select_call(fold(mq), kmean, nb=nb, tq=_tile(s, 512))
        o_moba = _attn_call(fold(mq), mqa, flat(mk), moba_ka, flat(mv), tq=_tile(s, 512),
                            tk=_tile(s, 512), out_dtype=BF16, ka_shared=True).reshape(b, hm, s, LANES)

        pe = jnp.stack([cmp_pe_k[l].reshape(1, -1), cmp_pe_v[l].reshape(1, -1)])
        w1 = jnp.stack([cmp_w1_k[l], cmp_w1_v[l]]).astype(BF16)
        w2 = jnp.stack([cmp_w2_k[l], cmp_w2_v[l]]).astype(BF16)
        kvc = _compress_call(cv, pe, w1, w2)
        o_cw, sbias = _nsa_local_call(nq, kvc, overlap, kw, vw, small, tq=_tile(s, 128))
        o_slc = _attn_call(nq, sbias, ks, slc_ka, vs, tq=_tile(s, 128), tk=_tile(s, 512),
                           out_dtype=F32, ka_shared=True)

        x, h_mlp = _outproj_call(o_moba, o_cw, o_slc, small, o_fox, w_out[l].astype(BF16), x,
                                 g_post_mix[l][None], gt_a, g_pre_mlp[l][None], sh_m, sc_m,
                                 tm=_tile(s, 256))
        x = _mlp_call(h_mlp, w_up[l].astype(BF16), w_down[l].astype(BF16), x,
                      g_post_mlp[l][None], gt_m, tm=_tile(s, 512), tf=512)
    return x
```

```python
import functools

import jax
import jax.numpy as jnp
import numpy as np
from jax import lax
from jax.experimental import pallas as pl
from jax.experimental.pallas import tpu as pltpu

HEAD_DIM = 128
ROPE_THETA = 500000.0
ROPE_DIMS = HEAD_DIM // 4
ROPE_HALF = ROPE_DIMS // 2
MOBA_BLOCK = 256
MOBA_TOPK = 3
NSA_CMP_LEN = 32
NSA_CMP_STRIDE = 16
NSA_CMP_HIDDEN = 256
NSA_SEL_BLOCK = 64
NSA_N_SEL = 16
NSA_WINDOW = 512
NSA_N_BRANCH = 3
NSA_N_KV = 6
NORM_EPS = 1e-6
NEG_INF = -1e30
ATTN_SCALE = HEAD_DIM ** -0.5
LOG2E = 1.4426950408889634
Q_SCALE = ATTN_SCALE * LOG2E

LANES = 128
VMEM_LIMIT_BYTES = 56 * 1024 * 1024

F32 = jnp.float32
BF16 = jnp.bfloat16


def _params(n_axes):
    return pltpu.CompilerParams(dimension_semantics=("arbitrary",) * n_axes,
                                vmem_limit_bytes=VMEM_LIMIT_BYTES)


def _resident(block_shape, index_map):
    return pl.BlockSpec(block_shape, index_map, pipeline_mode=pl.Buffered(1))


def _split3(x):
    p1 = x.astype(BF16)
    r1 = x - p1.astype(F32)
    p2 = r1.astype(BF16)
    p3 = (r1 - p2.astype(F32)).astype(BF16)
    return p1, p2, p3


def _dot(a, b):
    return jnp.dot(a, b, preferred_element_type=F32)


def _dot_nt(a, b):
    return lax.dot_general(a, b, (((1,), (1,)), ((), ())), preferred_element_type=F32)


def _rms(x, g):
    return x * lax.rsqrt(jnp.mean(x * x, axis=-1, keepdims=True) + NORM_EPS) * g


def _sigmoid(z):
    return 1.0 / (1.0 + jnp.exp(-z))


def _lane_col(x, c):
    return jnp.broadcast_to(x[:, c:c + 1], x.shape)


def _mod_kernel(c_ref, w_ref, b_ref, o_ref):
    c = c_ref[...]
    ca = c * _sigmoid(c)
    o_ref[0] = jnp.dot(ca, w_ref[0], precision=lax.Precision.HIGHEST,
                       preferred_element_type=F32) + b_ref[0]


def _mod_call(c_pad, w_mod, b_mod):
    depth, d, n = w_mod.shape
    rows = c_pad.shape[0]
    tn = 1024
    assert n % tn == 0
    return pl.pallas_call(
        _mod_kernel,
        grid=(depth, n // tn),
        in_specs=[pl.BlockSpec((rows, d), lambda l, j: (0, 0)),
                  pl.BlockSpec((1, d, tn), lambda l, j: (l, 0, j)),
                  pl.BlockSpec((1, 1, tn), lambda l, j: (l, 0, j))],
        out_specs=pl.BlockSpec((1, rows, tn), lambda l, j: (l, 0, j)),
        out_shape=jax.ShapeDtypeStruct((depth, rows, n), F32),
        compiler_params=_params(2),
        name="adaln_mod",
    )(c_pad, w_mod, b_mod.reshape(depth, 1, n))


def _wcat_kernel(w_ref, o_ref, *, o_g, o_f, o_ff):
    x = w_ref[0]
    tr, n_in = x.shape
    n_small = (o_f - o_g) + (n_in - o_ff)
    small = jnp.concatenate([x[:, o_g:o_f], x[:, o_ff:], jnp.zeros((tr, LANES - n_small), F32)], axis=1)
    o_ref[0, :, :o_g] = x[:, :o_g].astype(BF16)
    o_ref[0, :, o_g:o_g + (o_ff - o_f)] = x[:, o_f:o_ff].astype(BF16)
    o_ref[0, :, o_g + (o_ff - o_f):] = small.astype(BF16)


def _wcat_call(w_in, *, o_g, o_f, o_ff, tr):
    depth, d, n_in = w_in.shape
    npad = o_g + (o_ff - o_f) + LANES
    assert o_g % LANES == 0 and (o_ff - o_f) % LANES == 0
    return pl.pallas_call(
        functools.partial(_wcat_kernel, o_g=o_g, o_f=o_f, o_ff=o_ff),
        grid=(depth, d // tr),
        in_specs=[pl.BlockSpec((1, tr, n_in), lambda l, i: (l, i, 0))],
        out_specs=pl.BlockSpec((1, tr, npad), lambda l, i: (l, i, 0)),
        out_shape=jax.ShapeDtypeStruct((depth, d, npad), BF16),
        compiler_params=_params(2),
        name="w_in_layout",
    )(w_in)


def _inproj_kernel(x_ref, g_ref, sh_ref, sc_ref, cos_ref, sin_ref, w_ref,
                   mq_ref, mk_ref, mv_ref, kmean_ref, nq_ref, cv_ref,
                   ks_ref, vs_ref, kw_ref, vw_ref, small_ref, fq_ref, fk_ref, fv_ref,
                   *, hm, hn, hf):
    x = x_ref[0]
    h = _rms(x, g_ref[0]) * (1.0 + sc_ref[0, 0, 0]) + sh_ref[0, 0, 0]
    hb = h.astype(BF16)
    tm = x.shape[0]
    cos = cos_ref[0]
    sin = sin_ref[0]
    lane = lax.broadcasted_iota(jnp.int32, (tm, LANES), 1)

    def rope(y):
        swapped = jnp.where(lane < ROPE_HALF, pltpu.roll(y, LANES - ROPE_HALF, 1),
                            pltpu.roll(y, ROPE_HALF, 1))
        return y * cos + swapped * sin

    def heads(col0, n):
        outs = []
        j = 0
        while j < n:
            w = 2 if j + 1 < n else 1
            y = _dot(hb, w_ref[0, :, (col0 + j) * LANES:(col0 + j + w) * LANES])
            for t in range(w):
                outs.append(y[:, t * LANES:(t + 1) * LANES])
            j += w
        return outs

    col = 0
    for hh, y in enumerate(heads(col, hm)):
        mq_ref[0, hh] = (rope(y) * Q_SCALE).astype(BF16)
    col += hm
    nblk = tm // MOBA_BLOCK
    means = []
    for hh, y in enumerate(heads(col, hm)):
        yr = rope(y)
        mk_ref[0, hh] = yr.astype(BF16)
        means.append(jnp.mean(yr.reshape(nblk, MOBA_BLOCK, LANES), axis=1))
    kmean_ref[0, 0] = jnp.concatenate(means, axis=0)
    col += hm
    for hh, y in enumerate(heads(col, hm)):
        mv_ref[0, hh] = y.astype(BF16)
    col += hm
    for hh, y in enumerate(heads(col, hn)):
        nq_ref[0, hh] = (rope(y) * Q_SCALE).astype(BF16)
    col += hn
    kc, vc, ks, vs, kw, vw = heads(col, NSA_N_KV)
    cv_ref[0, 0] = rope(kc)
    cv_ref[0, 1] = vc
    ks_ref[0] = rope(ks).astype(BF16)
    vs_ref[0] = vs.astype(BF16)
    kw_ref[0] = rope(kw).astype(BF16)
    vw_ref[0] = vw.astype(BF16)
    col += NSA_N_KV
    for hh, y in enumerate(heads(col, hf)):
        fq_ref[0, hh] = (y * Q_SCALE).astype(BF16)
    col += hf
    for hh, y in enumerate(heads(col, hf)):
        fk_ref[0, hh] = y.astype(BF16)
    col += hf
    for hh, y in enumerate(heads(col, hf)):
        fv_ref[0, hh] = y.astype(BF16)
    col += hf
    small_ref[0] = heads(col, 1)[0]


def _layer_vec(l, d):
    return pl.BlockSpec((1, 1, d), lambda *_: (l, 0, 0))


def _mod_vec(l, k, d):
    return pl.BlockSpec((1, 1, 1, 1, d), lambda bb, *_: (l, k, bb, 0, 0))


def _inproj_call(x, g, mod6, cos_t, sin_t, w_cat, *, l, hm, hn, hf, tm):
    b, s, d = x.shape
    npad = w_cat.shape[2]
    nblk = tm // MOBA_BLOCK
    row = lambda bb, i: (bb, i, 0)
    head_spec = lambda n: pl.BlockSpec((1, n, tm, LANES), lambda bb, i: (bb, 0, i, 0))
    head_shape = lambda n, dt=BF16: jax.ShapeDtypeStruct((b, n, s, LANES), dt)
    tok_spec = pl.BlockSpec((1, tm, LANES), row)
    tok_shape = lambda dt=BF16: jax.ShapeDtypeStruct((b, s, LANES), dt)
    out_specs = [head_spec(hm), head_spec(hm), head_spec(hm),
                 pl.BlockSpec((1, 1, hm * nblk, LANES), lambda bb, i: (bb, i, 0, 0)),
                 head_spec(hn), head_spec(2),
                 tok_spec, tok_spec, tok_spec, tok_spec, tok_spec,
                 head_spec(hf), head_spec(hf), head_spec(hf)]
    out_shape = [head_shape(hm), head_shape(hm), head_shape(hm),
                 jax.ShapeDtypeStruct((b, s // tm, hm * nblk, LANES), F32),
                 head_shape(hn), head_shape(2, F32),
                 tok_shape(), tok_shape(), tok_shape(), tok_shape(), tok_shape(F32),
                 head_shape(hf), head_shape(hf), head_shape(hf)]
    return pl.pallas_call(
        functools.partial(_inproj_kernel, hm=hm, hn=hn, hf=hf),
        grid=(b, s // tm),
        in_specs=[pl.BlockSpec((1, tm, d), row),
                  _layer_vec(l, d), _mod_vec(l, 0, d), _mod_vec(l, 1, d),
                  pl.BlockSpec((1, tm, LANES), row),
                  pl.BlockSpec((1, tm, LANES), row),
                  _resident((1, d, npad), lambda bb, i: (l, 0, 0))],
        out_specs=out_specs,
        out_shape=out_shape,
        compiler_params=_params(2),
        name="inproj",
    )(x, g, mod6, mod6, cos_t, sin_t, w_cat)


def _fox_prep_kernel(small_ref, bf_ref, qa_ref, ka_ref, carry_ref, *, hf, col0):
    i = pl.program_id(1)

    @pl.when(i == 0)
    def _():
        carry_ref[...] = jnp.zeros_like(carry_ref)

    z = small_ref[0] + bf_ref[0]
    logf = jnp.minimum(z, 0.0) - jnp.log1p(jnp.exp(-jnp.abs(z)))
    ts = z.shape[0]
    r = lax.broadcasted_iota(jnp.int32, (ts, ts), 0)
    c = lax.broadcasted_iota(jnp.int32, (ts, ts), 1)
    tri = jnp.where(r >= c, 1.0, 0.0).astype(BF16)
    p1, p2, p3 = _split3(logf)
    cum = _dot(tri, p1) + _dot(tri, p2) + _dot(tri, p3) + carry_ref[...]
    carry_ref[...] = cum[ts - 1:ts, :]
    lane = lax.broadcasted_iota(jnp.int32, (ts, LANES), 1)
    for hh in range(hf):
        cb = _lane_col(cum, col0 + hh) * LOG2E
        c1, c2, c3 = (p.astype(F32) for p in _split3(cb))
        qa = jnp.where(lane == 0, c1, jnp.where(lane == 1, c2, jnp.where(
            lane == 2, c3, jnp.where(lane < 6, 1.0, 0.0))))
        ka = jnp.where(lane < 3, 1.0, jnp.where(lane == 3, -c1, jnp.where(
            lane == 4, -c2, jnp.where(lane == 5, -c3, 0.0))))
        qa_ref[0, hh] = qa.astype(BF16)
        ka_ref[0, hh] = ka.astype(BF16)


def _fox_prep_call(small, bf_rows, *, l, hf, col0, ts):
    b, s, _ = small.shape
    spec = pl.BlockSpec((1, hf, ts, LANES), lambda bb, i: (bb, 0, i, 0))
    shape = jax.ShapeDtypeStruct((b, hf, s, LANES), BF16)
    return pl.pallas_call(
        functools.partial(_fox_prep_kernel, hf=hf, col0=col0),
        grid=(b, s // ts),
        in_specs=[pl.BlockSpec((1, ts, LANES), lambda bb, i: (bb, i, 0)),
                  _layer_vec(l, LANES)],
        out_specs=[spec, spec],
        out_shape=[shape, shape],
        scratch_shapes=[pltpu.VMEM((1, LANES), F32)],
        compiler_params=_params(2),
        name="fox_prep",
    )(small, bf_rows)


def _first_max_pick(score, lane):
    m = jnp.max(score, axis=-1, keepdims=True)
    lane_f = lane.astype(F32)
    idx = jnp.min(jnp.where(score == m, lane_f, float(LANES)), axis=-1, keepdims=True)
    return lane_f == idx, m


def _moba_select_kernel(q_ref, km_ref, o_ref, *, nb, top):
    i = pl.program_id(1)
    q = q_ref[0, 0]
    tq = q.shape[0]
    k1, k2, k3 = _split3(km_ref[0])
    gate = _dot_nt(q, k1) + _dot_nt(q, k2) + _dot_nt(q, k3)
    lane = lax.broadcasted_iota(jnp.int32, (tq, LANES), 1)
    tok = i * tq + lax.broadcasted_iota(jnp.int32, (tq, LANES), 0)
    own = tok >> (MOBA_BLOCK.bit_length() - 1)
    past = lane < own
    score = jnp.where(lane < nb, jnp.where(past, gate, NEG_INF), -jnp.inf)
    sel = lane == own
    for _ in range(top):
        pick, _m = _first_max_pick(score, lane)
        sel = sel | (pick & past)
        score = jnp.where(pick, -jnp.inf, score)
    o_ref[0] = jnp.where(sel | (lane >= nb), 0.0, NEG_INF).astype(BF16)


def _moba_select_call(q, kmean_pad, *, nb, tq):
    g, _, s, _ = q.shape
    top = min(MOBA_TOPK, max(nb - 1, 1))
    return pl.pallas_call(
        functools.partial(_moba_select_kernel, nb=nb, top=top),
        grid=(g, s // tq),
        in_specs=[pl.BlockSpec((1, 1, tq, LANES), lambda gg, i: (gg, 0, i, 0)),
                  pl.BlockSpec((1, LANES, LANES), lambda gg, i: (gg, 0, 0))],
        out_specs=pl.BlockSpec((1, tq, LANES), lambda gg, i: (gg, i, 0)),
        out_shape=jax.ShapeDtypeStruct((g, s, LANES), BF16),
        compiler_params=_params(2),
        name="moba_select",
    )(q, kmean_pad)


def _attn_kernel(q_ref, qa_ref, k_ref, ka_ref, v_ref, o_ref,
                 kc_ref, vc_ref, s0_ref, s1_ref, m_ref, acc_ref, *, hs, tq, tk):
    rows = hs * tq
    seq = k_ref.shape[1]
    kc_ref[:, :LANES] = k_ref[0]
    kc_ref[:, LANES:] = ka_ref[0]
    vc_ref[:, :LANES] = v_ref[0]
    vc_ref[:, LANES:] = jnp.ones((seq, LANES), BF16)
    row_in_tile = lax.broadcasted_iota(jnp.int32, (rows, LANES), 0) & (tq - 1)
    col = lax.broadcasted_iota(jnp.int32, (rows, tk), 1)

    def logits(i, j, s_ref):
        qs = pl.multiple_of(i * tq, tq)
        qc = jnp.concatenate([q_ref[0, :, pl.ds(qs, tq), :].reshape(rows, LANES),
                              jnp.concatenate([qa_ref[0, pl.ds(qs, tq), :]] * hs, axis=0)], axis=-1)
        s_ref[...] = _dot_nt(qc, kc_ref[pl.ds(pl.multiple_of(j * tk, tk), tk), :])

    def update(i, j, s_ref):
        limit = jnp.tile(row_in_tile + (i * tq - j * tk), (1, tk // LANES))
        s = jnp.where(col <= limit, s_ref[...], NEG_INF)
        m_prev = jnp.where(j == 0, -jnp.inf, m_ref[...])
        m_new = jnp.maximum(m_prev, jnp.max(s, axis=-1, keepdims=True))
        alpha = jnp.exp2(m_prev - m_new)
        p = jnp.exp2(s - jnp.tile(m_new, (1, tk // LANES))).astype(BF16)
        m_ref[...] = m_new
        acc = (jnp.tile(alpha, (1, 2)) * acc_ref[...]
               + _dot(p, vc_ref[pl.ds(pl.multiple_of(j * tk, tk), tk), :]))
        acc_ref[...] = acc
        out = (acc[:, :LANES] / acc[:, LANES:]).astype(o_ref.dtype)
        qs = pl.multiple_of(i * tq, tq)
        for hh in range(hs):
            o_ref[0, hh, pl.ds(qs, tq), :] = out[hh * tq:(hh + 1) * tq]

    def advance(i, j):
        wrap = j == (i * tq) // tk
        return jnp.where(wrap, i + 1, i), jnp.where(wrap, 0, j + 1)

    n_pairs = sum((i * tq) // tk + 1 for i in range(seq // tq))
    zero = jnp.int32(0)
    acc_ref[...] = jnp.zeros_like(acc_ref)
    m_ref[...] = jnp.zeros_like(m_ref)
    logits(zero, zero, s0_ref)

    def trip(carry, s_cur, s_next):
        ib, jb, ia, ja = carry
        logits(ia, ja, s_next)
        update(ib, jb, s_cur)
        return (ia, ja) + advance(ia, ja)

    def body(_, carry):
        return trip(trip(carry, s0_ref, s1_ref), s1_ref, s0_ref)

    carry = lax.fori_loop(0, (n_pairs - 1) // 2, body, (zero, zero) + advance(zero, zero))
    if (n_pairs - 1) % 2:
        carry = trip(carry, s0_ref, s1_ref)
        update(carry[0], carry[1], s1_ref)
    else:
        update(carry[0], carry[1], s0_ref)


def _attn_call(q, qa, k, ka, v, *, tq, tk, out_dtype, kv_groups=1, ka_shared=False):
    g, hs, s, _ = q.shape
    assert tq & (tq - 1) == 0 and tk % tq == 0 and s % tk == 0
    rows = hs * tq
    seq = lambda gg: (gg // kv_groups, 0, 0)
    ka_map = (lambda gg: (0, 0, 0)) if ka_shared else seq
    return pl.pallas_call(
        functools.partial(_attn_kernel, hs=hs, tq=tq, tk=tk),
        grid=(g,),
        in_specs=[pl.BlockSpec((1, hs, s, LANES), lambda gg: (gg, 0, 0, 0)),
                  pl.BlockSpec((1, s, LANES), seq),
                  pl.BlockSpec((1, s, LANES), seq),
                  pl.BlockSpec((1, s, LANES), ka_map),
                  pl.BlockSpec((1, s, LANES), seq)],
        out_specs=pl.BlockSpec((1, hs, s, LANES), lambda gg: (gg, 0, 0, 0)),
        out_shape=jax.ShapeDtypeStruct((g, hs, s, LANES), out_dtype),
        scratch_shapes=[pltpu.VMEM((s, 2 * LANES), BF16),
                        pltpu.VMEM((s, 2 * LANES), BF16),
                        pltpu.VMEM((rows, tk), F32),
                        pltpu.VMEM((rows, tk), F32),
                        pltpu.VMEM((rows, LANES), F32),
                        pltpu.VMEM((rows, 2 * LANES), F32)],
        compiler_params=_params(1),
        name="flash_attn",
    )(q, qa, k, ka, v)


def _compress_kernel(x_ref, pe_ref, w1_ref, w2_ref, o_ref):
    x = x_ref[0, 0]
    half = x.shape[1]
    pe = pe_ref[0, 0]
    top = _dot((x + pe[:, :half]).astype(BF16), w1_ref[0, 0, :half, :])
    bot = _dot((x + pe[:, half:]).astype(BF16), w1_ref[0, 0, half:, :])
    n = x.shape[0]
    hid = top + pltpu.roll(bot, n - 1, 0)
    act = 0.5 * hid * (1.0 + jnp.tanh(0.7978845608028654 * (hid + 0.044715 * hid * hid * hid)))
    o_ref[0, 0] = _dot(act.astype(BF16), w2_ref[0, 0]).astype(o_ref.dtype)


def _compress_call(cv, pe, w1, w2, *, l):
    b, _, s, _ = cv.shape
    n = s // NSA_CMP_STRIDE
    width = NSA_CMP_STRIDE * LANES
    x2 = cv.reshape(b, 2, n, width)
    return pl.pallas_call(
        _compress_kernel,
        grid=(b, 2),
        in_specs=[pl.BlockSpec((1, 1, n, width), lambda bb, t: (bb, t, 0, 0)),
                  pl.BlockSpec((1, 1, 1, 2 * width), lambda bb, t: (l, t, 0, 0)),
                  pl.BlockSpec((1, 1, 2 * width, NSA_CMP_HIDDEN), lambda bb, t: (l, t, 0, 0)),
                  pl.BlockSpec((1, 1, NSA_CMP_HIDDEN, LANES), lambda bb, t: (l, t, 0, 0))],
        out_specs=pl.BlockSpec((1, 1, n, LANES), lambda bb, t: (bb, t, 0, 0)),
        out_shape=jax.ShapeDtypeStruct((b, 2, n, LANES), BF16),
        compiler_params=_params(2),
        name="nsa_compress",
    )(x2, pe, w1, w2)


def _nsa_local_kernel(q_ref, kc_ref, vc_ref, ov_ref, kw_ref, vw_ref, small_ref,
                      ocw_ref, bias_ref, *, hn, tq, n_sb):
    i = pl.program_id(1)
    rows = hn * tq
    q = q_ref[0].reshape(rows, LANES)
    gates = _sigmoid(small_ref[0])
    nc = kc_ref.shape[2]
    s = _dot_nt(q, kc_ref[0, 0])
    r = lax.broadcasted_iota(jnp.int32, (rows, nc), 0)
    tok = i * tq + (r & (tq - 1))
    cend = lax.broadcasted_iota(jnp.int32, (rows, nc), 1) * NSA_CMP_STRIDE + (NSA_CMP_LEN - 1)
    s = jnp.where(cend <= tok, s, NEG_INF)
    p = jnp.exp2(s - jnp.max(s, axis=-1, keepdims=True))
    p = p / jnp.sum(p, axis=-1, keepdims=True)
    p = jnp.where(tok >= NSA_CMP_LEN - 1, p, 0.0)
    o_cmp = _dot(p.astype(BF16), vc_ref[0, 0]).reshape(hn, tq, LANES)
    psum = jnp.sum(p.reshape(hn, tq, nc), axis=0)
    hi = psum.astype(BF16)
    lo = (psum - hi.astype(F32)).astype(BF16)
    imp = _dot(hi, ov_ref[...]) + _dot(lo, ov_ref[...])
    lane = lax.broadcasted_iota(jnp.int32, (tq, LANES), 1)
    cur = (i * tq + lax.broadcasted_iota(jnp.int32, (tq, LANES), 0)) >> (NSA_SEL_BLOCK.bit_length() - 1)
    forced = (lane == 0) | (lane == cur) | (lane == cur - 1)
    score = jnp.where((lane <= cur) & jnp.logical_not(forced), imp, -jnp.inf)
    sel = forced
    for _ in range(min(NSA_N_SEL, n_sb) - 3):
        pick, m = _first_max_pick(score, lane)
        pick = pick & (m > -jnp.inf)
        sel = sel | pick
        score = jnp.where(pick, -jnp.inf, score)
    bias_ref[0] = jnp.where(sel, 0.0, NEG_INF).astype(BF16)
    span = NSA_WINDOW + tq
    start = pl.multiple_of(jnp.maximum(i * tq - NSA_WINDOW, 0), tq)
    sw = _dot_nt(q, kw_ref[0, pl.ds(start, span), :])
    rw = lax.broadcasted_iota(jnp.int32, (rows, span), 0)
    tokw = i * tq + (rw & (tq - 1))
    kpos = start + lax.broadcasted_iota(jnp.int32, (rows, span), 1)
    sw = jnp.where((kpos <= tokw) & (kpos > tokw - NSA_WINDOW), sw, NEG_INF)
    pw = jnp.exp2(sw - jnp.max(sw, axis=-1, keepdims=True))
    pw = pw / jnp.sum(pw, axis=-1, keepdims=True)
    o_win = _dot(pw.astype(BF16), vw_ref[0, pl.ds(start, span), :]).reshape(hn, tq, LANES)
    for hh in range(hn):
        ocw_ref[0, hh] = (_lane_col(gates, hh * NSA_N_BRANCH) * o_cmp[hh]
                          + _lane_col(gates, hh * NSA_N_BRANCH + 2) * o_win[hh])


def _nsa_local_call(q, kvc, overlap, kw, vw, small, *, tq):
    b, hn, s, _ = q.shape
    nc = kvc.shape[2]
    n_sb = s // NSA_SEL_BLOCK
    assert n_sb <= LANES and tq & (tq - 1) == 0 and NSA_WINDOW % tq == 0
    seq = lambda bb, i: (bb, 0, 0)
    return pl.pallas_call(
        functools.partial(_nsa_local_kernel, hn=hn, tq=tq, n_sb=n_sb),
        grid=(b, s // tq),
        in_specs=[pl.BlockSpec((1, hn, tq, LANES), lambda bb, i: (bb, 0, i, 0)),
                  pl.BlockSpec((1, 1, nc, LANES), lambda bb, i: (bb, 0, 0, 0)),
                  pl.BlockSpec((1, 1, nc, LANES), lambda bb, i: (bb, 1, 0, 0)),
                  pl.BlockSpec((nc, LANES), lambda bb, i: (0, 0)),
                  pl.BlockSpec((1, s, LANES), seq),
                  pl.BlockSpec((1, s, LANES), seq),
                  pl.BlockSpec((1, tq, LANES), lambda bb, i: (bb, i, 0))],
        out_specs=[pl.BlockSpec((1, hn, tq, LANES), lambda bb, i: (bb, 0, i, 0)),
                   pl.BlockSpec((1, tq, LANES), lambda bb, i: (bb, i, 0))],
        out_shape=[jax.ShapeDtypeStruct((b, hn, s, LANES), F32),
                   jax.ShapeDtypeStruct((b, s, LANES), BF16)],
        compiler_params=_params(2),
        name="nsa_local",
    )(q, kvc, kvc, overlap, kw, vw, small)


def _outproj_kernel(om_ref, ocw_ref, osl_ref, small_ref, of_ref, w_ref, x_ref,
                    gpost_ref, gate_ref, gpre_ref, sh_ref, sc_ref, xo_ref, h_ref,
                    *, hm, hn, hf):
    gates = _sigmoid(small_ref[0])
    parts = [om_ref[0, hh] for hh in range(hm)]
    for hh in range(hn):
        o = ocw_ref[0, hh] + _lane_col(gates, hh * NSA_N_BRANCH + 1) * osl_ref[0, hh]
        parts.append(o.astype(BF16))
    parts += [of_ref[0, hh] for hh in range(hf)]
    y = _dot(jnp.concatenate(parts, axis=-1), w_ref[0])
    xn = x_ref[0] + gate_ref[0, 0, 0] * _rms(y, gpost_ref[0])
    xo_ref[0] = xn
    h_ref[0] = (_rms(xn, gpre_ref[0]) * (1.0 + sc_ref[0, 0, 0]) + sh_ref[0, 0, 0]).astype(BF16)


def _outproj_call(o_moba, o_cw, o_slc, small, o_fox, w_out, x, g_post, g_pre, mod6, *, l, tm):
    b, s, d = x.shape
    hm, hn, hf = o_moba.shape[1], o_cw.shape[1], o_fox.shape[1]
    head_spec = lambda n: pl.BlockSpec((1, n, tm, LANES), lambda bb, i: (bb, 0, i, 0))
    row = lambda bb, i: (bb, i, 0)
    return pl.pallas_call(
        functools.partial(_outproj_kernel, hm=hm, hn=hn, hf=hf),
        grid=(b, s // tm),
        in_specs=[head_spec(hm), head_spec(hn), head_spec(hn),
                  pl.BlockSpec((1, tm, LANES), row), head_spec(hf),
                  _resident((1,) + w_out.shape[1:], lambda bb, i: (l, 0, 0)),
                  pl.BlockSpec((1, tm, d), row), _layer_vec(l, d), _mod_vec(l, 2, d),
                  _layer_vec(l, d), _mod_vec(l, 3, d), _mod_vec(l, 4, d)],
        out_specs=[pl.BlockSpec((1, tm, d), row), pl.BlockSpec((1, tm, d), row)],
        out_shape=[jax.ShapeDtypeStruct((b, s, d), F32), jax.ShapeDtypeStruct((b, s, d), BF16)],
        compiler_params=_params(2),
        name="outproj",
    )(o_moba, o_cw, o_slc, small, o_fox, w_out, x, g_post, mod6, g_pre, mod6, mod6)


def _mlp_kernel(h_ref, wu_ref, wd_ref, x_ref, g_ref, gate_ref, o_ref, acc_ref):
    f = pl.program_id(2)

    @pl.when(f == 0)
    def _():
        acc_ref[...] = jnp.zeros_like(acc_ref)

    u = jnp.maximum(_dot(h_ref[0], wu_ref[0]), 0.0)
    acc_ref[...] += _dot((u * u).astype(BF16), wd_ref[0])

    @pl.when(f == pl.num_programs(2) - 1)
    def _():
        o_ref[0] = x_ref[0] + gate_ref[0, 0, 0] * _rms(acc_ref[...], g_ref[0])


def _mlp_call(h, w_up, w_down, x, g_post, mod6, *, l, tm, tf):
    b, s, d = x.shape
    dff = w_up.shape[2]
    row = lambda bb, i, f: (bb, i, 0)
    return pl.pallas_call(
        _mlp_kernel,
        grid=(b, s // tm, dff // tf),
        in_specs=[pl.BlockSpec((1, tm, d), row),
                  pl.BlockSpec((1, d, tf), lambda bb, i, f: (l, 0, f)),
                  pl.BlockSpec((1, tf, d), lambda bb, i, f: (l, f, 0)),
                  pl.BlockSpec((1, tm, d), row),
                  _layer_vec(l, d), _mod_vec(l, 5, d)],
        out_specs=pl.BlockSpec((1, tm, d), row),
        out_shape=jax.ShapeDtypeStruct((b, s, d), F32),
        scratch_shapes=[pltpu.VMEM((tm, d), F32)],
        compiler_params=_params(3),
        name="mlp",
    )(h, w_up, w_down, x, g_post, mod6)


def _tile(n, pref):
    t = min(n, pref)
    assert n % t == 0
    return t


def _rope_tables(positions):
    inv_freq = ROPE_THETA ** (-jnp.arange(ROPE_HALF, dtype=F32) / ROPE_HALF)
    ang = positions.astype(F32)[..., None] * inv_freq
    cos, sin = jnp.cos(ang), jnp.sin(ang)
    ones = jnp.ones(ang.shape[:-1] + (LANES - ROPE_DIMS,), F32)
    cos_t = jnp.concatenate([cos, cos, ones], axis=-1)
    sin_t = jnp.concatenate([-sin, sin, 0.0 * ones], axis=-1)
    return cos_t, sin_t


def _block_onehot(s, block):
    ids = np.arange(s) // block
    return jnp.asarray((ids[:, None] == np.arange(LANES)[None, :]).astype(np.float32), dtype=BF16)[None]


def _overlap_matrix(s):
    nc = s // NSA_CMP_STRIDE
    n_cmp = (s - NSA_CMP_LEN) // NSA_CMP_STRIDE + 1
    c = np.arange(nc)
    first = (c * NSA_CMP_STRIDE) // NSA_SEL_BLOCK
    last = (c * NSA_CMP_STRIDE + NSA_CMP_LEN - 1) // NSA_SEL_BLOCK
    sb = np.arange(LANES)
    ov = (sb[None, :] >= first[:, None]) & (sb[None, :] <= last[:, None]) & (c[:, None] < n_cmp)
    return jnp.asarray(ov.astype(np.float32), dtype=BF16)


def kernel(x, c, positions, w_mod, b_mod, g_pre_mix, g_post_mix, g_pre_mlp, g_post_mlp, w_in,
           b_forget, cmp_pe_k, cmp_pe_v, cmp_w1_k, cmp_w2_k, cmp_w1_v, cmp_w2_v, w_out, w_up, w_down):
    b, s, d = x.shape
    depth = w_mod.shape[0]
    n_heads = d // HEAD_DIM
    hm = n_heads // 4
    hn = n_heads // 4
    hf = n_heads - hm - hn
    mw, nw, fw = hm * HEAD_DIM, hn * HEAD_DIM, hf * HEAD_DIM
    n_gate = hn * NSA_N_BRANCH
    o_g = 3 * mw + nw + NSA_N_KV * HEAD_DIM
    o_f = o_g + n_gate
    o_ff = o_f + 3 * fw
    assert n_gate + hf <= LANES and s % MOBA_BLOCK == 0 and s // MOBA_BLOCK <= LANES

    c_pad = jnp.zeros((8, d), F32).at[:b].set(c)
    mod = _mod_call(c_pad, w_mod, b_mod)
    mod6 = mod[:, :b].reshape(depth, b, 6, 1, d).transpose(0, 2, 1, 3, 4)
    cos_t, sin_t = _rope_tables(positions)
    moba_ka = _block_onehot(s, MOBA_BLOCK)
    slc_ka = _block_onehot(s, NSA_SEL_BLOCK)
    overlap = _overlap_matrix(s)
    nb = s // MOBA_BLOCK

    w_cat = _wcat_call(w_in, o_g=o_g, o_f=o_f, o_ff=o_ff, tr=_tile(d, 256))
    w_out_b, w_up_b, w_down_b = w_out.astype(BF16), w_up.astype(BF16), w_down.astype(BF16)
    cmp_pe = jnp.stack([cmp_pe_k, cmp_pe_v], axis=1).reshape(depth, 2, 1, NSA_CMP_LEN * HEAD_DIM)
    cmp_w1 = jnp.stack([cmp_w1_k, cmp_w1_v], axis=1).astype(BF16)
    cmp_w2 = jnp.stack([cmp_w2_k, cmp_w2_v], axis=1).astype(BF16)
    bf_rows = jnp.zeros((depth, 1, LANES), F32).at[:, 0, n_gate:n_gate + hf].set(b_forget)
    gains = [g.reshape(depth, 1, d) for g in (g_pre_mix, g_post_mix, g_pre_mlp, g_post_mlp)]

    tm_in = _tile(s, 512)
    for l in range(depth):
        (mq, mk, mv, kmean, nq, cv, ks, vs, kw, vw, small, fq, fk, fv) = _inproj_call(
            x, gains[0], mod6, cos_t, sin_t, w_cat, l=l, hm=hm, hn=hn, hf=hf, tm=tm_in)

        fqa, fka = _fox_prep_call(small, bf_rows, l=l, hf=hf, col0=n_gate, ts=_tile(s, 512))
        fold = lambda t: t.reshape(b * t.shape[1], 1, s, LANES)
        flat = lambda t: t.reshape(b * t.shape[1], s, LANES)
        o_fox = _attn_call(fold(fq), flat(fqa), flat(fk), flat(fka), flat(fv),
                           tq=_tile(s, 512), tk=_tile(s, 1024), out_dtype=BF16).reshape(b, hf, s, LANES)

        kmean = kmean.reshape(b, s // tm_in, hm, tm_in // MOBA_BLOCK, LANES)
        kmean = kmean.transpose(0, 2, 1, 3, 4).reshape(b * hm, nb, LANES)
        kmean = jnp.pad(kmean, ((0, 0), (0, LANES - nb), (0, 0)))
        mqa = _moba_select_call(fold(mq), kmean, nb=nb, tq=_tile(s, 512))
        o_moba = _attn_call(fold(mq), mqa, flat(mk), moba_ka, flat(mv), tq=_tile(s, 512),
                            tk=_tile(s, 1024), out_dtype=BF16, ka_shared=True).reshape(b, hm, s, LANES)

        kvc = _compress_call(cv, cmp_pe, cmp_w1, cmp_w2, l=l)
        o_cw, sbias = _nsa_local_call(nq, kvc, overlap, kw, vw, small, tq=_tile(s, 128))
        slc_split = 2 if hn % 2 == 0 else 1
        o_slc = _attn_call(nq.reshape(b * slc_split, hn // slc_split, s, LANES), sbias, ks, slc_ka, vs,
                           tq=_tile(s, 256), tk=_tile(s, 1024), out_dtype=BF16,
                           kv_groups=slc_split, ka_shared=True).reshape(b, hn, s, LANES)

        x, h_mlp = _outproj_call(o_moba, o_cw, o_slc, small, o_fox, w_out_b, x,
                                 gains[1], gains[2], mod6, l=l, tm=_tile(s, 256))
        x = _mlp_call(h_mlp, w_up_b, w_down_b, x, gains[3], mod6, l=l,
                      tm=_tile(s, 512), tf=1024)
    return x
```

```python
import functools

import jax
import jax.numpy as jnp
import numpy as np
from jax import lax
from jax.experimental import pallas as pl
from jax.experimental.pallas import tpu as pltpu

HEAD_DIM = 128
ROPE_THETA = 500000.0
ROPE_DIMS = HEAD_DIM // 4
ROPE_HALF = ROPE_DIMS // 2
MOBA_BLOCK = 256
MOBA_TOPK = 3
NSA_CMP_LEN = 32
NSA_CMP_STRIDE = 16
NSA_CMP_HIDDEN = 256
NSA_SEL_BLOCK = 64
NSA_N_SEL = 16
NSA_WINDOW = 512
NSA_N_BRANCH = 3
NSA_N_KV = 6
NORM_EPS = 1e-6
NEG_INF = -1e30
ATTN_SCALE = HEAD_DIM ** -0.5
LOG2E = 1.4426950408889634
Q_SCALE = ATTN_SCALE * LOG2E

LANES = 128
VMEM_LIMIT_BYTES = 56 * 1024 * 1024

F32 = jnp.float32
BF16 = jnp.bfloat16


def _params(n_axes):
    return pltpu.CompilerParams(dimension_semantics=("arbitrary",) * n_axes,
                                vmem_limit_bytes=VMEM_LIMIT_BYTES)


def _resident(block_shape, index_map):
    return pl.BlockSpec(block_shape, index_map, pipeline_mode=pl.Buffered(1))


def _split3(x):
    p1 = x.astype(BF16)
    r1 = x - p1.astype(F32)
    p2 = r1.astype(BF16)
    p3 = (r1 - p2.astype(F32)).astype(BF16)
    return p1, p2, p3


def _dot(a, b):
    return jnp.dot(a, b, preferred_element_type=F32)


def _dot_nt(a, b):
    return lax.dot_general(a, b, (((1,), (1,)), ((), ())), preferred_element_type=F32)


def _rms(x, g):
    return x * lax.rsqrt(jnp.mean(x * x, axis=-1, keepdims=True) + NORM_EPS) * g


def _sigmoid(z):
    return 1.0 / (1.0 + jnp.exp(-z))


def _lane_col(x, c):
    return jnp.broadcast_to(x[:, c:c + 1], x.shape)


def _mod_kernel(c_ref, w_ref, b_ref, o_ref):
    c = c_ref[...]
    ca = c * _sigmoid(c)
    o_ref[0] = jnp.dot(ca, w_ref[0], precision=lax.Precision.HIGHEST,
                       preferred_element_type=F32) + b_ref[0]


def _mod_call(c_pad, w_mod, b_mod):
    depth, d, n = w_mod.shape
    rows = c_pad.shape[0]
    tn = 1024
    assert n % tn == 0
    return pl.pallas_call(
        _mod_kernel,
        grid=(depth, n // tn),
        in_specs=[pl.BlockSpec((rows, d), lambda l, j: (0, 0)),
                  pl.BlockSpec((1, d, tn), lambda l, j: (l, 0, j)),
                  pl.BlockSpec((1, 1, tn), lambda l, j: (l, 0, j))],
        out_specs=pl.BlockSpec((1, rows, tn), lambda l, j: (l, 0, j)),
        out_shape=jax.ShapeDtypeStruct((depth, rows, n), F32),
        compiler_params=_params(2),
        name="adaln_mod",
    )(c_pad, w_mod, b_mod.reshape(depth, 1, n))


def _wcat_kernel(w_ref, o_ref, *, o_g, o_f, o_ff):
    x = w_ref[0]
    tr, n_in = x.shape
    n_small = (o_f - o_g) + (n_in - o_ff)
    small = jnp.concatenate([x[:, o_g:o_f], x[:, o_ff:], jnp.zeros((tr, LANES - n_small), F32)], axis=1)
    o_ref[0, :, :o_g] = x[:, :o_g].astype(BF16)
    o_ref[0, :, o_g:o_g + (o_ff - o_f)] = x[:, o_f:o_ff].astype(BF16)
    o_ref[0, :, o_g + (o_ff - o_f):] = small.astype(BF16)


def _wcat_call(w_in, *, o_g, o_f, o_ff, tr):
    depth, d, n_in = w_in.shape
    npad = o_g + (o_ff - o_f) + LANES
    assert o_g % LANES == 0 and (o_ff - o_f) % LANES == 0
    return pl.pallas_call(
        functools.partial(_wcat_kernel, o_g=o_g, o_f=o_f, o_ff=o_ff),
        grid=(depth, d // tr),
        in_specs=[pl.BlockSpec((1, tr, n_in), lambda l, i: (l, i, 0))],
        out_specs=pl.BlockSpec((1, tr, npad), lambda l, i: (l, i, 0)),
        out_shape=jax.ShapeDtypeStruct((depth, d, npad), BF16),
        compiler_params=_params(2),
        name="w_in_layout",
    )(w_in)


def _inproj_kernel(x_ref, g_ref, sh_ref, sc_ref, cos_ref, sin_ref, w_ref,
                   mq_ref, mk_ref, mv_ref, kmean_ref, nq_ref, cv_ref,
                   ks_ref, vs_ref, kw_ref, vw_ref, small_ref, fq_ref, fk_ref, fv_ref,
                   *, hm, hn, hf):
    x = x_ref[0]
    h = _rms(x, g_ref[0]) * (1.0 + sc_ref[0, 0, 0]) + sh_ref[0, 0, 0]
    hb = h.astype(BF16)
    tm = x.shape[0]
    cos = cos_ref[0]
    sin = sin_ref[0]
    lane = lax.broadcasted_iota(jnp.int32, (tm, LANES), 1)

    def rope(y):
        swapped = jnp.where(lane < ROPE_HALF, pltpu.roll(y, LANES - ROPE_HALF, 1),
                            pltpu.roll(y, ROPE_HALF, 1))
        return y * cos + swapped * sin

    def heads(col0, n):
        outs = []
        j = 0
        while j < n:
            w = 2 if j + 1 < n else 1
            y = _dot(hb, w_ref[0, :, (col0 + j) * LANES:(col0 + j + w) * LANES])
            for t in range(w):
                outs.append(y[:, t * LANES:(t + 1) * LANES])
            j += w
        return outs

    col = 0
    for hh, y in enumerate(heads(col, hm)):
        mq_ref[0, hh] = (rope(y) * Q_SCALE).astype(BF16)
    col += hm
    nblk = tm // MOBA_BLOCK
    means = []
    for hh, y in enumerate(heads(col, hm)):
        yr = rope(y)
        mk_ref[0, hh] = yr.astype(BF16)
        means.append(jnp.mean(yr.reshape(nblk, MOBA_BLOCK, LANES), axis=1))
    kmean_ref[0, 0] = jnp.concatenate(means, axis=0)
    col += hm
    for hh, y in enumerate(heads(col, hm)):
        mv_ref[0, hh] = y.astype(BF16)
    col += hm
    for hh, y in enumerate(heads(col, hn)):
        nq_ref[0, hh] = (rope(y) * Q_SCALE).astype(BF16)
    col += hn
    kc, vc, ks, vs, kw, vw = heads(col, NSA_N_KV)
    cv_ref[0, 0] = rope(kc)
    cv_ref[0, 1] = vc
    ks_ref[0] = rope(ks).astype(BF16)
    vs_ref[0] = vs.astype(BF16)
    kw_ref[0] = rope(kw).astype(BF16)
    vw_ref[0] = vw.astype(BF16)
    col += NSA_N_KV
    for hh, y in enumerate(heads(col, hf)):
        fq_ref[0, hh] = (y * Q_SCALE).astype(BF16)
    col += hf
    for hh, y in enumerate(heads(col, hf)):
        fk_ref[0, hh] = y.astype(BF16)
    col += hf
    for hh, y in enumerate(heads(col, hf)):
        fv_ref[0, hh] = y.astype(BF16)
    col += hf
    small_ref[0] = heads(col, 1)[0]


def _layer_vec(l, d):
    return pl.BlockSpec((1, 1, d), lambda *_: (l, 0, 0))


def _mod_vec(l, k, d):
    return pl.BlockSpec((1, 1, 1, 1, d), lambda bb, *_: (l, k, bb, 0, 0))


def _inproj_call(x, g, mod6, cos_t, sin_t, w_cat, *, l, hm, hn, hf, tm):
    b, s, d = x.shape
    npad = w_cat.shape[2]
    nblk = tm // MOBA_BLOCK
    row = lambda bb, i: (bb, i, 0)
    head_spec = lambda n: pl.BlockSpec((1, n, tm, LANES), lambda bb, i: (bb, 0, i, 0))
    head_shape = lambda n, dt=BF16: jax.ShapeDtypeStruct((b, n, s, LANES), dt)
    tok_spec = pl.BlockSpec((1, tm, LANES), row)
    tok_shape = lambda dt=BF16: jax.ShapeDtypeStruct((b, s, LANES), dt)
    out_specs = [head_spec(hm), head_spec(hm), head_spec(hm),
                 pl.BlockSpec((1, 1, hm * nblk, LANES), lambda bb, i: (bb, i, 0, 0)),
                 head_spec(hn), head_spec(2),
                 tok_spec, tok_spec, tok_spec, tok_spec, tok_spec,
                 head_spec(hf), head_spec(hf), head_spec(hf)]
    out_shape = [head_shape(hm), head_shape(hm), head_shape(hm),
                 jax.ShapeDtypeStruct((b, s // tm, hm * nblk, LANES), F32),
                 head_shape(hn), head_shape(2, F32),
                 tok_shape(), tok_shape(), tok_shape(), tok_shape(), tok_shape(F32),
                 head_shape(hf), head_shape(hf), head_shape(hf)]
    return pl.pallas_call(
        functools.partial(_inproj_kernel, hm=hm, hn=hn, hf=hf),
        grid=(b, s // tm),
        in_specs=[pl.BlockSpec((1, tm, d), row),
                  _layer_vec(l, d), _mod_vec(l, 0, d), _mod_vec(l, 1, d),
                  pl.BlockSpec((1, tm, LANES), row),
                  pl.BlockSpec((1, tm, LANES), row),
                  _resident((1, d, npad), lambda bb, i: (l, 0, 0))],
        out_specs=out_specs,
        out_shape=out_shape,
        compiler_params=_params(2),
        name="inproj",
    )(x, g, mod6, mod6, cos_t, sin_t, w_cat)


def _fox_prep_kernel(small_ref, bf_ref, qa_ref, ka_ref, carry_ref, *, hf, col0):
    i = pl.program_id(1)

    @pl.when(i == 0)
    def _():
        carry_ref[...] = jnp.zeros_like(carry_ref)

    z = small_ref[0] + bf_ref[0]
    logf = jnp.minimum(z, 0.0) - jnp.log1p(jnp.exp(-jnp.abs(z)))
    ts = z.shape[0]
    r = lax.broadcasted_iota(jnp.int32, (ts, ts), 0)
    c = lax.broadcasted_iota(jnp.int32, (ts, ts), 1)
    tri = jnp.where(r >= c, 1.0, 0.0).astype(BF16)
    p1, p2, p3 = _split3(logf)
    cum = _dot(tri, p1) + _dot(tri, p2) + _dot(tri, p3) + carry_ref[...]
    carry_ref[...] = cum[ts - 1:ts, :]
    lane = lax.broadcasted_iota(jnp.int32, (ts, LANES), 1)
    for hh in range(hf):
        cb = _lane_col(cum, col0 + hh) * LOG2E
        c1, c2, c3 = (p.astype(F32) for p in _split3(cb))
        qa = jnp.where(lane == 0, c1, jnp.where(lane == 1, c2, jnp.where(
            lane == 2, c3, jnp.where(lane < 6, 1.0, 0.0))))
        ka = jnp.where(lane < 3, 1.0, jnp.where(lane == 3, -c1, jnp.where(
            lane == 4, -c2, jnp.where(lane == 5, -c3, 0.0))))
        qa_ref[0, hh] = qa.astype(BF16)
        ka_ref[0, hh] = ka.astype(BF16)


def _fox_prep_call(small, bf_rows, *, l, hf, col0, ts):
    b, s, _ = small.shape
    spec = pl.BlockSpec((1, hf, ts, LANES), lambda bb, i: (bb, 0, i, 0))
    shape = jax.ShapeDtypeStruct((b, hf, s, LANES), BF16)
    return pl.pallas_call(
        functools.partial(_fox_prep_kernel, hf=hf, col0=col0),
        grid=(b, s // ts),
        in_specs=[pl.BlockSpec((1, ts, LANES), lambda bb, i: (bb, i, 0)),
                  _layer_vec(l, LANES)],
        out_specs=[spec, spec],
        out_shape=[shape, shape],
        scratch_shapes=[pltpu.VMEM((1, LANES), F32)],
        compiler_params=_params(2),
        name="fox_prep",
    )(small, bf_rows)


def _first_max_row(score, blk):
    m = jnp.max(score, axis=0, keepdims=True)
    idx = jnp.min(jnp.where(score == m, blk, float(score.shape[0])), axis=0, keepdims=True)
    return blk == idx, m


def _moba_select_kernel(q_ref, km_ref, o_ref, *, nb, top):
    i = pl.program_id(1)
    q = q_ref[0, 0]
    tq = q.shape[0]
    nbp = -(-nb // 8) * 8
    k1, k2, k3 = _split3(km_ref[0, :nbp, :])
    gate = _dot_nt(k1, q) + _dot_nt(k2, q) + _dot_nt(k3, q)
    blk = lax.broadcasted_iota(jnp.int32, (nbp, tq), 0).astype(F32)
    tok = i * tq + lax.broadcasted_iota(jnp.int32, (nbp, tq), 1)
    own = (tok >> (MOBA_BLOCK.bit_length() - 1)).astype(F32)
    past = blk < own
    score = jnp.where(blk < nb, jnp.where(past, gate, NEG_INF), -jnp.inf)
    sel = jnp.where(blk == own, 1.0, 0.0)
    for _ in range(top):
        pick, _m = _first_max_row(score, blk)
        sel = jnp.where(pick, jnp.where(past, 1.0, sel), sel)
        score = jnp.where(pick, -jnp.inf, score)
    bias = jnp.where(sel > 0.0, 0.0, jnp.where(blk < nb, NEG_INF, 0.0))
    bias = jnp.concatenate([bias, jnp.zeros((LANES - nbp, tq), F32)], axis=0)
    o_ref[0] = bias.T.astype(BF16)


def _moba_select_call(q, kmean_pad, *, nb, tq):
    g, _, s, _ = q.shape
    top = min(MOBA_TOPK, max(nb - 1, 1))
    return pl.pallas_call(
        functools.partial(_moba_select_kernel, nb=nb, top=top),
        grid=(g, s // tq),
        in_specs=[pl.BlockSpec((1, 1, tq, LANES), lambda gg, i: (gg, 0, i, 0)),
                  pl.BlockSpec((1, LANES, LANES), lambda gg, i: (gg, 0, 0))],
        out_specs=pl.BlockSpec((1, tq, LANES), lambda gg, i: (gg, i, 0)),
        out_shape=jax.ShapeDtypeStruct((g, s, LANES), BF16),
        compiler_params=_params(2),
        name="moba_select",
    )(q, kmean_pad)


def _attn_kernel(q_ref, qa_ref, k_ref, ka_ref, v_ref, o_ref,
                 kc_ref, vc_ref, s0_ref, s1_ref, m_ref, acc_ref, *, hs, tq, tk):
    rows = hs * tq
    seq = k_ref.shape[1]
    kc_ref[:, :LANES] = k_ref[0]
    kc_ref[:, LANES:] = ka_ref[0]
    vc_ref[:, :LANES] = v_ref[0]
    vc_ref[:, LANES:] = jnp.ones((seq, LANES), BF16)
    row_in_tile = lax.broadcasted_iota(jnp.int32, (rows, LANES), 0) & (tq - 1)
    col = lax.broadcasted_iota(jnp.int32, (rows, tk), 1)

    def logits(i, j, s_ref):
        qs = pl.multiple_of(i * tq, tq)
        qc = jnp.concatenate([q_ref[0, :, pl.ds(qs, tq), :].reshape(rows, LANES),
                              jnp.concatenate([qa_ref[0, pl.ds(qs, tq), :]] * hs, axis=0)], axis=-1)
        s_ref[...] = _dot_nt(qc, kc_ref[pl.ds(pl.multiple_of(j * tk, tk), tk), :])

    def update(i, j, s_ref):
        limit = jnp.tile(row_in_tile + (i * tq - j * tk), (1, tk // LANES))
        s = jnp.where(col <= limit, s_ref[...], NEG_INF)
        m_prev = jnp.where(j == 0, -jnp.inf, m_ref[...])
        m_new = jnp.maximum(m_prev, jnp.max(s, axis=-1, keepdims=True))
        alpha = jnp.exp2(m_prev - m_new)
        p = jnp.exp2(s - jnp.tile(m_new, (1, tk // LANES))).astype(BF16)
        m_ref[...] = m_new
        acc = (jnp.tile(alpha, (1, 2)) * acc_ref[...]
               + _dot(p, vc_ref[pl.ds(pl.multiple_of(j * tk, tk), tk), :]))
        acc_ref[...] = acc
        out = (acc[:, :LANES] / acc[:, LANES:]).astype(o_ref.dtype)
        qs = pl.multiple_of(i * tq, tq)
        for hh in range(hs):
            o_ref[0, hh, pl.ds(qs, tq), :] = out[hh * tq:(hh + 1) * tq]

    def advance(i, j):
        wrap = j == (i * tq) // tk
        return jnp.where(wrap, i + 1, i), jnp.where(wrap, 0, j + 1)

    n_pairs = sum((i * tq) // tk + 1 for i in range(seq // tq))
    zero = jnp.int32(0)
    acc_ref[...] = jnp.zeros_like(acc_ref)
    m_ref[...] = jnp.zeros_like(m_ref)
    logits(zero, zero, s0_ref)

    def trip(carry, s_cur, s_next):
        ib, jb, ia, ja = carry
        logits(ia, ja, s_next)
        update(ib, jb, s_cur)
        return (ia, ja) + advance(ia, ja)

    def body(_, carry):
        return trip(trip(carry, s0_ref, s1_ref), s1_ref, s0_ref)

    carry = lax.fori_loop(0, (n_pairs - 1) // 2, body, (zero, zero) + advance(zero, zero))
    if (n_pairs - 1) % 2:
        carry = trip(carry, s0_ref, s1_ref)
        update(carry[0], carry[1], s1_ref)
    else:
        update(carry[0], carry[1], s0_ref)


def _attn_call(q, qa, k, ka, v, *, tq, tk, out_dtype, kv_groups=1, ka_shared=False):
    g, hs, s, _ = q.shape
    assert tq & (tq - 1) == 0 and tk % tq == 0 and s % tk == 0
    rows = hs * tq
    seq = lambda gg: (gg // kv_groups, 0, 0)
    ka_map = (lambda gg: (0, 0, 0)) if ka_shared else seq
    return pl.pallas_call(
        functools.partial(_attn_kernel, hs=hs, tq=tq, tk=tk),
        grid=(g,),
        in_specs=[pl.BlockSpec((1, hs, s, LANES), lambda gg: (gg, 0, 0, 0)),
                  pl.BlockSpec((1, s, LANES), seq),
                  pl.BlockSpec((1, s, LANES), seq),
                  pl.BlockSpec((1, s, LANES), ka_map),
                  pl.BlockSpec((1, s, LANES), seq)],
        out_specs=pl.BlockSpec((1, hs, s, LANES), lambda gg: (gg, 0, 0, 0)),
        out_shape=jax.ShapeDtypeStruct((g, hs, s, LANES), out_dtype),
        scratch_shapes=[pltpu.VMEM((s, 2 * LANES), BF16),
                        pltpu.VMEM((s, 2 * LANES), BF16),
                        pltpu.VMEM((rows, tk), F32),
                        pltpu.VMEM((rows, tk), F32),
                        pltpu.VMEM((rows, LANES), F32),
                        pltpu.VMEM((rows, 2 * LANES), F32)],
        compiler_params=_params(1),
        name="flash_attn",
    )(q, qa, k, ka, v)


def _compress_kernel(x_ref, pe_ref, w1_ref, w2_ref, o_ref):
    x = x_ref[0, 0]
    half = x.shape[1]
    pe = pe_ref[0, 0]
    top = _dot((x + pe[:, :half]).astype(BF16), w1_ref[0, 0, :half, :])
    bot = _dot((x + pe[:, half:]).astype(BF16), w1_ref[0, 0, half:, :])
    n = x.shape[0]
    hid = top + pltpu.roll(bot, n - 1, 0)
    act = 0.5 * hid * (1.0 + jnp.tanh(0.7978845608028654 * (hid + 0.044715 * hid * hid * hid)))
    o_ref[0, 0] = _dot(act.astype(BF16), w2_ref[0, 0]).astype(o_ref.dtype)


def _compress_call(cv, pe, w1, w2, *, l):
    b, _, s, _ = cv.shape
    n = s // NSA_CMP_STRIDE
    width = NSA_CMP_STRIDE * LANES
    x2 = cv.reshape(b, 2, n, width)
    return pl.pallas_call(
        _compress_kernel,
        grid=(b, 2),
        in_specs=[pl.BlockSpec((1, 1, n, width), lambda bb, t: (bb, t, 0, 0)),
                  pl.BlockSpec((1, 1, 1, 2 * width), lambda bb, t: (l, t, 0, 0)),
                  pl.BlockSpec((1, 1, 2 * width, NSA_CMP_HIDDEN), lambda bb, t: (l, t, 0, 0)),
                  pl.BlockSpec((1, 1, NSA_CMP_HIDDEN, LANES), lambda bb, t: (l, t, 0, 0))],
        out_specs=pl.BlockSpec((1, 1, n, LANES), lambda bb, t: (bb, t, 0, 0)),
        out_shape=jax.ShapeDtypeStruct((b, 2, n, LANES), BF16),
        compiler_params=_params(2),
        name="nsa_compress",
    )(x2, pe, w1, w2)


def _nsa_local_kernel(q_ref, kc_ref, vc_ref, ovt_ref, kw_ref, vw_ref, small_ref,
                      ocw_ref, bias_ref, *, hn, tq, n_sb):
    i = pl.program_id(1)
    rows = hn * tq
    q = q_ref[0].reshape(rows, LANES)
    gates = _sigmoid(small_ref[0])
    nc = kc_ref.shape[2]
    s = _dot_nt(q, kc_ref[0, 0])
    r = lax.broadcasted_iota(jnp.int32, (rows, nc), 0)
    tok = i * tq + (r & (tq - 1))
    cend = lax.broadcasted_iota(jnp.int32, (rows, nc), 1) * NSA_CMP_STRIDE + (NSA_CMP_LEN - 1)
    s = jnp.where(cend <= tok, s, NEG_INF)
    e = jnp.exp2(s - jnp.max(s, axis=-1, keepdims=True))
    tok1 = i * tq + (lax.broadcasted_iota(jnp.int32, (rows, 1), 0) & (tq - 1))
    rinv = jnp.where(tok1 >= NSA_CMP_LEN - 1, 1.0 / jnp.sum(e, axis=-1, keepdims=True), 0.0)
    o_cmp = (_dot(e.astype(BF16), vc_ref[0, 0]) * rinv).reshape(hn, tq, LANES)
    psum = jnp.sum((e * rinv).reshape(hn, tq, nc), axis=0)
    hi = psum.astype(BF16)
    lo = (psum - hi.astype(F32)).astype(BF16)
    imp = _dot_nt(ovt_ref[...], hi) + _dot_nt(ovt_ref[...], lo)
    blk = lax.broadcasted_iota(jnp.int32, (LANES, tq), 0).astype(F32)
    cur = ((i * tq + lax.broadcasted_iota(jnp.int32, (LANES, tq), 1))
           >> (NSA_SEL_BLOCK.bit_length() - 1)).astype(F32)
    score = jnp.where(blk >= 1.0, jnp.where(blk <= cur - 2.0, imp, -jnp.inf), -jnp.inf)
    sel = jnp.where(blk == 0.0, 1.0, jnp.where(blk == cur, 1.0, jnp.where(blk == cur - 1.0, 1.0, 0.0)))
    for _ in range(min(NSA_N_SEL, n_sb) - 3):
        pick, m = _first_max_row(score, blk)
        sel = jnp.where(pick, jnp.where(m > -jnp.inf, 1.0, sel), sel)
        score = jnp.where(pick, -jnp.inf, score)
    bias_ref[0] = jnp.where(sel > 0.0, 0.0, NEG_INF).T.astype(BF16)
    span = NSA_WINDOW + tq
    start = pl.multiple_of(jnp.maximum(i * tq - NSA_WINDOW, 0), tq)
    sw = _dot_nt(q, kw_ref[0, pl.ds(start, span), :])
    rw = lax.broadcasted_iota(jnp.int32, (rows, span), 0)
    back = (i * tq - start) + (rw & (tq - 1)) - lax.broadcasted_iota(jnp.int32, (rows, span), 1)
    sw = jnp.where(lax.bitcast_convert_type(back, jnp.uint32) < NSA_WINDOW, sw, NEG_INF)
    ew = jnp.exp2(sw - jnp.max(sw, axis=-1, keepdims=True)).astype(BF16)
    vw1 = jnp.concatenate([vw_ref[0, pl.ds(start, span), :], jnp.ones((span, LANES), BF16)], axis=-1)
    ow = _dot(ew, vw1)
    o_win = (ow[:, :LANES] / ow[:, LANES:]).reshape(hn, tq, LANES)
    for hh in range(hn):
        ocw_ref[0, hh] = (_lane_col(gates, hh * NSA_N_BRANCH) * o_cmp[hh]
                          + _lane_col(gates, hh * NSA_N_BRANCH + 2) * o_win[hh])


def _nsa_local_call(q, kvc, overlap, kw, vw, small, *, tq):
    b, hn, s, _ = q.shape
    nc = kvc.shape[2]
    n_sb = s // NSA_SEL_BLOCK
    assert n_sb <= LANES and tq & (tq - 1) == 0 and NSA_WINDOW % tq == 0
    seq = lambda bb, i: (bb, 0, 0)
    return pl.pallas_call(
        functools.partial(_nsa_local_kernel, hn=hn, tq=tq, n_sb=n_sb),
        grid=(b, s // tq),
        in_specs=[pl.BlockSpec((1, hn, tq, LANES), lambda bb, i: (bb, 0, i, 0)),
                  pl.BlockSpec((1, 1, nc, LANES), lambda bb, i: (bb, 0, 0, 0)),
                  pl.BlockSpec((1, 1, nc, LANES), lambda bb, i: (bb, 1, 0, 0)),
                  pl.BlockSpec((LANES, nc), lambda bb, i: (0, 0)),
                  pl.BlockSpec((1, s, LANES), seq),
                  pl.BlockSpec((1, s, LANES), seq),
                  pl.BlockSpec((1, tq, LANES), lambda bb, i: (bb, i, 0))],
        out_specs=[pl.BlockSpec((1, hn, tq, LANES), lambda bb, i: (bb, 0, i, 0)),
                   pl.BlockSpec((1, tq, LANES), lambda bb, i: (bb, i, 0))],
        out_shape=[jax.ShapeDtypeStruct((b, hn, s, LANES), F32),
                   jax.ShapeDtypeStruct((b, s, LANES), BF16)],
        compiler_params=_params(2),
        name="nsa_local",
    )(q, kvc, kvc, overlap, kw, vw, small)


def _outproj_kernel(om_ref, ocw_ref, osl_ref, small_ref, of_ref, w_ref, x_ref,
                    gpost_ref, gate_ref, gpre_ref, sh_ref, sc_ref, xo_ref, h_ref,
                    *, hm, hn, hf):
    gates = _sigmoid(small_ref[0])
    parts = [om_ref[0, hh] for hh in range(hm)]
    for hh in range(hn):
        o = ocw_ref[0, hh] + _lane_col(gates, hh * NSA_N_BRANCH + 1) * osl_ref[0, hh]
        parts.append(o.astype(BF16))
    parts += [of_ref[0, hh] for hh in range(hf)]
    y = _dot(jnp.concatenate(parts, axis=-1), w_ref[0])
    xn = x_ref[0] + gate_ref[0, 0, 0] * _rms(y, gpost_ref[0])
    xo_ref[0] = xn
    h_ref[0] = (_rms(xn, gpre_ref[0]) * (1.0 + sc_ref[0, 0, 0]) + sh_ref[0, 0, 0]).astype(BF16)


def _outproj_call(o_moba, o_cw, o_slc, small, o_fox, w_out, x, g_post, g_pre, mod6, *, l, tm):
    b, s, d = x.shape
    hm, hn, hf = o_moba.shape[1], o_cw.shape[1], o_fox.shape[1]
    head_spec = lambda n: pl.BlockSpec((1, n, tm, LANES), lambda bb, i: (bb, 0, i, 0))
    row = lambda bb, i: (bb, i, 0)
    return pl.pallas_call(
        functools.partial(_outproj_kernel, hm=hm, hn=hn, hf=hf),
        grid=(b, s // tm),
        in_specs=[head_spec(hm), head_spec(hn), head_spec(hn),
                  pl.BlockSpec((1, tm, LANES), row), head_spec(hf),
                  _resident((1,) + w_out.shape[1:], lambda bb, i: (l, 0, 0)),
                  pl.BlockSpec((1, tm, d), row), _layer_vec(l, d), _mod_vec(l, 2, d),
                  _layer_vec(l, d), _mod_vec(l, 3, d), _mod_vec(l, 4, d)],
        out_specs=[pl.BlockSpec((1, tm, d), row), pl.BlockSpec((1, tm, d), row)],
        out_shape=[jax.ShapeDtypeStruct((b, s, d), F32), jax.ShapeDtypeStruct((b, s, d), BF16)],
        compiler_params=_params(2),
        name="outproj",
    )(o_moba, o_cw, o_slc, small, o_fox, w_out, x, g_post, mod6, g_pre, mod6, mod6)


def _mlp_kernel(h_ref, wu_ref, wd_ref, x_ref, g_ref, gate_ref, o_ref):
    f = pl.program_id(2)

    @pl.when(f == 0)
    def _():
        o_ref[...] = jnp.zeros_like(o_ref)

    u = jnp.maximum(_dot(h_ref[0], wu_ref[0]), 0.0)
    o_ref[0] += _dot((u * u).astype(BF16), wd_ref[0])

    @pl.when(f == pl.num_programs(2) - 1)
    def _():
        o_ref[0] = x_ref[0] + gate_ref[0, 0, 0] * _rms(o_ref[0], g_ref[0])


def _mlp_call(h, w_up, w_down, x, g_post, mod6, *, l, tm, tf):
    b, s, d = x.shape
    dff = w_up.shape[2]
    row = lambda bb, i, f: (bb, i, 0)
    return pl.pallas_call(
        _mlp_kernel,
        grid=(b, s // tm, dff // tf),
        in_specs=[pl.BlockSpec((1, tm, d), row),
                  pl.BlockSpec((1, d, tf), lambda bb, i, f: (l, 0, f)),
                  pl.BlockSpec((1, tf, d), lambda bb, i, f: (l, f, 0)),
                  _resident((1, tm, d), row),
                  _layer_vec(l, d), _mod_vec(l, 5, d)],
        out_specs=pl.BlockSpec((1, tm, d), row),
        out_shape=jax.ShapeDtypeStruct((b, s, d), F32),
        compiler_params=_params(3),
        name="mlp",
    )(h, w_up, w_down, x, g_post, mod6)


def _tile(n, pref):
    t = min(n, pref)
    assert n % t == 0
    return t


def _rope_tables(positions):
    inv_freq = ROPE_THETA ** (-jnp.arange(ROPE_HALF, dtype=F32) / ROPE_HALF)
    ang = positions.astype(F32)[..., None] * inv_freq
    cos, sin = jnp.cos(ang), jnp.sin(ang)
    ones = jnp.ones(ang.shape[:-1] + (LANES - ROPE_DIMS,), F32)
    cos_t = jnp.concatenate([cos, cos, ones], axis=-1)
    sin_t = jnp.concatenate([-sin, sin, 0.0 * ones], axis=-1)
    return cos_t, sin_t


def _block_onehot(s, block):
    ids = np.arange(s) // block
    return jnp.asarray((ids[:, None] == np.arange(LANES)[None, :]).astype(np.float32), dtype=BF16)[None]


def _overlap_matrix(s):
    nc = s // NSA_CMP_STRIDE
    n_cmp = (s - NSA_CMP_LEN) // NSA_CMP_STRIDE + 1
    c = np.arange(nc)
    first = (c * NSA_CMP_STRIDE) // NSA_SEL_BLOCK
    last = (c * NSA_CMP_STRIDE + NSA_CMP_LEN - 1) // NSA_SEL_BLOCK
    sb = np.arange(LANES)
    ov = (sb[None, :] >= first[:, None]) & (sb[None, :] <= last[:, None]) & (c[:, None] < n_cmp)
    return jnp.asarray(ov.T.astype(np.float32), dtype=BF16)


def kernel(x, c, positions, w_mod, b_mod, g_pre_mix, g_post_mix, g_pre_mlp, g_post_mlp, w_in,
           b_forget, cmp_pe_k, cmp_pe_v, cmp_w1_k, cmp_w2_k, cmp_w1_v, cmp_w2_v, w_out, w_up, w_down):
    b, s, d = x.shape
    depth = w_mod.shape[0]
    n_heads = d // HEAD_DIM
    hm = n_heads // 4
    hn = n_heads // 4
    hf = n_heads - hm - hn
    mw, nw, fw = hm * HEAD_DIM, hn * HEAD_DIM, hf * HEAD_DIM
    n_gate = hn * NSA_N_BRANCH
    o_g = 3 * mw + nw + NSA_N_KV * HEAD_DIM
    o_f = o_g + n_gate
    o_ff = o_f + 3 * fw
    assert n_gate + hf <= LANES and s % MOBA_BLOCK == 0 and s // MOBA_BLOCK <= LANES

    c_pad = jnp.zeros((8, d), F32).at[:b].set(c)
    mod = _mod_call(c_pad, w_mod, b_mod)
    mod6 = mod[:, :b].reshape(depth, b, 6, 1, d).transpose(0, 2, 1, 3, 4)
    cos_t, sin_t = _rope_tables(positions)
    moba_ka = _block_onehot(s, MOBA_BLOCK)
    slc_ka = _block_onehot(s, NSA_SEL_BLOCK)
    overlap = _overlap_matrix(s)
    nb = s // MOBA_BLOCK

    w_cat = _wcat_call(w_in, o_g=o_g, o_f=o_f, o_ff=o_ff, tr=_tile(d, 256))
    w_out_b, w_up_b, w_down_b = w_out.astype(BF16), w_up.astype(BF16), w_down.astype(BF16)
    cmp_pe = jnp.stack([cmp_pe_k, cmp_pe_v], axis=1).reshape(depth, 2, 1, NSA_CMP_LEN * HEAD_DIM)
    cmp_w1 = jnp.stack([cmp_w1_k, cmp_w1_v], axis=1).astype(BF16)
    cmp_w2 = jnp.stack([cmp_w2_k, cmp_w2_v], axis=1).astype(BF16)
    bf_rows = jnp.zeros((depth, 1, LANES), F32).at[:, 0, n_gate:n_gate + hf].set(b_forget)
    gains = [g.reshape(depth, 1, d) for g in (g_pre_mix, g_post_mix, g_pre_mlp, g_post_mlp)]

    tm_in = _tile(s, 512)
    for l in range(depth):
        (mq, mk, mv, kmean, nq, cv, ks, vs, kw, vw, small, fq, fk, fv) = _inproj_call(
            x, gains[0], mod6, cos_t, sin_t, w_cat, l=l, hm=hm, hn=hn, hf=hf, tm=tm_in)

        fqa, fka = _fox_prep_call(small, bf_rows, l=l, hf=hf, col0=n_gate, ts=_tile(s, 512))
        fold = lambda t: t.reshape(b * t.shape[1], 1, s, LANES)
        flat = lambda t: t.reshape(b * t.shape[1], s, LANES)
        o_fox = _attn_call(fold(fq), flat(fqa), flat(fk), flat(fka), flat(fv),
                           tq=_tile(s, 512), tk=_tile(s, 1024), out_dtype=BF16).reshape(b, hf, s, LANES)

        kmean = kmean.reshape(b, s // tm_in, hm, tm_in // MOBA_BLOCK, LANES)
        kmean = kmean.transpose(0, 2, 1, 3, 4).reshape(b * hm, nb, LANES)
        kmean = jnp.pad(kmean, ((0, 0), (0, LANES - nb), (0, 0)))
        mqa = _moba_select_call(fold(mq), kmean, nb=nb, tq=_tile(s, 512))
        o_moba = _attn_call(fold(mq), mqa, flat(mk), moba_ka, flat(mv), tq=_tile(s, 512),
                            tk=_tile(s, 1024), out_dtype=BF16, ka_shared=True).reshape(b, hm, s, LANES)

        kvc = _compress_call(cv, cmp_pe, cmp_w1, cmp_w2, l=l)
        o_cw, sbias = _nsa_local_call(nq, kvc, overlap, kw, vw, small, tq=_tile(s, 256))
        slc_split = 2 if hn % 2 == 0 else 1
        o_slc = _attn_call(nq.reshape(b * slc_split, hn // slc_split, s, LANES), sbias, ks, slc_ka, vs,
                           tq=_tile(s, 256), tk=_tile(s, 1024), out_dtype=BF16,
                           kv_groups=slc_split, ka_shared=True).reshape(b, hn, s, LANES)

        x, h_mlp = _outproj_call(o_moba, o_cw, o_slc, small, o_fox, w_out_b, x,
                                 gains[1], gains[2], mod6, l=l, tm=_tile(s, 512))
        x = _mlp_call(h_mlp, w_up_b, w_down_b, x, gains[3], mod6, l=l,
                      tm=_tile(s, 1024), tf=512)
    return x
```

```python
import functools

import jax
import jax.numpy as jnp
import numpy as np
from jax import lax
from jax.experimental import pallas as pl
from jax.experimental.pallas import tpu as pltpu

HEAD_DIM = 128
ROPE_THETA = 500000.0
ROPE_DIMS = HEAD_DIM // 4
ROPE_HALF = ROPE_DIMS // 2
MOBA_BLOCK = 256
MOBA_TOPK = 3
NSA_CMP_LEN = 32
NSA_CMP_STRIDE = 16
NSA_CMP_HIDDEN = 256
NSA_SEL_BLOCK = 64
NSA_N_SEL = 16
NSA_WINDOW = 512
NSA_N_BRANCH = 3
NSA_N_KV = 6
NORM_EPS = 1e-6
NEG_INF = -1e30
ATTN_SCALE = HEAD_DIM ** -0.5
LOG2E = 1.4426950408889634
Q_SCALE = ATTN_SCALE * LOG2E

LANES = 128
VMEM_LIMIT_BYTES = 56 * 1024 * 1024

F32 = jnp.float32
BF16 = jnp.bfloat16


def _params(n_axes):
    return pltpu.CompilerParams(dimension_semantics=("arbitrary",) * n_axes,
                                vmem_limit_bytes=VMEM_LIMIT_BYTES)


def _resident(block_shape, index_map):
    return pl.BlockSpec(block_shape, index_map, pipeline_mode=pl.Buffered(1))


def _split3(x):
    p1 = x.astype(BF16)
    r1 = x - p1.astype(F32)
    p2 = r1.astype(BF16)
    p3 = (r1 - p2.astype(F32)).astype(BF16)
    return p1, p2, p3


def _dot(a, b):
    return jnp.dot(a, b, preferred_element_type=F32)


def _dot_nt(a, b):
    return lax.dot_general(a, b, (((1,), (1,)), ((), ())), preferred_element_type=F32)


def _rms(x, g):
    return x * lax.rsqrt(jnp.mean(x * x, axis=-1, keepdims=True) + NORM_EPS) * g


def _sigmoid(z):
    return 1.0 / (1.0 + jnp.exp(-z))


def _lane_col(x, c):
    return jnp.broadcast_to(x[:, c:c + 1], x.shape)


def _mod_kernel(c_ref, w_ref, b_ref, o_ref):
    c = c_ref[...]
    ca = c * _sigmoid(c)
    o_ref[0] = jnp.dot(ca, w_ref[0], precision=lax.Precision.HIGHEST,
                       preferred_element_type=F32) + b_ref[0]


def _mod_call(c_pad, w_mod, b_mod):
    depth, d, n = w_mod.shape
    rows = c_pad.shape[0]
    tn = 1024
    assert n % tn == 0
    return pl.pallas_call(
        _mod_kernel,
        grid=(depth, n // tn),
        in_specs=[pl.BlockSpec((rows, d), lambda l, j: (0, 0)),
                  pl.BlockSpec((1, d, tn), lambda l, j: (l, 0, j)),
                  pl.BlockSpec((1, 1, tn), lambda l, j: (l, 0, j))],
        out_specs=pl.BlockSpec((1, rows, tn), lambda l, j: (l, 0, j)),
        out_shape=jax.ShapeDtypeStruct((depth, rows, n), F32),
        compiler_params=_params(2),
        name="adaln_mod",
    )(c_pad, w_mod, b_mod.reshape(depth, 1, n))


def _wcat_kernel(w_ref, o_ref, *, o_g, o_f, o_ff):
    x = w_ref[0]
    tr, n_in = x.shape
    n_small = (o_f - o_g) + (n_in - o_ff)
    small = jnp.concatenate([x[:, o_g:o_f], x[:, o_ff:], jnp.zeros((tr, LANES - n_small), F32)], axis=1)
    o_ref[0, :, :o_g] = x[:, :o_g].astype(BF16)
    o_ref[0, :, o_g:o_g + (o_ff - o_f)] = x[:, o_f:o_ff].astype(BF16)
    o_ref[0, :, o_g + (o_ff - o_f):] = small.astype(BF16)


def _wcat_call(w_in, *, o_g, o_f, o_ff, tr):
    depth, d, n_in = w_in.shape
    npad = o_g + (o_ff - o_f) + LANES
    assert o_g % LANES == 0 and (o_ff - o_f) % LANES == 0
    return pl.pallas_call(
        functools.partial(_wcat_kernel, o_g=o_g, o_f=o_f, o_ff=o_ff),
        grid=(depth, d // tr),
        in_specs=[pl.BlockSpec((1, tr, n_in), lambda l, i: (l, i, 0))],
        out_specs=pl.BlockSpec((1, tr, npad), lambda l, i: (l, i, 0)),
        out_shape=jax.ShapeDtypeStruct((depth, d, npad), BF16),
        compiler_params=_params(2),
        name="w_in_layout",
    )(w_in)


def _inproj_kernel(x_ref, g_ref, sh_ref, sc_ref, cos_ref, sin_ref, w_ref,
                   mq_ref, mk_ref, mv_ref, kmean_ref, nq_ref, cv_ref,
                   ks_ref, vs_ref, kw_ref, vw_ref, small_ref, fq_ref, fk_ref, fv_ref,
                   *, hm, hn, hf):
    x = x_ref[0]
    h = _rms(x, g_ref[0]) * (1.0 + sc_ref[0, 0, 0]) + sh_ref[0, 0, 0]
    hb = h.astype(BF16)
    tm = x.shape[0]
    cos = cos_ref[0]
    sin = sin_ref[0]
    lane = lax.broadcasted_iota(jnp.int32, (tm, LANES), 1)

    def rope(y):
        swapped = jnp.where(lane < ROPE_HALF, pltpu.roll(y, LANES - ROPE_HALF, 1),
                            pltpu.roll(y, ROPE_HALF, 1))
        return y * cos + swapped * sin

    def heads(col0, n):
        outs = []
        j = 0
        while j < n:
            w = 2 if j + 1 < n else 1
            y = _dot(hb, w_ref[0, :, (col0 + j) * LANES:(col0 + j + w) * LANES])
            for t in range(w):
                outs.append(y[:, t * LANES:(t + 1) * LANES])
            j += w
        return outs

    col = 0
    for hh, y in enumerate(heads(col, hm)):
        mq_ref[0, hh] = (rope(y) * Q_SCALE).astype(BF16)
    col += hm
    nblk = tm // MOBA_BLOCK
    means = []
    for hh, y in enumerate(heads(col, hm)):
        yr = rope(y)
        mk_ref[0, hh] = yr.astype(BF16)
        means.append(jnp.mean(yr.reshape(nblk, MOBA_BLOCK, LANES), axis=1))
    kmean_ref[0, 0] = jnp.concatenate(means, axis=0)
    col += hm
    for hh, y in enumerate(heads(col, hm)):
        mv_ref[0, hh] = y.astype(BF16)
    col += hm
    for hh, y in enumerate(heads(col, hn)):
        nq_ref[0, hh] = (rope(y) * Q_SCALE).astype(BF16)
    col += hn
    kc, vc, ks, vs, kw, vw = heads(col, NSA_N_KV)
    cv_ref[0, 0] = rope(kc)
    cv_ref[0, 1] = vc
    ks_ref[0] = rope(ks).astype(BF16)
    vs_ref[0] = vs.astype(BF16)
    kw_ref[0] = rope(kw).astype(BF16)
    vw_ref[0] = vw.astype(BF16)
    col += NSA_N_KV
    for hh, y in enumerate(heads(col, hf)):
        fq_ref[0, hh] = (y * Q_SCALE).astype(BF16)
    col += hf
    for hh, y in enumerate(heads(col, hf)):
        fk_ref[0, hh] = y.astype(BF16)
    col += hf
    for hh, y in enumerate(heads(col, hf)):
        fv_ref[0, hh] = y.astype(BF16)
    col += hf
    small_ref[0] = heads(col, 1)[0]


def _layer_vec(l, d):
    return pl.BlockSpec((1, 1, d), lambda *_: (l, 0, 0))


def _mod_vec(l, k, d):
    return pl.BlockSpec((1, 1, 1, 1, d), lambda bb, *_: (l, k, bb, 0, 0))


def _inproj_call(x, g, mod6, cos_t, sin_t, w_cat, *, l, hm, hn, hf, tm):
    b, s, d = x.shape
    npad = w_cat.shape[2]
    nblk = tm // MOBA_BLOCK
    row = lambda bb, i: (bb, i, 0)
    head_spec = lambda n: pl.BlockSpec((1, n, tm, LANES), lambda bb, i: (bb, 0, i, 0))
    head_shape = lambda n, dt=BF16: jax.ShapeDtypeStruct((b, n, s, LANES), dt)
    tok_spec = pl.BlockSpec((1, tm, LANES), row)
    tok_shape = lambda dt=BF16: jax.ShapeDtypeStruct((b, s, LANES), dt)
    out_specs = [head_spec(hm), head_spec(hm), head_spec(hm),
                 pl.BlockSpec((1, 1, hm * nblk, LANES), lambda bb, i: (bb, i, 0, 0)),
                 head_spec(hn), head_spec(2),
                 tok_spec, tok_spec, tok_spec, tok_spec, tok_spec,
                 head_spec(hf), head_spec(hf), head_spec(hf)]
    out_shape = [head_shape(hm), head_shape(hm), head_shape(hm),
                 jax.ShapeDtypeStruct((b, s // tm, hm * nblk, LANES), F32),
                 head_shape(hn), head_shape(2, F32),
                 tok_shape(), tok_shape(), tok_shape(), tok_shape(), tok_shape(F32),
                 head_shape(hf), head_shape(hf), head_shape(hf)]
    return pl.pallas_call(
        functools.partial(_inproj_kernel, hm=hm, hn=hn, hf=hf),
        grid=(b, s // tm),
        in_specs=[pl.BlockSpec((1, tm, d), row),
                  _layer_vec(l, d), _mod_vec(l, 0, d), _mod_vec(l, 1, d),
                  pl.BlockSpec((1, tm, LANES), row),
                  pl.BlockSpec((1, tm, LANES), row),
                  _resident((1, d, npad), lambda bb, i: (l, 0, 0))],
        out_specs=out_specs,
        out_shape=out_shape,
        compiler_params=_params(2),
        name="inproj",
    )(x, g, mod6, mod6, cos_t, sin_t, w_cat)


def _fox_prep_kernel(small_ref, bf_ref, qa_ref, ka_ref, carry_ref, *, hf, col0):
    i = pl.program_id(1)

    @pl.when(i == 0)
    def _():
        carry_ref[...] = jnp.zeros_like(carry_ref)

    z = small_ref[0] + bf_ref[0]
    logf = jnp.minimum(z, 0.0) - jnp.log1p(jnp.exp(-jnp.abs(z)))
    ts = z.shape[0]
    r = lax.broadcasted_iota(jnp.int32, (ts, ts), 0)
    c = lax.broadcasted_iota(jnp.int32, (ts, ts), 1)
    tri = jnp.where(r >= c, 1.0, 0.0).astype(BF16)
    p1, p2, p3 = _split3(logf)
    cum = _dot(tri, p1) + _dot(tri, p2) + _dot(tri, p3) + carry_ref[...]
    carry_ref[...] = cum[ts - 1:ts, :]
    lane = lax.broadcasted_iota(jnp.int32, (ts, LANES), 1)
    for hh in range(hf):
        cb = _lane_col(cum, col0 + hh) * LOG2E
        c1, c2, c3 = (p.astype(F32) for p in _split3(cb))
        qa = jnp.where(lane == 0, c1, jnp.where(lane == 1, c2, jnp.where(
            lane == 2, c3, jnp.where(lane < 6, 1.0, 0.0))))
        ka = jnp.where(lane < 3, 1.0, jnp.where(lane == 3, -c1, jnp.where(
            lane == 4, -c2, jnp.where(lane == 5, -c3, 0.0))))
        qa_ref[0, hh] = qa.astype(BF16)
        ka_ref[0, hh] = ka.astype(BF16)


def _fox_prep_call(small, bf_rows, *, l, hf, col0, ts):
    b, s, _ = small.shape
    spec = pl.BlockSpec((1, hf, ts, LANES), lambda bb, i: (bb, 0, i, 0))
    shape = jax.ShapeDtypeStruct((b, hf, s, LANES), BF16)
    return pl.pallas_call(
        functools.partial(_fox_prep_kernel, hf=hf, col0=col0),
        grid=(b, s // ts),
        in_specs=[pl.BlockSpec((1, ts, LANES), lambda bb, i: (bb, i, 0)),
                  _layer_vec(l, LANES)],
        out_specs=[spec, spec],
        out_shape=[shape, shape],
        scratch_shapes=[pltpu.VMEM((1, LANES), F32)],
        compiler_params=_params(2),
        name="fox_prep",
    )(small, bf_rows)


def _first_max_row(score, blk):
    m = jnp.max(score, axis=0, keepdims=True)
    idx = jnp.min(jnp.where(score == m, blk, float(score.shape[0])), axis=0, keepdims=True)
    return blk == idx, m


def _moba_select_kernel(q_ref, km_ref, o_ref, *, nb, top):
    i = pl.program_id(1)
    q = q_ref[0, 0]
    tq = q.shape[0]
    nbp = -(-nb // 8) * 8
    k1, k2, k3 = _split3(km_ref[0, :nbp, :])
    gate = _dot_nt(k1, q) + _dot_nt(k2, q) + _dot_nt(k3, q)
    blk = lax.broadcasted_iota(jnp.int32, (nbp, tq), 0).astype(F32)
    tok = i * tq + lax.broadcasted_iota(jnp.int32, (nbp, tq), 1)
    own = (tok >> (MOBA_BLOCK.bit_length() - 1)).astype(F32)
    past = blk < own
    score = jnp.where(blk < nb, jnp.where(past, gate, NEG_INF), -jnp.inf)
    sel = jnp.where(blk == own, 1.0, 0.0)
    for _ in range(top):
        pick, _m = _first_max_row(score, blk)
        sel = jnp.where(pick, jnp.where(past, 1.0, sel), sel)
        score = jnp.where(pick, -jnp.inf, score)
    bias = jnp.where(sel > 0.0, 0.0, jnp.where(blk < nb, NEG_INF, 0.0))
    bias = jnp.concatenate([bias, jnp.zeros((LANES - nbp, tq), F32)], axis=0)
    o_ref[0] = bias.T.astype(BF16)


def _moba_select_call(q, kmean_pad, *, nb, tq):
    g, _, s, _ = q.shape
    top = min(MOBA_TOPK, max(nb - 1, 1))
    return pl.pallas_call(
        functools.partial(_moba_select_kernel, nb=nb, top=top),
        grid=(g, s // tq),
        in_specs=[pl.BlockSpec((1, 1, tq, LANES), lambda gg, i: (gg, 0, i, 0)),
                  pl.BlockSpec((1, LANES, LANES), lambda gg, i: (gg, 0, 0))],
        out_specs=pl.BlockSpec((1, tq, LANES), lambda gg, i: (gg, i, 0)),
        out_shape=jax.ShapeDtypeStruct((g, s, LANES), BF16),
        compiler_params=_params(2),
        name="moba_select",
    )(q, kmean_pad)


def _attn_kernel(q_ref, qa_ref, k_ref, ka_ref, v_ref, o_ref,
                 kc_ref, vc_ref, s_ref, m_ref, acc_ref, *, hs, tq, tk, ns):
    rows = hs * tq
    seq = k_ref.shape[1]
    kc_ref[:, :LANES] = k_ref[0]
    kc_ref[:, LANES:] = ka_ref[0]
    vc_ref[:, :LANES] = v_ref[0]
    vc_ref[:, LANES:] = jnp.ones((seq, LANES), BF16)
    row_in_tile = lax.broadcasted_iota(jnp.int32, (rows, LANES), 0) & (tq - 1)
    col = lax.broadcasted_iota(jnp.int32, (rows, tk), 1)

    def logits(i, j, buf):
        qs = pl.multiple_of(i * tq, tq)
        qc = jnp.concatenate([q_ref[0, :, pl.ds(qs, tq), :].reshape(rows, LANES),
                              jnp.concatenate([qa_ref[0, pl.ds(qs, tq), :]] * hs, axis=0)], axis=-1)
        s_ref[buf] = _dot_nt(qc, kc_ref[pl.ds(pl.multiple_of(j * tk, tk), tk), :])

    def update(st, i, j, buf):
        limit = jnp.tile(row_in_tile + (i * tq - j * tk), (1, tk // LANES))
        s = jnp.where(col <= limit, s_ref[buf], NEG_INF)
        m_prev = jnp.where(j == 0, -jnp.inf, m_ref[st])
        m_new = jnp.maximum(m_prev, jnp.max(s, axis=-1, keepdims=True))
        alpha = jnp.exp2(m_prev - m_new)
        p = jnp.exp2(s - jnp.tile(m_new, (1, tk // LANES))).astype(BF16)
        m_ref[st] = m_new
        acc = (jnp.tile(alpha, (1, 2)) * acc_ref[st]
               + _dot(p, vc_ref[pl.ds(pl.multiple_of(j * tk, tk), tk), :]))
        acc_ref[st] = acc
        out = (acc[:, :LANES] / acc[:, LANES:]).astype(o_ref.dtype)
        qs = pl.multiple_of(i * tq, tq)
        for hh in range(hs):
            o_ref[0, hh, pl.ds(qs, tq), :] = out[hh * tq:(hh + 1) * tq]

    def advance(i, j):
        wrap = j == (i * tq) // tk
        return jnp.where(wrap, i + ns, i), jnp.where(wrap, 0, j + 1)

    counts = [sum((i * tq) // tk + 1 for i in range(st, seq // tq, ns)) for st in range(ns)]
    assert min(counts) == max(counts)
    n_pairs = counts[0]
    zero = jnp.int32(0)
    acc_ref[...] = jnp.zeros_like(acc_ref)
    m_ref[...] = jnp.zeros_like(m_ref)
    first = [(jnp.int32(st), zero) for st in range(ns)]
    for st in range(ns):
        logits(*first[st], 2 * st)

    def trip(carry, cur, nxt):
        new = []
        for st in range(ns):
            ib, jb, ia, ja = carry[4 * st:4 * st + 4]
            logits(ia, ja, 2 * st + nxt)
            new += [ia, ja, *advance(ia, ja)]
        for st in range(ns):
            update(st, carry[4 * st], carry[4 * st + 1], 2 * st + cur)
        return tuple(new)

    def body(_, carry):
        return trip(trip(carry, 0, 1), 1, 0)

    carry = tuple(v for st in range(ns) for v in (*first[st], *advance(*first[st])))
    carry = lax.fori_loop(0, (n_pairs - 1) // 2, body, carry)
    last = 0
    if (n_pairs - 1) % 2:
        carry = trip(carry, 0, 1)
        last = 1
    for st in range(ns):
        update(st, carry[4 * st], carry[4 * st + 1], 2 * st + last)


def _attn_call(q, qa, k, ka, v, *, tq, tk, out_dtype, kv_groups=1, ka_shared=False):
    g, hs, s, _ = q.shape
    assert tq & (tq - 1) == 0 and tk % tq == 0 and s % tk == 0
    rows = hs * tq
    ns = 2 if (tk // tq) % 2 == 0 and (s // tq) % 2 == 0 else 1
    seq = lambda gg: (gg // kv_groups, 0, 0)
    ka_map = (lambda gg: (0, 0, 0)) if ka_shared else seq
    return pl.pallas_call(
        functools.partial(_attn_kernel, hs=hs, tq=tq, tk=tk, ns=ns),
        grid=(g,),
        in_specs=[pl.BlockSpec((1, hs, s, LANES), lambda gg: (gg, 0, 0, 0)),
                  pl.BlockSpec((1, s, LANES), seq),
                  _resident((1, s, LANES), seq),
                  _resident((1, s, LANES), ka_map),
                  _resident((1, s, LANES), seq)],
        out_specs=pl.BlockSpec((1, hs, s, LANES), lambda gg: (gg, 0, 0, 0)),
        out_shape=jax.ShapeDtypeStruct((g, hs, s, LANES), out_dtype),
        scratch_shapes=[pltpu.VMEM((s, 2 * LANES), BF16),
                        pltpu.VMEM((s, 2 * LANES), BF16),
                        pltpu.VMEM((2 * ns, rows, tk), F32),
                        pltpu.VMEM((ns, rows, LANES), F32),
                        pltpu.VMEM((ns, rows, 2 * LANES), F32)],
        compiler_params=_params(1),
        name="flash_attn",
    )(q, qa, k, ka, v)


def _compress_kernel(x_ref, pe_ref, w1_ref, w2_ref, o_ref):
    x = x_ref[0, 0]
    half = x.shape[1]
    pe = pe_ref[0, 0]
    top = _dot((x + pe[:, :half]).astype(BF16), w1_ref[0, 0, :half, :])
    bot = _dot((x + pe[:, half:]).astype(BF16), w1_ref[0, 0, half:, :])
    n = x.shape[0]
    hid = top + pltpu.roll(bot, n - 1, 0)
    act = 0.5 * hid * (1.0 + jnp.tanh(0.7978845608028654 * (hid + 0.044715 * hid * hid * hid)))
    o_ref[0, 0] = _dot(act.astype(BF16), w2_ref[0, 0]).astype(o_ref.dtype)


def _compress_call(cv, pe, w1, w2, *, l):
    b, _, s, _ = cv.shape
    n = s // NSA_CMP_STRIDE
    width = NSA_CMP_STRIDE * LANES
    x2 = cv.reshape(b, 2, n, width)
    return pl.pallas_call(
        _compress_kernel,
        grid=(b, 2),
        in_specs=[pl.BlockSpec((1, 1, n, width), lambda bb, t: (bb, t, 0, 0)),
                  pl.BlockSpec((1, 1, 1, 2 * width), lambda bb, t: (l, t, 0, 0)),
                  pl.BlockSpec((1, 1, 2 * width, NSA_CMP_HIDDEN), lambda bb, t: (l, t, 0, 0)),
                  pl.BlockSpec((1, 1, NSA_CMP_HIDDEN, LANES), lambda bb, t: (l, t, 0, 0))],
        out_specs=pl.BlockSpec((1, 1, n, LANES), lambda bb, t: (bb, t, 0, 0)),
        out_shape=jax.ShapeDtypeStruct((b, 2, n, LANES), BF16),
        compiler_params=_params(2),
        name="nsa_compress",
    )(x2, pe, w1, w2)


def _nsa_local_kernel(q_ref, kc_ref, vc_ref, ovt_ref, kw_ref, vw_ref, small_ref,
                      ocw_ref, bias_ref, *, hn, tq, n_sb):
    i = pl.program_id(1)
    rows = hn * tq
    q = q_ref[0].reshape(rows, LANES)
    gates = _sigmoid(small_ref[0])
    nc = kc_ref.shape[2]
    s = _dot_nt(q, kc_ref[0, 0])
    r = lax.broadcasted_iota(jnp.int32, (rows, nc), 0)
    tok = i * tq + (r & (tq - 1))
    cend = lax.broadcasted_iota(jnp.int32, (rows, nc), 1) * NSA_CMP_STRIDE + (NSA_CMP_LEN - 1)
    s = jnp.where(cend <= tok, s, NEG_INF)
    e = jnp.exp2(s - jnp.max(s, axis=-1, keepdims=True))
    tok1 = i * tq + (lax.broadcasted_iota(jnp.int32, (rows, 1), 0) & (tq - 1))
    rinv = jnp.where(tok1 >= NSA_CMP_LEN - 1, 1.0 / jnp.sum(e, axis=-1, keepdims=True), 0.0)
    o_cmp = (_dot(e.astype(BF16), vc_ref[0, 0]) * rinv).reshape(hn, tq, LANES)
    psum = jnp.sum((e * rinv).reshape(hn, tq, nc), axis=0)
    hi = psum.astype(BF16)
    lo = (psum - hi.astype(F32)).astype(BF16)
    imp = _dot_nt(ovt_ref[...], hi) + _dot_nt(ovt_ref[...], lo)
    blk = lax.broadcasted_iota(jnp.int32, (LANES, tq), 0).astype(F32)
    cur = ((i * tq + lax.broadcasted_iota(jnp.int32, (LANES, tq), 1))
           >> (NSA_SEL_BLOCK.bit_length() - 1)).astype(F32)
    score = jnp.where(blk >= 1.0, jnp.where(blk <= cur - 2.0, imp, -jnp.inf), -jnp.inf)
    sel = jnp.where(blk == 0.0, 1.0, jnp.where(blk == cur, 1.0, jnp.where(blk == cur - 1.0, 1.0, 0.0)))
    for _ in range(min(NSA_N_SEL, n_sb) - 3):
        pick, m = _first_max_row(score, blk)
        sel = jnp.where(pick, jnp.where(m > -jnp.inf, 1.0, sel), sel)
        score = jnp.where(pick, -jnp.inf, score)
    bias_ref[0] = jnp.where(sel > 0.0, 0.0, NEG_INF).T.astype(BF16)
    span = NSA_WINDOW + tq
    start = pl.multiple_of(jnp.maximum(i * tq - NSA_WINDOW, 0), tq)
    sw = _dot_nt(q, kw_ref[0, pl.ds(start, span), :])
    rw = lax.broadcasted_iota(jnp.int32, (rows, span), 0)
    back = (i * tq - start) + (rw & (tq - 1)) - lax.broadcasted_iota(jnp.int32, (rows, span), 1)
    sw = jnp.where(lax.bitcast_convert_type(back, jnp.uint32) < NSA_WINDOW, sw, NEG_INF)
    ew = jnp.exp2(sw - jnp.max(sw, axis=-1, keepdims=True)).astype(BF16)
    vw1 = jnp.concatenate([vw_ref[0, pl.ds(start, span), :], jnp.ones((span, LANES), BF16)], axis=-1)
    ow = _dot(ew, vw1)
    o_win = (ow[:, :LANES] / ow[:, LANES:]).reshape(hn, tq, LANES)
    for hh in range(hn):
        ocw_ref[0, hh] = (_lane_col(gates, hh * NSA_N_BRANCH) * o_cmp[hh]
                          + _lane_col(gates, hh * NSA_N_BRANCH + 2) * o_win[hh])


def _nsa_local_call(q, kvc, overlap, kw, vw, small, *, tq):
    b, hn, s, _ = q.shape
    nc = kvc.shape[2]
    n_sb = s // NSA_SEL_BLOCK
    assert n_sb <= LANES and tq & (tq - 1) == 0 and NSA_WINDOW % tq == 0
    seq = lambda bb, i: (bb, 0, 0)
    return pl.pallas_call(
        functools.partial(_nsa_local_kernel, hn=hn, tq=tq, n_sb=n_sb),
        grid=(b, s // tq),
        in_specs=[pl.BlockSpec((1, hn, tq, LANES), lambda bb, i: (bb, 0, i, 0)),
                  pl.BlockSpec((1, 1, nc, LANES), lambda bb, i: (bb, 0, 0, 0)),
                  pl.BlockSpec((1, 1, nc, LANES), lambda bb, i: (bb, 1, 0, 0)),
                  pl.BlockSpec((LANES, nc), lambda bb, i: (0, 0)),
                  pl.BlockSpec((1, s, LANES), seq),
                  pl.BlockSpec((1, s, LANES), seq),
                  pl.BlockSpec((1, tq, LANES), lambda bb, i: (bb, i, 0))],
        out_specs=[pl.BlockSpec((1, hn, tq, LANES), lambda bb, i: (bb, 0, i, 0)),
                   pl.BlockSpec((1, tq, LANES), lambda bb, i: (bb, i, 0))],
        out_shape=[jax.ShapeDtypeStruct((b, hn, s, LANES), F32),
                   jax.ShapeDtypeStruct((b, s, LANES), BF16)],
        compiler_params=_params(2),
        name="nsa_local",
    )(q, kvc, kvc, overlap, kw, vw, small)


def _outproj_kernel(om_ref, ocw_ref, osl_ref, small_ref, of_ref, w_ref, x_ref,
                    gpost_ref, gate_ref, gpre_ref, sh_ref, sc_ref, xo_ref, h_ref,
                    *, hm, hn, hf):
    gates = _sigmoid(small_ref[0])
    parts = [om_ref[0, hh] for hh in range(hm)]
    for hh in range(hn):
        o = ocw_ref[0, hh] + _lane_col(gates, hh * NSA_N_BRANCH + 1) * osl_ref[0, hh]
        parts.append(o.astype(BF16))
    parts += [of_ref[0, hh] for hh in range(hf)]
    y = _dot(jnp.concatenate(parts, axis=-1), w_ref[0])
    xn = x_ref[0] + gate_ref[0, 0, 0] * _rms(y, gpost_ref[0])
    xo_ref[0] = xn
    h_ref[0] = (_rms(xn, gpre_ref[0]) * (1.0 + sc_ref[0, 0, 0]) + sh_ref[0, 0, 0]).astype(BF16)


def _outproj_call(o_moba, o_cw, o_slc, small, o_fox, w_out, x, g_post, g_pre, mod6, *, l, tm):
    b, s, d = x.shape
    hm, hn, hf = o_moba.shape[1], o_cw.shape[1], o_fox.shape[1]
    head_spec = lambda n: pl.BlockSpec((1, n, tm, LANES), lambda bb, i: (bb, 0, i, 0))
    row = lambda bb, i: (bb, i, 0)
    return pl.pallas_call(
        functools.partial(_outproj_kernel, hm=hm, hn=hn, hf=hf),
        grid=(b, s // tm),
        in_specs=[head_spec(hm), head_spec(hn), head_spec(hn),
                  pl.BlockSpec((1, tm, LANES), row), head_spec(hf),
                  _resident((1,) + w_out.shape[1:], lambda bb, i: (l, 0, 0)),
                  pl.BlockSpec((1, tm, d), row), _layer_vec(l, d), _mod_vec(l, 2, d),
                  _layer_vec(l, d), _mod_vec(l, 3, d), _mod_vec(l, 4, d)],
        out_specs=[pl.BlockSpec((1, tm, d), row), pl.BlockSpec((1, tm, d), row)],
        out_shape=[jax.ShapeDtypeStruct((b, s, d), F32), jax.ShapeDtypeStruct((b, s, d), BF16)],
        compiler_params=_params(2),
        name="outproj",
    )(o_moba, o_cw, o_slc, small, o_fox, w_out, x, g_post, mod6, g_pre, mod6, mod6)


def _mlp_kernel(h_ref, wu_ref, wd_ref, x_ref, g_ref, gate_ref, o_ref, acc_ref):
    f = pl.program_id(2)

    @pl.when(f == 0)
    def _():
        acc_ref[...] = jnp.zeros_like(acc_ref)

    u = jnp.maximum(_dot(h_ref[0], wu_ref[0]), 0.0)
    acc_ref[...] += _dot((u * u).astype(BF16), wd_ref[0])

    @pl.when(f == pl.num_programs(2) - 1)
    def _():
        o_ref[0] = x_ref[0] + gate_ref[0, 0, 0] * _rms(acc_ref[...], g_ref[0])


def _mlp_call(h, w_up, w_down, x, g_post, mod6, *, l, tm, tf):
    b, s, d = x.shape
    dff = w_up.shape[2]
    row = lambda bb, i, f: (bb, i, 0)
    return pl.pallas_call(
        _mlp_kernel,
        grid=(b, s // tm, dff // tf),
        in_specs=[pl.BlockSpec((1, tm, d), row),
                  pl.BlockSpec((1, d, tf), lambda bb, i, f: (l, 0, f)),
                  pl.BlockSpec((1, tf, d), lambda bb, i, f: (l, f, 0)),
                  pl.BlockSpec((1, tm, d), row),
                  _layer_vec(l, d), _mod_vec(l, 5, d)],
        out_specs=pl.BlockSpec((1, tm, d), row),
        out_shape=jax.ShapeDtypeStruct((b, s, d), F32),
        scratch_shapes=[pltpu.VMEM((tm, d), F32)],
        compiler_params=_params(3),
        name="mlp",
    )(h, w_up, w_down, x, g_post, mod6)


def _tile(n, pref):
    t = min(n, pref)
    assert n % t == 0
    return t


def _rope_tables(positions):
    inv_freq = ROPE_THETA ** (-jnp.arange(ROPE_HALF, dtype=F32) / ROPE_HALF)
    ang = positions.astype(F32)[..., None] * inv_freq
    cos, sin = jnp.cos(ang), jnp.sin(ang)
    ones = jnp.ones(ang.shape[:-1] + (LANES - ROPE_DIMS,), F32)
    cos_t = jnp.concatenate([cos, cos, ones], axis=-1)
    sin_t = jnp.concatenate([-sin, sin, 0.0 * ones], axis=-1)
    return cos_t, sin_t


def _block_onehot(s, block):
    ids = np.arange(s) // block
    return jnp.asarray((ids[:, None] == np.arange(LANES)[None, :]).astype(np.float32), dtype=BF16)[None]


def _overlap_matrix(s):
    nc = s // NSA_CMP_STRIDE
    n_cmp = (s - NSA_CMP_LEN) // NSA_CMP_STRIDE + 1
    c = np.arange(nc)
    first = (c * NSA_CMP_STRIDE) // NSA_SEL_BLOCK
    last = (c * NSA_CMP_STRIDE + NSA_CMP_LEN - 1) // NSA_SEL_BLOCK
    sb = np.arange(LANES)
    ov = (sb[None, :] >= first[:, None]) & (sb[None, :] <= last[:, None]) & (c[:, None] < n_cmp)
    return jnp.asarray(ov.T.astype(np.float32), dtype=BF16)


def kernel(x, c, positions, w_mod, b_mod, g_pre_mix, g_post_mix, g_pre_mlp, g_post_mlp, w_in,
           b_forget, cmp_pe_k, cmp_pe_v, cmp_w1_k, cmp_w2_k, cmp_w1_v, cmp_w2_v, w_out, w_up, w_down):
    b, s, d = x.shape
    depth = w_mod.shape[0]
    n_heads = d // HEAD_DIM
    hm = n_heads // 4
    hn = n_heads // 4
    hf = n_heads - hm - hn
    mw, nw, fw = hm * HEAD_DIM, hn * HEAD_DIM, hf * HEAD_DIM
    n_gate = hn * NSA_N_BRANCH
    o_g = 3 * mw + nw + NSA_N_KV * HEAD_DIM
    o_f = o_g + n_gate
    o_ff = o_f + 3 * fw
    assert n_gate + hf <= LANES and s % MOBA_BLOCK == 0 and s // MOBA_BLOCK <= LANES

    c_pad = jnp.zeros((8, d), F32).at[:b].set(c)
    mod = _mod_call(c_pad, w_mod, b_mod)
    mod6 = mod[:, :b].reshape(depth, b, 6, 1, d).transpose(0, 2, 1, 3, 4)
    cos_t, sin_t = _rope_tables(positions)
    moba_ka = _block_onehot(s, MOBA_BLOCK)
    slc_ka = _block_onehot(s, NSA_SEL_BLOCK)
    overlap = _overlap_matrix(s)
    nb = s // MOBA_BLOCK

    w_cat = _wcat_call(w_in, o_g=o_g, o_f=o_f, o_ff=o_ff, tr=_tile(d, 256))
    w_out_b, w_up_b, w_down_b = w_out.astype(BF16), w_up.astype(BF16), w_down.astype(BF16)
    cmp_pe = jnp.stack([cmp_pe_k, cmp_pe_v], axis=1).reshape(depth, 2, 1, NSA_CMP_LEN * HEAD_DIM)
    cmp_w1 = jnp.stack([cmp_w1_k, cmp_w1_v], axis=1).astype(BF16)
    cmp_w2 = jnp.stack([cmp_w2_k, cmp_w2_v], axis=1).astype(BF16)
    bf_rows = jnp.zeros((depth, 1, LANES), F32).at[:, 0, n_gate:n_gate + hf].set(b_forget)
    gains = [g.reshape(depth, 1, d) for g in (g_pre_mix, g_post_mix, g_pre_mlp, g_post_mlp)]

    tm_in = _tile(s, 512)
    for l in range(depth):
        (mq, mk, mv, kmean, nq, cv, ks, vs, kw, vw, small, fq, fk, fv) = _inproj_call(
            x, gains[0], mod6, cos_t, sin_t, w_cat, l=l, hm=hm, hn=hn, hf=hf, tm=tm_in)

        fqa, fka = _fox_prep_call(small, bf_rows, l=l, hf=hf, col0=n_gate, ts=_tile(s, 512))
        fold = lambda t: t.reshape(b * t.shape[1], 1, s, LANES)
        flat = lambda t: t.reshape(b * t.shape[1], s, LANES)
        o_fox = _attn_call(fold(fq), flat(fqa), flat(fk), flat(fka), flat(fv),
                           tq=_tile(s, 512), tk=_tile(s, 1024), out_dtype=BF16).reshape(b, hf, s, LANES)

        kmean = kmean.reshape(b, s // tm_in, hm, tm_in // MOBA_BLOCK, LANES)
        kmean = kmean.transpose(0, 2, 1, 3, 4).reshape(b * hm, nb, LANES)
        kmean = jnp.pad(kmean, ((0, 0), (0, LANES - nb), (0, 0)))
        mqa = _moba_select_call(fold(mq), kmean, nb=nb, tq=_tile(s, 512))
        o_moba = _attn_call(fold(mq), mqa, flat(mk), moba_ka, flat(mv), tq=_tile(s, 512),
                            tk=_tile(s, 1024), out_dtype=BF16, ka_shared=True).reshape(b, hm, s, LANES)

        kvc = _compress_call(cv, cmp_pe, cmp_w1, cmp_w2, l=l)
        o_cw, sbias = _nsa_local_call(nq, kvc, overlap, kw, vw, small, tq=_tile(s, 256))
        o_slc = _attn_call(fold(nq), sbias, ks, slc_ka, vs, tq=_tile(s, 512), tk=_tile(s, 1024),
                           out_dtype=BF16, kv_groups=hn, ka_shared=True).reshape(b, hn, s, LANES)

        x, h_mlp = _outproj_call(o_moba, o_cw, o_slc, small, o_fox, w_out_b, x,
                                 gains[1], gains[2], mod6, l=l, tm=_tile(s, 512))
        x = _mlp_call(h_mlp, w_up_b, w_down_b, x, gains[3], mod6, l=l,
                      tm=_tile(s, 512), tf=1024)
    return x
```

```python
import functools

import jax
import jax.numpy as jnp
import numpy as np
from jax import lax
from jax.experimental import pallas as pl
from jax.experimental.pallas import tpu as pltpu

HEAD_DIM = 128
ROPE_THETA = 500000.0
ROPE_DIMS = HEAD_DIM // 4
ROPE_HALF = ROPE_DIMS // 2
MOBA_BLOCK = 256
MOBA_TOPK = 3
NSA_CMP_LEN = 32
NSA_CMP_STRIDE = 16
NSA_CMP_HIDDEN = 256
NSA_SEL_BLOCK = 64
NSA_N_SEL = 16
NSA_WINDOW = 512
NSA_N_BRANCH = 3
NSA_N_KV = 6
NORM_EPS = 1e-6
NEG_INF = -1e30
ATTN_SCALE = HEAD_DIM ** -0.5
LOG2E = 1.4426950408889634
Q_SCALE = ATTN_SCALE * LOG2E

LANES = 128
VMEM_LIMIT_BYTES = 56 * 1024 * 1024

F32 = jnp.float32
BF16 = jnp.bfloat16


def _params(n_axes):
    return pltpu.CompilerParams(dimension_semantics=("arbitrary",) * n_axes,
                                vmem_limit_bytes=VMEM_LIMIT_BYTES)


def _resident(block_shape, index_map):
    return pl.BlockSpec(block_shape, index_map, pipeline_mode=pl.Buffered(1))


def _split3(x):
    p1 = x.astype(BF16)
    r1 = x - p1.astype(F32)
    p2 = r1.astype(BF16)
    p3 = (r1 - p2.astype(F32)).astype(BF16)
    return p1, p2, p3


def _dot(a, b):
    return jnp.dot(a, b, preferred_element_type=F32)


def _dot_nt(a, b):
    return lax.dot_general(a, b, (((1,), (1,)), ((), ())), preferred_element_type=F32)


def _rms(x, g):
    return x * lax.rsqrt(jnp.mean(x * x, axis=-1, keepdims=True) + NORM_EPS) * g


def _sigmoid(z):
    return 1.0 / (1.0 + jnp.exp(-z))


FUTURE_COL0 = LANES // 2


def _future_tile_columns(col, tile):
    c = col - FUTURE_COL0
    qa = jnp.where(c > tile, NEG_INF, 0.0)
    ka = jnp.where(c == tile, 1.0, 0.0)
    return qa, ka


def _lane_col(x, c):
    return jnp.broadcast_to(x[:, c:c + 1], x.shape)


def _mod_kernel(c_ref, w_ref, b_ref, o_ref):
    c = c_ref[...]
    ca = c * _sigmoid(c)
    o_ref[0] = jnp.dot(ca, w_ref[0], precision=lax.Precision.HIGHEST,
                       preferred_element_type=F32) + b_ref[0]


def _mod_call(c_pad, w_mod, b_mod):
    depth, d, n = w_mod.shape
    rows = c_pad.shape[0]
    tn = 1024
    assert n % tn == 0
    return pl.pallas_call(
        _mod_kernel,
        grid=(depth, n // tn),
        in_specs=[pl.BlockSpec((rows, d), lambda l, j: (0, 0)),
                  pl.BlockSpec((1, d, tn), lambda l, j: (l, 0, j)),
                  pl.BlockSpec((1, 1, tn), lambda l, j: (l, 0, j))],
        out_specs=pl.BlockSpec((1, rows, tn), lambda l, j: (l, 0, j)),
        out_shape=jax.ShapeDtypeStruct((depth, rows, n), F32),
        compiler_params=_params(2),
        name="adaln_mod",
    )(c_pad, w_mod, b_mod.reshape(depth, 1, n))


def _wcat_kernel(w_ref, o_ref, *, o_g, o_f, o_ff):
    x = w_ref[0]
    tr, n_in = x.shape
    n_small = (o_f - o_g) + (n_in - o_ff)
    small = jnp.concatenate([x[:, o_g:o_f], x[:, o_ff:], jnp.zeros((tr, LANES - n_small), F32)], axis=1)
    o_ref[0, :, :o_g] = x[:, :o_g].astype(BF16)
    o_ref[0, :, o_g:o_g + (o_ff - o_f)] = x[:, o_f:o_ff].astype(BF16)
    o_ref[0, :, o_g + (o_ff - o_f):] = small.astype(BF16)


def _wcat_call(w_in, *, o_g, o_f, o_ff, tr):
    depth, d, n_in = w_in.shape
    npad = o_g + (o_ff - o_f) + LANES
    assert o_g % LANES == 0 and (o_ff - o_f) % LANES == 0
    return pl.pallas_call(
        functools.partial(_wcat_kernel, o_g=o_g, o_f=o_f, o_ff=o_ff),
        grid=(depth, d // tr),
        in_specs=[pl.BlockSpec((1, tr, n_in), lambda l, i: (l, i, 0))],
        out_specs=pl.BlockSpec((1, tr, npad), lambda l, i: (l, i, 0)),
        out_shape=jax.ShapeDtypeStruct((depth, d, npad), BF16),
        compiler_params=_params(2),
        name="w_in_layout",
    )(w_in)


def _inproj_kernel(x_ref, g_ref, sh_ref, sc_ref, cos_ref, sin_ref, w_ref,
                   mq_ref, mk_ref, mv_ref, kmean_ref, nq_ref, cv_ref,
                   ks_ref, vs_ref, kw_ref, vw_ref, small_ref, fq_ref, fk_ref, fv_ref,
                   *, hm, hn, hf):
    x = x_ref[0]
    h = _rms(x, g_ref[0]) * (1.0 + sc_ref[0, 0, 0]) + sh_ref[0, 0, 0]
    hb = h.astype(BF16)
    tm = x.shape[0]
    cos = cos_ref[0]
    sin = sin_ref[0]
    lane = lax.broadcasted_iota(jnp.int32, (tm, LANES), 1)

    def rope(y):
        swapped = jnp.where(lane < ROPE_HALF, pltpu.roll(y, LANES - ROPE_HALF, 1),
                            pltpu.roll(y, ROPE_HALF, 1))
        return y * cos + swapped * sin

    def heads(col0, n):
        outs = []
        j = 0
        while j < n:
            w = 2 if j + 1 < n else 1
            y = _dot(hb, w_ref[0, :, (col0 + j) * LANES:(col0 + j + w) * LANES])
            for t in range(w):
                outs.append(y[:, t * LANES:(t + 1) * LANES])
            j += w
        return outs

    col = 0
    for hh, y in enumerate(heads(col, hm)):
        mq_ref[0, hh] = (rope(y) * Q_SCALE).astype(BF16)
    col += hm
    nblk = tm // MOBA_BLOCK
    means = []
    for hh, y in enumerate(heads(col, hm)):
        yr = rope(y)
        mk_ref[0, hh] = yr.astype(BF16)
        means.append(jnp.mean(yr.reshape(nblk, MOBA_BLOCK, LANES), axis=1))
    kmean_ref[0, 0] = jnp.concatenate(means, axis=0)
    col += hm
    for hh, y in enumerate(heads(col, hm)):
        mv_ref[0, hh] = y.astype(BF16)
    col += hm
    for hh, y in enumerate(heads(col, hn)):
        nq_ref[0, hh] = (rope(y) * Q_SCALE).astype(BF16)
    col += hn
    kc, vc, ks, vs, kw, vw = heads(col, NSA_N_KV)
    cv_ref[0, 0] = rope(kc)
    cv_ref[0, 1] = vc
    ks_ref[0] = rope(ks).astype(BF16)
    vs_ref[0] = vs.astype(BF16)
    kw_ref[0] = rope(kw).astype(BF16)
    vw_ref[0] = vw.astype(BF16)
    col += NSA_N_KV
    for hh, y in enumerate(heads(col, hf)):
        fq_ref[0, hh] = (y * Q_SCALE).astype(BF16)
    col += hf
    for hh, y in enumerate(heads(col, hf)):
        fk_ref[0, hh] = y.astype(BF16)
    col += hf
    for hh, y in enumerate(heads(col, hf)):
        fv_ref[0, hh] = y.astype(BF16)
    col += hf
    small_ref[0] = heads(col, 1)[0]


def _layer_vec(l, d):
    return pl.BlockSpec((1, 1, d), lambda *_: (l, 0, 0))


def _mod_vec(l, k, d):
    return pl.BlockSpec((1, 1, 1, 1, d), lambda bb, *_: (l, k, bb, 0, 0))


def _inproj_call(x, g, mod6, cos_t, sin_t, w_cat, *, l, hm, hn, hf, tm):
    b, s, d = x.shape
    npad = w_cat.shape[2]
    nblk = tm // MOBA_BLOCK
    row = lambda bb, i: (bb, i, 0)
    head_spec = lambda n: pl.BlockSpec((1, n, tm, LANES), lambda bb, i: (bb, 0, i, 0))
    head_shape = lambda n, dt=BF16: jax.ShapeDtypeStruct((b, n, s, LANES), dt)
    tok_spec = pl.BlockSpec((1, tm, LANES), row)
    tok_shape = lambda dt=BF16: jax.ShapeDtypeStruct((b, s, LANES), dt)
    out_specs = [head_spec(hm), head_spec(hm), head_spec(hm),
                 pl.BlockSpec((1, 1, hm * nblk, LANES), lambda bb, i: (bb, i, 0, 0)),
                 head_spec(hn), head_spec(2),
                 tok_spec, tok_spec, tok_spec, tok_spec, tok_spec,
                 head_spec(hf), head_spec(hf), head_spec(hf)]
    out_shape = [head_shape(hm), head_shape(hm), head_shape(hm),
                 jax.ShapeDtypeStruct((b, s // tm, hm * nblk, LANES), F32),
                 head_shape(hn), head_shape(2, F32),
                 tok_shape(), tok_shape(), tok_shape(), tok_shape(), tok_shape(F32),
                 head_shape(hf), head_shape(hf), head_shape(hf)]
    return pl.pallas_call(
        functools.partial(_inproj_kernel, hm=hm, hn=hn, hf=hf),
        grid=(b, s // tm),
        in_specs=[pl.BlockSpec((1, tm, d), row),
                  _layer_vec(l, d), _mod_vec(l, 0, d), _mod_vec(l, 1, d),
                  pl.BlockSpec((1, tm, LANES), row),
                  pl.BlockSpec((1, tm, LANES), row),
                  _resident((1, d, npad), lambda bb, i: (l, 0, 0))],
        out_specs=out_specs,
        out_shape=out_shape,
        compiler_params=_params(2),
        name="inproj",
    )(x, g, mod6, mod6, cos_t, sin_t, w_cat)


def _fox_prep_kernel(small_ref, bf_ref, qa_ref, ka_ref, carry_ref, *, hf, col0):
    i = pl.program_id(1)

    @pl.when(i == 0)
    def _():
        carry_ref[...] = jnp.zeros_like(carry_ref)

    z = small_ref[0] + bf_ref[0]
    logf = jnp.minimum(z, 0.0) - jnp.log1p(jnp.exp(-jnp.abs(z)))
    ts = z.shape[0]
    r = lax.broadcasted_iota(jnp.int32, (ts, ts), 0)
    c = lax.broadcasted_iota(jnp.int32, (ts, ts), 1)
    tri = jnp.where(r >= c, 1.0, 0.0).astype(BF16)
    p1, p2, p3 = _split3(logf)
    cum = _dot(tri, p1) + _dot(tri, p2) + _dot(tri, p3) + carry_ref[...]
    carry_ref[...] = cum[ts - 1:ts, :]
    lane = lax.broadcasted_iota(jnp.int32, (ts, LANES), 1)
    tile = (i * ts + lax.broadcasted_iota(jnp.int32, (ts, LANES), 0)) >> (LANES.bit_length() - 1)
    qa_future, ka_future = _future_tile_columns(lane, tile)
    for hh in range(hf):
        cb = _lane_col(cum, col0 + hh) * LOG2E
        c1, c2, c3 = (p.astype(F32) for p in _split3(cb))
        qa = jnp.where(lane == 0, c1, jnp.where(lane == 1, c2, jnp.where(
            lane == 2, c3, jnp.where(lane < 6, 1.0, qa_future))))
        ka = jnp.where(lane < 3, 1.0, jnp.where(lane == 3, -c1, jnp.where(
            lane == 4, -c2, jnp.where(lane == 5, -c3, ka_future))))
        qa_ref[0, hh] = qa.astype(BF16)
        ka_ref[0, hh] = ka.astype(BF16)


def _fox_prep_call(small, bf_rows, *, l, hf, col0, ts):
    b, s, _ = small.shape
    spec = pl.BlockSpec((1, hf, ts, LANES), lambda bb, i: (bb, 0, i, 0))
    shape = jax.ShapeDtypeStruct((b, hf, s, LANES), BF16)
    return pl.pallas_call(
        functools.partial(_fox_prep_kernel, hf=hf, col0=col0),
        grid=(b, s // ts),
        in_specs=[pl.BlockSpec((1, ts, LANES), lambda bb, i: (bb, i, 0)),
                  _layer_vec(l, LANES)],
        out_specs=[spec, spec],
        out_shape=[shape, shape],
        scratch_shapes=[pltpu.VMEM((1, LANES), F32)],
        compiler_params=_params(2),
        name="fox_prep",
    )(small, bf_rows)


def _first_max_row(score, blk):
    m = jnp.max(score, axis=0, keepdims=True)
    idx = jnp.min(jnp.where(score == m, blk, float(score.shape[0])), axis=0, keepdims=True)
    return blk == idx, m


def _moba_select_kernel(q_ref, km_ref, o_ref, *, nb, top):
    i = pl.program_id(1)
    q = q_ref[0, 0]
    tq = q.shape[0]
    nbp = -(-nb // 8) * 8
    k1, k2, k3 = _split3(km_ref[0, :nbp, :])
    gate = _dot_nt(k1, q) + _dot_nt(k2, q) + _dot_nt(k3, q)
    blk = lax.broadcasted_iota(jnp.int32, (nbp, tq), 0).astype(F32)
    tok = i * tq + lax.broadcasted_iota(jnp.int32, (nbp, tq), 1)
    own = (tok >> (MOBA_BLOCK.bit_length() - 1)).astype(F32)
    past = blk < own
    score = jnp.where(blk < nb, jnp.where(past, gate, NEG_INF), -jnp.inf)
    sel = jnp.where(blk == own, 1.0, 0.0)
    for _ in range(top):
        pick, _m = _first_max_row(score, blk)
        sel = jnp.where(pick, jnp.where(past, 1.0, sel), sel)
        score = jnp.where(pick, -jnp.inf, score)
    bias = jnp.where(sel > 0.0, 0.0, jnp.where(blk < nb, NEG_INF, 0.0))
    n_fut = LANES - FUTURE_COL0
    fut_col = FUTURE_COL0 + lax.broadcasted_iota(jnp.int32, (n_fut, tq), 0)
    fut_tile = (i * tq + lax.broadcasted_iota(jnp.int32, (n_fut, tq), 1)) >> (LANES.bit_length() - 1)
    future, _ = _future_tile_columns(fut_col, fut_tile)
    bias = jnp.concatenate([bias, jnp.zeros((FUTURE_COL0 - nbp, tq), F32), future], axis=0)
    o_ref[0] = bias.T.astype(BF16)


def _moba_select_call(q, kmean_pad, *, nb, tq):
    g, _, s, _ = q.shape
    top = min(MOBA_TOPK, max(nb - 1, 1))
    return pl.pallas_call(
        functools.partial(_moba_select_kernel, nb=nb, top=top),
        grid=(g, s // tq),
        in_specs=[pl.BlockSpec((1, 1, tq, LANES), lambda gg, i: (gg, 0, i, 0)),
                  pl.BlockSpec((1, LANES, LANES), lambda gg, i: (gg, 0, 0))],
        out_specs=pl.BlockSpec((1, tq, LANES), lambda gg, i: (gg, i, 0)),
        out_shape=jax.ShapeDtypeStruct((g, s, LANES), BF16),
        compiler_params=_params(2),
        name="moba_select",
    )(q, kmean_pad)


def _attn_kernel(q_ref, qa_ref, k_ref, ka_ref, v_ref, o_ref,
                 kc_ref, vc_ref, s_ref, m_ref, acc_ref, *, hs, tq, tk):
    rows = hs * tq
    seq = k_ref.shape[1]
    kc_ref[:, :LANES] = k_ref[0]
    kc_ref[:, LANES:] = ka_ref[0]
    vc_ref[:, :LANES] = v_ref[0]
    vc_ref[:, LANES:] = jnp.ones((seq, LANES), BF16)
    key_minus_query = (lax.broadcasted_iota(jnp.int32, (LANES, LANES), 1)
                       - lax.broadcasted_iota(jnp.int32, (LANES, LANES), 0))

    def logits(i, j, buf):
        qs = pl.multiple_of(i * tq, tq)
        qc = jnp.concatenate([q_ref[0, :, pl.ds(qs, tq), :].reshape(rows, LANES),
                              jnp.concatenate([qa_ref[0, pl.ds(qs, tq), :]] * hs, axis=0)], axis=-1)
        s_ref[buf] = _dot_nt(qc, kc_ref[pl.ds(pl.multiple_of(j * tk, tk), tk), :])

    def causal(i, j, s):
        delta = i * tq - j * tk
        per_tile = tq // LANES
        row_groups = []
        for rg in range(rows // LANES):
            rq = rg % per_tile
            blocks = []
            for kt in range(tk // LANES):
                blk = s[rg * LANES:(rg + 1) * LANES, kt * LANES:(kt + 1) * LANES]
                if kt >= rq and (kt - rq) % per_tile == 0:
                    blk = jnp.where(key_minus_query <= delta + (rq - kt) * LANES, blk, NEG_INF)
                blocks.append(blk)
            row_groups.append(jnp.concatenate(blocks, axis=1))
        return jnp.concatenate(row_groups, axis=0)

    def update(i, j, buf):
        s = causal(i, j, s_ref[buf])
        m_prev = jnp.where(j == 0, -jnp.inf, m_ref[...])
        m_new = jnp.maximum(m_prev, jnp.max(s, axis=-1, keepdims=True))
        alpha = jnp.exp2(m_prev - m_new)
        p = jnp.exp2(s - jnp.tile(m_new, (1, tk // LANES))).astype(BF16)
        m_ref[...] = m_new
        acc = (jnp.tile(alpha, (1, 2)) * acc_ref[...]
               + _dot(p, vc_ref[pl.ds(pl.multiple_of(j * tk, tk), tk), :]))
        acc_ref[...] = acc
        out = (acc[:, :LANES] / acc[:, LANES:]).astype(o_ref.dtype)
        qs = pl.multiple_of(i * tq, tq)
        for hh in range(hs):
            o_ref[0, hh, pl.ds(qs, tq), :] = out[hh * tq:(hh + 1) * tq]

    def advance(i, j):
        wrap = j == (i * tq) // tk
        return jnp.where(wrap, i + 1, i), jnp.where(wrap, 0, j + 1)

    n_pairs = sum((i * tq) // tk + 1 for i in range(seq // tq))
    zero = jnp.int32(0)
    acc_ref[...] = jnp.zeros_like(acc_ref)
    m_ref[...] = jnp.zeros_like(m_ref)
    logits(zero, zero, 0)

    def trip(carry, cur, nxt):
        ib, jb, ia, ja = carry
        logits(ia, ja, nxt)
        update(ib, jb, cur)
        return (ia, ja) + advance(ia, ja)

    def body(_, carry):
        return trip(trip(carry, 0, 1), 1, 0)

    carry = lax.fori_loop(0, (n_pairs - 1) // 2, body, (zero, zero) + advance(zero, zero))
    last = 0
    if (n_pairs - 1) % 2:
        carry = trip(carry, 0, 1)
        last = 1
    update(carry[0], carry[1], last)


def _attn_call(q, qa, k, ka, v, *, tq, tk, out_dtype, kv_groups=1, ka_shared=False):
    g, hs, s, _ = q.shape
    assert tq & (tq - 1) == 0 and tk % tq == 0 and s % tk == 0
    rows = hs * tq
    seq = lambda gg: (gg // kv_groups, 0, 0)
    ka_map = (lambda gg: (0, 0, 0)) if ka_shared else seq
    return pl.pallas_call(
        functools.partial(_attn_kernel, hs=hs, tq=tq, tk=tk),
        grid=(g,),
        in_specs=[pl.BlockSpec((1, hs, s, LANES), lambda gg: (gg, 0, 0, 0)),
                  pl.BlockSpec((1, s, LANES), seq),
                  pl.BlockSpec((1, s, LANES), seq),
                  pl.BlockSpec((1, s, LANES), ka_map),
                  pl.BlockSpec((1, s, LANES), seq)],
        out_specs=pl.BlockSpec((1, hs, s, LANES), lambda gg: (gg, 0, 0, 0)),
        out_shape=jax.ShapeDtypeStruct((g, hs, s, LANES), out_dtype),
        scratch_shapes=[pltpu.VMEM((s, 2 * LANES), BF16),
                        pltpu.VMEM((s, 2 * LANES), BF16),
                        pltpu.VMEM((2, rows, tk), F32),
                        pltpu.VMEM((rows, LANES), F32),
                        pltpu.VMEM((rows, 2 * LANES), F32)],
        compiler_params=_params(1),
        name="flash_attn",
    )(q, qa, k, ka, v)


def _compress_kernel(x_ref, pe_ref, w1_ref, w2_ref, o_ref):
    x = x_ref[0, 0]
    half = x.shape[1]
    pe = pe_ref[0, 0]
    top = _dot((x + pe[:, :half]).astype(BF16), w1_ref[0, 0, :half, :])
    bot = _dot((x + pe[:, half:]).astype(BF16), w1_ref[0, 0, half:, :])
    n = x.shape[0]
    hid = top + pltpu.roll(bot, n - 1, 0)
    act = 0.5 * hid * (1.0 + jnp.tanh(0.7978845608028654 * (hid + 0.044715 * hid * hid * hid)))
    o_ref[0, 0] = _dot(act.astype(BF16), w2_ref[0, 0]).astype(o_ref.dtype)


def _compress_call(cv, pe, w1, w2, *, l):
    b, _, s, _ = cv.shape
    n = s // NSA_CMP_STRIDE
    width = NSA_CMP_STRIDE * LANES
    x2 = cv.reshape(b, 2, n, width)
    return pl.pallas_call(
        _compress_kernel,
        grid=(b, 2),
        in_specs=[pl.BlockSpec((1, 1, n, width), lambda bb, t: (bb, t, 0, 0)),
                  pl.BlockSpec((1, 1, 1, 2 * width), lambda bb, t: (l, t, 0, 0)),
                  pl.BlockSpec((1, 1, 2 * width, NSA_CMP_HIDDEN), lambda bb, t: (l, t, 0, 0)),
                  pl.BlockSpec((1, 1, NSA_CMP_HIDDEN, LANES), lambda bb, t: (l, t, 0, 0))],
        out_specs=pl.BlockSpec((1, 1, n, LANES), lambda bb, t: (bb, t, 0, 0)),
        out_shape=jax.ShapeDtypeStruct((b, 2, n, LANES), BF16),
        compiler_params=_params(2),
        name="nsa_compress",
    )(x2, pe, w1, w2)


def _nsa_local_kernel(q_ref, kc_ref, vc_ref, ovt_ref, kw_ref, vw_ref, small_ref,
                      ocw_ref, bias_ref, *, hn, tq, n_sb):
    i = pl.program_id(1)
    rows = hn * tq
    q = q_ref[0].reshape(rows, LANES)
    gates = _sigmoid(small_ref[0])
    nc = kc_ref.shape[2]
    s = _dot_nt(q, kc_ref[0, 0])
    r = lax.broadcasted_iota(jnp.int32, (rows, nc), 0)
    tok = i * tq + (r & (tq - 1))
    cend = lax.broadcasted_iota(jnp.int32, (rows, nc), 1) * NSA_CMP_STRIDE + (NSA_CMP_LEN - 1)
    s = jnp.where(cend <= tok, s, NEG_INF)
    e = jnp.exp2(s - jnp.max(s, axis=-1, keepdims=True))
    tok1 = i * tq + (lax.broadcasted_iota(jnp.int32, (rows, 1), 0) & (tq - 1))
    rinv = jnp.where(tok1 >= NSA_CMP_LEN - 1, 1.0 / jnp.sum(e, axis=-1, keepdims=True), 0.0)
    o_cmp = (_dot(e.astype(BF16), vc_ref[0, 0]) * rinv).reshape(hn, tq, LANES)
    psum = jnp.sum((e * rinv).reshape(hn, tq, nc), axis=0)
    hi = psum.astype(BF16)
    lo = (psum - hi.astype(F32)).astype(BF16)
    imp = _dot_nt(ovt_ref[...], hi) + _dot_nt(ovt_ref[...], lo)
    blk = lax.broadcasted_iota(jnp.int32, (LANES, tq), 0).astype(F32)
    cur = ((i * tq + lax.broadcasted_iota(jnp.int32, (LANES, tq), 1))
           >> (NSA_SEL_BLOCK.bit_length() - 1)).astype(F32)
    score = jnp.where(blk >= 1.0, jnp.where(blk <= cur - 2.0, imp, -jnp.inf), -jnp.inf)
    sel = jnp.where(blk == 0.0, 1.0, jnp.where(blk == cur, 1.0, jnp.where(blk == cur - 1.0, 1.0, 0.0)))
    for _ in range(min(NSA_N_SEL, n_sb) - 3):
        pick, m = _first_max_row(score, blk)
        sel = jnp.where(pick, jnp.where(m > -jnp.inf, 1.0, sel), sel)
        score = jnp.where(pick, -jnp.inf, score)
    bias_ref[0] = jnp.where(sel > 0.0, 0.0, NEG_INF).T.astype(BF16)
    span = NSA_WINDOW + tq
    start = pl.multiple_of(jnp.maximum(i * tq - NSA_WINDOW, 0), tq)
    sw = _dot_nt(q, kw_ref[0, pl.ds(start, span), :])
    rw = lax.broadcasted_iota(jnp.int32, (rows, span), 0)
    back = (i * tq - start) + (rw & (tq - 1)) - lax.broadcasted_iota(jnp.int32, (rows, span), 1)
    sw = jnp.where(lax.bitcast_convert_type(back, jnp.uint32) < NSA_WINDOW, sw, NEG_INF)
    ew = jnp.exp2(sw - jnp.max(sw, axis=-1, keepdims=True)).astype(BF16)
    vw1 = jnp.concatenate([vw_ref[0, pl.ds(start, span), :], jnp.ones((span, LANES), BF16)], axis=-1)
    ow = _dot(ew, vw1)
    o_win = (ow[:, :LANES] / ow[:, LANES:]).reshape(hn, tq, LANES)
    for hh in range(hn):
        ocw_ref[0, hh] = (_lane_col(gates, hh * NSA_N_BRANCH) * o_cmp[hh]
                          + _lane_col(gates, hh * NSA_N_BRANCH + 2) * o_win[hh])


def _nsa_local_call(q, kvc, overlap, kw, vw, small, *, tq):
    b, hn, s, _ = q.shape
    nc = kvc.shape[2]
    n_sb = s // NSA_SEL_BLOCK
    assert n_sb <= LANES and tq & (tq - 1) == 0 and NSA_WINDOW % tq == 0
    seq = lambda bb, i: (bb, 0, 0)
    return pl.pallas_call(
        functools.partial(_nsa_local_kernel, hn=hn, tq=tq, n_sb=n_sb),
        grid=(b, s // tq),
        in_specs=[pl.BlockSpec((1, hn, tq, LANES), lambda bb, i: (bb, 0, i, 0)),
                  pl.BlockSpec((1, 1, nc, LANES), lambda bb, i: (bb, 0, 0, 0)),
                  pl.BlockSpec((1, 1, nc, LANES), lambda bb, i: (bb, 1, 0, 0)),
                  pl.BlockSpec((LANES, nc), lambda bb, i: (0, 0)),
                  pl.BlockSpec((1, s, LANES), seq),
                  pl.BlockSpec((1, s, LANES), seq),
                  pl.BlockSpec((1, tq, LANES), lambda bb, i: (bb, i, 0))],
        out_specs=[pl.BlockSpec((1, hn, tq, LANES), lambda bb, i: (bb, 0, i, 0)),
                   pl.BlockSpec((1, tq, LANES), lambda bb, i: (bb, i, 0))],
        out_shape=[jax.ShapeDtypeStruct((b, hn, s, LANES), F32),
                   jax.ShapeDtypeStruct((b, s, LANES), BF16)],
        compiler_params=_params(2),
        name="nsa_local",
    )(q, kvc, kvc, overlap, kw, vw, small)


def _outproj_kernel(om_ref, ocw_ref, osl_ref, small_ref, of_ref, w_ref, x_ref,
                    gpost_ref, gate_ref, gpre_ref, sh_ref, sc_ref, xo_ref, h_ref,
                    *, hm, hn, hf):
    gates = _sigmoid(small_ref[0])
    parts = [om_ref[0, hh] for hh in range(hm)]
    for hh in range(hn):
        o = ocw_ref[0, hh] + _lane_col(gates, hh * NSA_N_BRANCH + 1) * osl_ref[0, hh]
        parts.append(o.astype(BF16))
    parts += [of_ref[0, hh] for hh in range(hf)]
    y = _dot(jnp.concatenate(parts, axis=-1), w_ref[0])
    xn = x_ref[0] + gate_ref[0, 0, 0] * _rms(y, gpost_ref[0])
    xo_ref[0] = xn
    h_ref[0] = (_rms(xn, gpre_ref[0]) * (1.0 + sc_ref[0, 0, 0]) + sh_ref[0, 0, 0]).astype(BF16)


def _outproj_call(o_moba, o_cw, o_slc, small, o_fox, w_out, x, g_post, g_pre, mod6, *, l, tm):
    b, s, d = x.shape
    hm, hn, hf = o_moba.shape[1], o_cw.shape[1], o_fox.shape[1]
    head_spec = lambda n: pl.BlockSpec((1, n, tm, LANES), lambda bb, i: (bb, 0, i, 0))
    row = lambda bb, i: (bb, i, 0)
    return pl.pallas_call(
        functools.partial(_outproj_kernel, hm=hm, hn=hn, hf=hf),
        grid=(b, s // tm),
        in_specs=[head_spec(hm), head_spec(hn), head_spec(hn),
                  pl.BlockSpec((1, tm, LANES), row), head_spec(hf),
                  _resident((1,) + w_out.shape[1:], lambda bb, i: (l, 0, 0)),
                  pl.BlockSpec((1, tm, d), row), _layer_vec(l, d), _mod_vec(l, 2, d),
                  _layer_vec(l, d), _mod_vec(l, 3, d), _mod_vec(l, 4, d)],
        out_specs=[pl.BlockSpec((1, tm, d), row), pl.BlockSpec((1, tm, d), row)],
        out_shape=[jax.ShapeDtypeStruct((b, s, d), F32), jax.ShapeDtypeStruct((b, s, d), BF16)],
        compiler_params=_params(2),
        name="outproj",
    )(o_moba, o_cw, o_slc, small, o_fox, w_out, x, g_post, mod6, g_pre, mod6, mod6)


def _mlp_kernel(h_ref, wu_ref, wd_ref, x_ref, g_ref, gate_ref, o_ref, acc_ref):
    f = pl.program_id(2)

    @pl.when(f == 0)
    def _():
        acc_ref[...] = jnp.zeros_like(acc_ref)

    u = jnp.maximum(_dot(h_ref[0], wu_ref[0]), 0.0)
    acc_ref[...] += _dot((u * u).astype(BF16), wd_ref[0])

    @pl.when(f == pl.num_programs(2) - 1)
    def _():
        o_ref[0] = x_ref[0] + gate_ref[0, 0, 0] * _rms(acc_ref[...], g_ref[0])


def _mlp_call(h, w_up, w_down, x, g_post, mod6, *, l, tm, tf):
    b, s, d = x.shape
    dff = w_up.shape[2]
    row = lambda bb, i, f: (bb, i, 0)
    return pl.pallas_call(
        _mlp_kernel,
        grid=(b, s // tm, dff // tf),
        in_specs=[pl.BlockSpec((1, tm, d), row),
                  pl.BlockSpec((1, d, tf), lambda bb, i, f: (l, 0, f)),
                  pl.BlockSpec((1, tf, d), lambda bb, i, f: (l, f, 0)),
                  pl.BlockSpec((1, tm, d), row),
                  _layer_vec(l, d), _mod_vec(l, 5, d)],
        out_specs=pl.BlockSpec((1, tm, d), row),
        out_shape=jax.ShapeDtypeStruct((b, s, d), F32),
        scratch_shapes=[pltpu.VMEM((tm, d), F32)],
        compiler_params=_params(3),
        name="mlp",
    )(h, w_up, w_down, x, g_post, mod6)


def _tile(n, pref):
    t = min(n, pref)
    assert n % t == 0
    return t


def _rope_tables(positions):
    inv_freq = ROPE_THETA ** (-jnp.arange(ROPE_HALF, dtype=F32) / ROPE_HALF)
    ang = positions.astype(F32)[..., None] * inv_freq
    cos, sin = jnp.cos(ang), jnp.sin(ang)
    ones = jnp.ones(ang.shape[:-1] + (LANES - ROPE_DIMS,), F32)
    cos_t = jnp.concatenate([cos, cos, ones], axis=-1)
    sin_t = jnp.concatenate([-sin, sin, 0.0 * ones], axis=-1)
    return cos_t, sin_t


def _block_onehot(s, block, future_tiles=False):
    cols = np.arange(LANES)[None, :]
    onehot = (np.arange(s) // block)[:, None] == cols
    if future_tiles:
        assert s // block <= FUTURE_COL0
        onehot = onehot | ((np.arange(s) // LANES)[:, None] == cols - FUTURE_COL0)
    return jnp.asarray(onehot.astype(np.float32), dtype=BF16)[None]


def _overlap_matrix(s):
    nc = s // NSA_CMP_STRIDE
    n_cmp = (s - NSA_CMP_LEN) // NSA_CMP_STRIDE + 1
    c = np.arange(nc)
    first = (c * NSA_CMP_STRIDE) // NSA_SEL_BLOCK
    last = (c * NSA_CMP_STRIDE + NSA_CMP_LEN - 1) // NSA_SEL_BLOCK
    sb = np.arange(LANES)
    ov = (sb[None, :] >= first[:, None]) & (sb[None, :] <= last[:, None]) & (c[:, None] < n_cmp)
    return jnp.asarray(ov.T.astype(np.float32), dtype=BF16)


def kernel(x, c, positions, w_mod, b_mod, g_pre_mix, g_post_mix, g_pre_mlp, g_post_mlp, w_in,
           b_forget, cmp_pe_k, cmp_pe_v, cmp_w1_k, cmp_w2_k, cmp_w1_v, cmp_w2_v, w_out, w_up, w_down):
    b, s, d = x.shape
    depth = w_mod.shape[0]
    n_heads = d // HEAD_DIM
    hm = n_heads // 4
    hn = n_heads // 4
    hf = n_heads - hm - hn
    mw, nw, fw = hm * HEAD_DIM, hn * HEAD_DIM, hf * HEAD_DIM
    n_gate = hn * NSA_N_BRANCH
    o_g = 3 * mw + nw + NSA_N_KV * HEAD_DIM
    o_f = o_g + n_gate
    o_ff = o_f + 3 * fw
    assert n_gate + hf <= LANES and s % MOBA_BLOCK == 0 and s // LANES <= LANES - FUTURE_COL0

    c_pad = jnp.zeros((8, d), F32).at[:b].set(c)
    mod = _mod_call(c_pad, w_mod, b_mod)
    mod6 = mod[:, :b].reshape(depth, b, 6, 1, d).transpose(0, 2, 1, 3, 4)
    cos_t, sin_t = _rope_tables(positions)
    moba_ka = _block_onehot(s, MOBA_BLOCK, future_tiles=True)
    slc_ka = _block_onehot(s, NSA_SEL_BLOCK)
    overlap = _overlap_matrix(s)
    nb = s // MOBA_BLOCK

    w_cat = _wcat_call(w_in, o_g=o_g, o_f=o_f, o_ff=o_ff, tr=_tile(d, 256))
    w_out_b, w_up_b, w_down_b = w_out.astype(BF16), w_up.astype(BF16), w_down.astype(BF16)
    cmp_pe = jnp.stack([cmp_pe_k, cmp_pe_v], axis=1).reshape(depth, 2, 1, NSA_CMP_LEN * HEAD_DIM)
    cmp_w1 = jnp.stack([cmp_w1_k, cmp_w1_v], axis=1).astype(BF16)
    cmp_w2 = jnp.stack([cmp_w2_k, cmp_w2_v], axis=1).astype(BF16)
    bf_rows = jnp.zeros((depth, 1, LANES), F32).at[:, 0, n_gate:n_gate + hf].set(b_forget)
    gains = [g.reshape(depth, 1, d) for g in (g_pre_mix, g_post_mix, g_pre_mlp, g_post_mlp)]

    tm_in = _tile(s, 512)
    for l in range(depth):
        (mq, mk, mv, kmean, nq, cv, ks, vs, kw, vw, small, fq, fk, fv) = _inproj_call(
            x, gains[0], mod6, cos_t, sin_t, w_cat, l=l, hm=hm, hn=hn, hf=hf, tm=tm_in)

        fqa, fka = _fox_prep_call(small, bf_rows, l=l, hf=hf, col0=n_gate, ts=_tile(s, 512))
        fold = lambda t: t.reshape(b * t.shape[1], 1, s, LANES)
        flat = lambda t: t.reshape(b * t.shape[1], s, LANES)
        o_fox = _attn_call(fold(fq), flat(fqa), flat(fk), flat(fka), flat(fv),
                           tq=_tile(s, 512), tk=_tile(s, 1024), out_dtype=BF16).reshape(b, hf, s, LANES)

        kmean = kmean.reshape(b, s // tm_in, hm, tm_in // MOBA_BLOCK, LANES)
        kmean = kmean.transpose(0, 2, 1, 3, 4).reshape(b * hm, nb, LANES)
        kmean = jnp.pad(kmean, ((0, 0), (0, LANES - nb), (0, 0)))
        mqa = _moba_select_call(fold(mq), kmean, nb=nb, tq=_tile(s, 512))
        o_moba = _attn_call(fold(mq), mqa, flat(mk), moba_ka, flat(mv), tq=_tile(s, 512),
                            tk=_tile(s, 1024), out_dtype=BF16, ka_shared=True).reshape(b, hm, s, LANES)

        kvc = _compress_call(cv, cmp_pe, cmp_w1, cmp_w2, l=l)
        o_cw, sbias = _nsa_local_call(nq, kvc, overlap, kw, vw, small, tq=_tile(s, 256))
        slc_split = 2 if hn % 2 == 0 else 1
        o_slc = _attn_call(nq.reshape(b * slc_split, hn // slc_split, s, LANES), sbias, ks, slc_ka, vs,
                           tq=_tile(s, 256), tk=_tile(s, 1024), out_dtype=BF16,
                           kv_groups=slc_split, ka_shared=True).reshape(b, hn, s, LANES)

        x, h_mlp = _outproj_call(o_moba, o_cw, o_slc, small, o_fox, w_out_b, x,
                                 gains[1], gains[2], mod6, l=l, tm=_tile(s, 512))
        x = _mlp_call(h_mlp, w_up_b, w_down_b, x, gains[3], mod6, l=l,
                      tm=_tile(s, 512), tf=1024)
    return x
```

```python
import functools

import jax
import jax.numpy as jnp
import numpy as np
from jax import lax
from jax.experimental import pallas as pl
from jax.experimental.pallas import tpu as pltpu

HEAD_DIM = 128
ROPE_THETA = 500000.0
ROPE_DIMS = HEAD_DIM // 4
ROPE_HALF = ROPE_DIMS // 2
MOBA_BLOCK = 256
MOBA_TOPK = 3
NSA_CMP_LEN = 32
NSA_CMP_STRIDE = 16
NSA_CMP_HIDDEN = 256
NSA_SEL_BLOCK = 64
NSA_N_SEL = 16
NSA_WINDOW = 512
NSA_N_BRANCH = 3
NSA_N_KV = 6
NORM_EPS = 1e-6
NEG_INF = -1e30
ATTN_SCALE = HEAD_DIM ** -0.5
LOG2E = 1.4426950408889634
Q_SCALE = ATTN_SCALE * LOG2E

LANES = 128
VMEM_LIMIT_BYTES = 56 * 1024 * 1024

F32 = jnp.float32
BF16 = jnp.bfloat16


def _params(n_axes):
    return pltpu.CompilerParams(dimension_semantics=("arbitrary",) * n_axes,
                                vmem_limit_bytes=VMEM_LIMIT_BYTES)


def _resident(block_shape, index_map):
    return pl.BlockSpec(block_shape, index_map, pipeline_mode=pl.Buffered(1))


def _split3(x):
    p1 = x.astype(BF16)
    r1 = x - p1.astype(F32)
    p2 = r1.astype(BF16)
    p3 = (r1 - p2.astype(F32)).astype(BF16)
    return p1, p2, p3


def _dot(a, b):
    return jnp.dot(a, b, preferred_element_type=F32)


def _dot_nt(a, b):
    return lax.dot_general(a, b, (((1,), (1,)), ((), ())), preferred_element_type=F32)


def _rms(x, g):
    return x * lax.rsqrt(jnp.mean(x * x, axis=-1, keepdims=True) + NORM_EPS) * g


def _sigmoid(z):
    return 1.0 / (1.0 + jnp.exp(-z))


FUTURE_COL0 = LANES // 2


def _future_tile_columns(col, tile):
    c = col - FUTURE_COL0
    qa = jnp.where(c > tile, NEG_INF, 0.0)
    ka = jnp.where(c == tile, 1.0, 0.0)
    return qa, ka


def _lane_col(x, c):
    return jnp.broadcast_to(x[:, c:c + 1], x.shape)


def _mod_kernel(c_ref, w_ref, b_ref, o_ref):
    c = c_ref[...]
    ca = c * _sigmoid(c)
    o_ref[0] = jnp.dot(ca, w_ref[0], precision=lax.Precision.HIGHEST,
                       preferred_element_type=F32) + b_ref[0]


def _mod_call(c_pad, w_mod, b_mod):
    depth, d, n = w_mod.shape
    rows = c_pad.shape[0]
    tn = 1024
    assert n % tn == 0
    return pl.pallas_call(
        _mod_kernel,
        grid=(depth, n // tn),
        in_specs=[pl.BlockSpec((rows, d), lambda l, j: (0, 0)),
                  pl.BlockSpec((1, d, tn), lambda l, j: (l, 0, j)),
                  pl.BlockSpec((1, 1, tn), lambda l, j: (l, 0, j))],
        out_specs=pl.BlockSpec((1, rows, tn), lambda l, j: (l, 0, j)),
        out_shape=jax.ShapeDtypeStruct((depth, rows, n), F32),
        compiler_params=_params(2),
        name="adaln_mod",
    )(c_pad, w_mod, b_mod.reshape(depth, 1, n))


def _wcat_kernel(w_ref, o_ref, *, o_g, o_f, o_ff):
    n_in = w_ref.shape[1]
    n_fox = o_ff - o_f
    n_small = (o_f - o_g) + (n_in - o_ff)
    tc = o_ref.shape[2]
    o_ref[0, :o_g, :] = w_ref[0, :o_g, :].astype(BF16)
    o_ref[0, o_g:o_g + n_fox, :] = w_ref[0, o_f:o_ff, :].astype(BF16)
    small = jnp.concatenate([w_ref[0, o_g:o_f, :], w_ref[0, o_ff:, :],
                             jnp.zeros((LANES - n_small, tc), F32)], axis=0)
    o_ref[0, o_g + n_fox:, :] = small.astype(BF16)


def _wcat_call(w_in_t, *, o_g, o_f, o_ff, tc):
    depth, n_in, d = w_in_t.shape
    npad = o_g + (o_ff - o_f) + LANES
    assert o_g % LANES == 0 and (o_ff - o_f) % LANES == 0
    return pl.pallas_call(
        functools.partial(_wcat_kernel, o_g=o_g, o_f=o_f, o_ff=o_ff),
        grid=(depth, d // tc),
        in_specs=[pl.BlockSpec((1, n_in, tc), lambda l, i: (l, 0, i))],
        out_specs=pl.BlockSpec((1, npad, tc), lambda l, i: (l, 0, i)),
        out_shape=jax.ShapeDtypeStruct((depth, npad, d), BF16),
        compiler_params=_params(2),
        name="w_in_layout",
    )(w_in_t)


def _inproj_kernel(x_ref, g_ref, sh_ref, sc_ref, cos_ref, sin_ref, w_ref,
                   mq_ref, mk_ref, mv_ref, kmean_ref, nq_ref, cv_ref,
                   ks_ref, vs_ref, kw_ref, vw_ref, small_ref, fq_ref, fk_ref, fv_ref,
                   *, hm, hn, hf):
    x = x_ref[0]
    h = _rms(x, g_ref[0]) * (1.0 + sc_ref[0, 0, 0]) + sh_ref[0, 0, 0]
    hb = h.astype(BF16)
    tm = x.shape[0]
    cos = cos_ref[0]
    sin = sin_ref[0]
    lane = lax.broadcasted_iota(jnp.int32, (tm, LANES), 1)

    def rope(y):
        swapped = jnp.where(lane < ROPE_HALF, pltpu.roll(y, LANES - ROPE_HALF, 1),
                            pltpu.roll(y, ROPE_HALF, 1))
        return y * cos + swapped * sin

    def heads(col0, n):
        outs = []
        j = 0
        while j < n:
            w = 2 if j + 1 < n else 1
            y = _dot_nt(hb, w_ref[0, (col0 + j) * LANES:(col0 + j + w) * LANES, :])
            for t in range(w):
                outs.append(y[:, t * LANES:(t + 1) * LANES])
            j += w
        return outs

    col = 0
    for hh, y in enumerate(heads(col, hm)):
        mq_ref[0, hh] = (rope(y) * Q_SCALE).astype(BF16)
    col += hm
    nblk = tm // MOBA_BLOCK
    means = []
    for hh, y in enumerate(heads(col, hm)):
        yr = rope(y)
        mk_ref[0, hh] = yr.astype(BF16)
        means.append(jnp.mean(yr.reshape(nblk, MOBA_BLOCK, LANES), axis=1))
    kmean_ref[0, 0] = jnp.concatenate(means, axis=0)
    col += hm
    for hh, y in enumerate(heads(col, hm)):
        mv_ref[0, hh] = y.astype(BF16)
    col += hm
    for hh, y in enumerate(heads(col, hn)):
        nq_ref[0, hh] = (rope(y) * Q_SCALE).astype(BF16)
    col += hn
    kc, vc, ks, vs, kw, vw = heads(col, NSA_N_KV)
    cv_ref[0, 0] = rope(kc)
    cv_ref[0, 1] = vc
    ks_ref[0] = rope(ks).astype(BF16)
    vs_ref[0] = vs.astype(BF16)
    kw_ref[0] = rope(kw).astype(BF16)
    vw_ref[0] = vw.astype(BF16)
    col += NSA_N_KV
    for hh, y in enumerate(heads(col, hf)):
        fq_ref[0, hh] = (y * Q_SCALE).astype(BF16)
    col += hf
    for hh, y in enumerate(heads(col, hf)):
        fk_ref[0, hh] = y.astype(BF16)
    col += hf
    for hh, y in enumerate(heads(col, hf)):
        fv_ref[0, hh] = y.astype(BF16)
    col += hf
    small_ref[0] = heads(col, 1)[0]


def _layer_vec(l, d):
    return pl.BlockSpec((1, 1, d), lambda *_: (l, 0, 0))


def _mod_vec(l, k, d):
    return pl.BlockSpec((1, 1, 1, 1, d), lambda bb, *_: (l, k, bb, 0, 0))


def _inproj_call(x, g, mod6, cos_t, sin_t, w_cat, *, l, hm, hn, hf, tm):
    b, s, d = x.shape
    npad = w_cat.shape[1]
    nblk = tm // MOBA_BLOCK
    row = lambda bb, i: (bb, i, 0)
    head_spec = lambda n: pl.BlockSpec((1, n, tm, LANES), lambda bb, i: (bb, 0, i, 0))
    head_shape = lambda n, dt=BF16: jax.ShapeDtypeStruct((b, n, s, LANES), dt)
    tok_spec = pl.BlockSpec((1, tm, LANES), row)
    tok_shape = lambda dt=BF16: jax.ShapeDtypeStruct((b, s, LANES), dt)
    out_specs = [head_spec(hm), head_spec(hm), head_spec(hm),
                 pl.BlockSpec((1, 1, hm * nblk, LANES), lambda bb, i: (bb, i, 0, 0)),
                 head_spec(hn), head_spec(2),
                 tok_spec, tok_spec, tok_spec, tok_spec, tok_spec,
                 head_spec(hf), head_spec(hf), head_spec(hf)]
    out_shape = [head_shape(hm), head_shape(hm), head_shape(hm),
                 jax.ShapeDtypeStruct((b, s // tm, hm * nblk, LANES), F32),
                 head_shape(hn), head_shape(2, F32),
                 tok_shape(), tok_shape(), tok_shape(), tok_shape(), tok_shape(F32),
                 head_shape(hf), head_shape(hf), head_shape(hf)]
    return pl.pallas_call(
        functools.partial(_inproj_kernel, hm=hm, hn=hn, hf=hf),
        grid=(b, s // tm),
        in_specs=[pl.BlockSpec((1, tm, d), row),
                  _layer_vec(l, d), _mod_vec(l, 0, d), _mod_vec(l, 1, d),
                  pl.BlockSpec((1, tm, LANES), row),
                  pl.BlockSpec((1, tm, LANES), row),
                  _resident((1, npad, d), lambda bb, i: (l, 0, 0))],
        out_specs=out_specs,
        out_shape=out_shape,
        compiler_params=_params(2),
        name="inproj",
    )(x, g, mod6, mod6, cos_t, sin_t, w_cat)


def _fox_prep_kernel(small_ref, bf_ref, qa_ref, ka_ref, carry_ref, *, hf, col0):
    i = pl.program_id(1)

    @pl.when(i == 0)
    def _():
        carry_ref[...] = jnp.zeros_like(carry_ref)

    z = small_ref[0] + bf_ref[0]
    logf = jnp.minimum(z, 0.0) - jnp.log1p(jnp.exp(-jnp.abs(z)))
    ts = z.shape[0]
    r = lax.broadcasted_iota(jnp.int32, (ts, ts), 0)
    c = lax.broadcasted_iota(jnp.int32, (ts, ts), 1)
    tri = jnp.where(r >= c, 1.0, 0.0).astype(BF16)
    p1, p2, p3 = _split3(logf)
    cum = _dot(tri, p1) + _dot(tri, p2) + _dot(tri, p3) + carry_ref[...]
    carry_ref[...] = cum[ts - 1:ts, :]
    lane = lax.broadcasted_iota(jnp.int32, (ts, LANES), 1)
    tile = (i * ts + lax.broadcasted_iota(jnp.int32, (ts, LANES), 0)) >> (LANES.bit_length() - 1)
    qa_future, ka_future = _future_tile_columns(lane, tile)
    for hh in range(hf):
        cb = _lane_col(cum, col0 + hh) * LOG2E
        c1, c2, c3 = (p.astype(F32) for p in _split3(cb))
        qa = jnp.where(lane == 0, c1, jnp.where(lane == 1, c2, jnp.where(
            lane == 2, c3, jnp.where(lane < 6, 1.0, qa_future))))
        ka = jnp.where(lane < 3, 1.0, jnp.where(lane == 3, -c1, jnp.where(
            lane == 4, -c2, jnp.where(lane == 5, -c3, ka_future))))
        qa_ref[0, hh] = qa.astype(BF16)
        ka_ref[0, hh] = ka.astype(BF16)


def _fox_prep_call(small, bf_rows, *, l, hf, col0, ts):
    b, s, _ = small.shape
    spec = pl.BlockSpec((1, hf, ts, LANES), lambda bb, i: (bb, 0, i, 0))
    shape = jax.ShapeDtypeStruct((b, hf, s, LANES), BF16)
    return pl.pallas_call(
        functools.partial(_fox_prep_kernel, hf=hf, col0=col0),
        grid=(b, s // ts),
        in_specs=[pl.BlockSpec((1, ts, LANES), lambda bb, i: (bb, i, 0)),
                  _layer_vec(l, LANES)],
        out_specs=[spec, spec],
        out_shape=[shape, shape],
        scratch_shapes=[pltpu.VMEM((1, LANES), F32)],
        compiler_params=_params(2),
        name="fox_prep",
    )(small, bf_rows)


def _first_max_row(score, blk):
    m = jnp.max(score, axis=0, keepdims=True)
    idx = jnp.min(jnp.where(score == m, blk, float(score.shape[0])), axis=0, keepdims=True)
    return blk == idx, m


def _moba_select_kernel(q_ref, km_ref, o_ref, *, nb, top):
    i = pl.program_id(1)
    q = q_ref[0, 0]
    tq = q.shape[0]
    nbp = -(-nb // 8) * 8
    k1, k2, k3 = _split3(km_ref[0, :nbp, :])
    gate = _dot_nt(k1, q) + _dot_nt(k2, q) + _dot_nt(k3, q)
    blk = lax.broadcasted_iota(jnp.int32, (nbp, tq), 0).astype(F32)
    tok = i * tq + lax.broadcasted_iota(jnp.int32, (nbp, tq), 1)
    own = (tok >> (MOBA_BLOCK.bit_length() - 1)).astype(F32)
    past = blk < own
    score = jnp.where(blk < nb, jnp.where(past, gate, NEG_INF), -jnp.inf)
    sel = jnp.where(blk == own, 1.0, 0.0)
    for _ in range(top):
        pick, _m = _first_max_row(score, blk)
        sel = jnp.where(pick, jnp.where(past, 1.0, sel), sel)
        score = jnp.where(pick, -jnp.inf, score)
    bias = jnp.where(sel > 0.0, 0.0, jnp.where(blk < nb, NEG_INF, 0.0))
    n_fut = LANES - FUTURE_COL0
    fut_col = FUTURE_COL0 + lax.broadcasted_iota(jnp.int32, (n_fut, tq), 0)
    fut_tile = (i * tq + lax.broadcasted_iota(jnp.int32, (n_fut, tq), 1)) >> (LANES.bit_length() - 1)
    future, _ = _future_tile_columns(fut_col, fut_tile)
    bias = jnp.concatenate([bias, jnp.zeros((FUTURE_COL0 - nbp, tq), F32), future], axis=0)
    o_ref[0] = bias.T.astype(BF16)


def _moba_select_call(q, kmean_pad, *, nb, tq):
    g, _, s, _ = q.shape
    top = min(MOBA_TOPK, max(nb - 1, 1))
    return pl.pallas_call(
        functools.partial(_moba_select_kernel, nb=nb, top=top),
        grid=(g, s // tq),
        in_specs=[pl.BlockSpec((1, 1, tq, LANES), lambda gg, i: (gg, 0, i, 0)),
                  pl.BlockSpec((1, LANES, LANES), lambda gg, i: (gg, 0, 0))],
        out_specs=pl.BlockSpec((1, tq, LANES), lambda gg, i: (gg, i, 0)),
        out_shape=jax.ShapeDtypeStruct((g, s, LANES), BF16),
        compiler_params=_params(2),
        name="moba_select",
    )(q, kmean_pad)


TRIPS_PER_BODY = 8


def _attn_kernel(q_ref, qa_ref, k_ref, ka_ref, v_ref, o_ref,
                 kc_ref, vc_ref, s_ref, smax_ref, m_ref, acc_ref, *, hs, tq, tk):
    rows = hs * tq
    seq = k_ref.shape[1]
    kc_ref[:, :LANES] = k_ref[0]
    kc_ref[:, LANES:] = ka_ref[0]
    vc_ref[:, :LANES] = v_ref[0]
    vc_ref[:, LANES:] = jnp.ones((seq, LANES), BF16)
    key_minus_query = (lax.broadcasted_iota(jnp.int32, (LANES, LANES), 1)
                       - lax.broadcasted_iota(jnp.int32, (LANES, LANES), 0))

    def logits(i, j, buf):
        qs = pl.multiple_of(i * tq, tq)
        qc = jnp.concatenate([q_ref[0, :, pl.ds(qs, tq), :].reshape(rows, LANES),
                              jnp.concatenate([qa_ref[0, pl.ds(qs, tq), :]] * hs, axis=0)], axis=-1)
        s = causal(i, j, _dot_nt(qc, kc_ref[pl.ds(pl.multiple_of(j * tk, tk), tk), :]))
        s_ref[buf] = s
        smax_ref[buf] = jnp.broadcast_to(jnp.max(s, axis=-1, keepdims=True), (rows, LANES))

    def causal(i, j, s):
        delta = i * tq - j * tk
        per_tile = tq // LANES
        row_groups = []
        for rg in range(rows // LANES):
            rq = rg % per_tile
            blocks = []
            for kt in range(tk // LANES):
                blk = s[rg * LANES:(rg + 1) * LANES, kt * LANES:(kt + 1) * LANES]
                if kt >= rq and (kt - rq) % per_tile == 0:
                    blk = jnp.where(key_minus_query <= delta + (rq - kt) * LANES, blk, NEG_INF)
                blocks.append(blk)
            row_groups.append(jnp.concatenate(blocks, axis=1))
        return jnp.concatenate(row_groups, axis=0)

    def update(i, j, buf):
        s = s_ref[buf]
        m_prev = jnp.where(j == 0, -jnp.inf, m_ref[...])
        m_new = jnp.maximum(m_prev, smax_ref[buf])
        alpha = jnp.exp2(m_prev - m_new)
        p = jnp.exp2(s - jnp.tile(m_new, (1, tk // LANES))).astype(BF16)
        m_ref[...] = m_new
        acc = (jnp.tile(alpha, (1, 2)) * acc_ref[...]
               + _dot(p, vc_ref[pl.ds(pl.multiple_of(j * tk, tk), tk), :]))
        acc_ref[...] = acc
        out = (acc[:, :LANES] / acc[:, LANES:]).astype(o_ref.dtype)
        qs = pl.multiple_of(i * tq, tq)
        for hh in range(hs):
            o_ref[0, hh, pl.ds(qs, tq), :] = out[hh * tq:(hh + 1) * tq]

    def advance(i, j):
        wrap = j == (i * tq) // tk
        return jnp.where(wrap, i + 1, i), jnp.where(wrap, 0, j + 1)

    n_pairs = sum((i * tq) // tk + 1 for i in range(seq // tq))
    zero = jnp.int32(0)
    acc_ref[...] = jnp.zeros_like(acc_ref)
    m_ref[...] = jnp.zeros_like(m_ref)
    logits(zero, zero, 0)

    def trip(carry, cur, nxt):
        ib, jb, ia, ja = carry
        logits(ia, ja, nxt)
        update(ib, jb, cur)
        return (ia, ja) + advance(ia, ja)

    def trips(carry, n):
        for t in range(n):
            carry = trip(carry, t % 2, (t + 1) % 2)
        return carry

    carry = lax.fori_loop(0, (n_pairs - 1) // TRIPS_PER_BODY, lambda _, c: trips(c, TRIPS_PER_BODY),
                          (zero, zero) + advance(zero, zero))
    rest = (n_pairs - 1) % TRIPS_PER_BODY
    carry = trips(carry, rest)
    update(carry[0], carry[1], rest % 2)


def _attn_call(q, qa, k, ka, v, *, tq, tk, out_dtype, kv_groups=1, ka_shared=False):
    g, hs, s, _ = q.shape
    assert tq & (tq - 1) == 0 and tk % tq == 0 and s % tk == 0
    rows = hs * tq
    seq = lambda gg: (gg // kv_groups, 0, 0)
    ka_map = (lambda gg: (0, 0, 0)) if ka_shared else seq
    return pl.pallas_call(
        functools.partial(_attn_kernel, hs=hs, tq=tq, tk=tk),
        grid=(g,),
        in_specs=[pl.BlockSpec((1, hs, s, LANES), lambda gg: (gg, 0, 0, 0)),
                  pl.BlockSpec((1, s, LANES), seq),
                  pl.BlockSpec((1, s, LANES), seq),
                  pl.BlockSpec((1, s, LANES), ka_map),
                  pl.BlockSpec((1, s, LANES), seq)],
        out_specs=pl.BlockSpec((1, hs, s, LANES), lambda gg: (gg, 0, 0, 0)),
        out_shape=jax.ShapeDtypeStruct((g, hs, s, LANES), out_dtype),
        scratch_shapes=[pltpu.VMEM((s, 2 * LANES), BF16),
                        pltpu.VMEM((s, 2 * LANES), BF16),
                        pltpu.VMEM((2, rows, tk), F32),
                        pltpu.VMEM((2, rows, LANES), F32),
                        pltpu.VMEM((rows, LANES), F32),
                        pltpu.VMEM((rows, 2 * LANES), F32)],
        compiler_params=_params(1),
        name="flash_attn",
    )(q, qa, k, ka, v)


def _compress_kernel(x_ref, pe_ref, w1_ref, w2_ref, o_ref):
    x = x_ref[0, 0]
    half = x.shape[1]
    pe = pe_ref[0, 0]
    top = _dot((x + pe[:, :half]).astype(BF16), w1_ref[0, 0, :half, :])
    bot = _dot((x + pe[:, half:]).astype(BF16), w1_ref[0, 0, half:, :])
    n = x.shape[0]
    hid = top + pltpu.roll(bot, n - 1, 0)
    act = 0.5 * hid * (1.0 + jnp.tanh(0.7978845608028654 * (hid + 0.044715 * hid * hid * hid)))
    o_ref[0, 0] = _dot(act.astype(BF16), w2_ref[0, 0]).astype(o_ref.dtype)


def _compress_call(cv, pe, w1, w2, *, l):
    b, _, s, _ = cv.shape
    n = s // NSA_CMP_STRIDE
    width = NSA_CMP_STRIDE * LANES
    x2 = cv.reshape(b, 2, n, width)
    return pl.pallas_call(
        _compress_kernel,
        grid=(b, 2),
        in_specs=[pl.BlockSpec((1, 1, n, width), lambda bb, t: (bb, t, 0, 0)),
                  pl.BlockSpec((1, 1, 1, 2 * width), lambda bb, t: (l, t, 0, 0)),
                  pl.BlockSpec((1, 1, 2 * width, NSA_CMP_HIDDEN), lambda bb, t: (l, t, 0, 0)),
                  pl.BlockSpec((1, 1, NSA_CMP_HIDDEN, LANES), lambda bb, t: (l, t, 0, 0))],
        out_specs=pl.BlockSpec((1, 1, n, LANES), lambda bb, t: (bb, t, 0, 0)),
        out_shape=jax.ShapeDtypeStruct((b, 2, n, LANES), BF16),
        compiler_params=_params(2),
        name="nsa_compress",
    )(x2, pe, w1, w2)


def _nsa_local_kernel(q_ref, kc_ref, vc_ref, ovt_ref, kw_ref, vw_ref, small_ref,
                      ocw_ref, bias_ref, *, hn, tq, n_sb):
    i = pl.program_id(1)
    rows = hn * tq
    q = q_ref[0].reshape(rows, LANES)
    gates = _sigmoid(small_ref[0])
    nc = kc_ref.shape[2]
    s = _dot_nt(q, kc_ref[0, 0])
    r = lax.broadcasted_iota(jnp.int32, (rows, nc), 0)
    tok = i * tq + (r & (tq - 1))
    cend = lax.broadcasted_iota(jnp.int32, (rows, nc), 1) * NSA_CMP_STRIDE + (NSA_CMP_LEN - 1)
    s = jnp.where(cend <= tok, s, NEG_INF)
    e = jnp.exp2(s - jnp.max(s, axis=-1, keepdims=True))
    tok1 = i * tq + (lax.broadcasted_iota(jnp.int32, (rows, 1), 0) & (tq - 1))
    rinv = jnp.where(tok1 >= NSA_CMP_LEN - 1, 1.0 / jnp.sum(e, axis=-1, keepdims=True), 0.0)
    o_cmp = (_dot(e.astype(BF16), vc_ref[0, 0]) * rinv).reshape(hn, tq, LANES)
    psum = jnp.sum((e * rinv).reshape(hn, tq, nc), axis=0)
    hi = psum.astype(BF16)
    lo = (psum - hi.astype(F32)).astype(BF16)
    imp = _dot_nt(ovt_ref[...], hi) + _dot_nt(ovt_ref[...], lo)
    blk = lax.broadcasted_iota(jnp.int32, (LANES, tq), 0).astype(F32)
    cur = ((i * tq + lax.broadcasted_iota(jnp.int32, (LANES, tq), 1))
           >> (NSA_SEL_BLOCK.bit_length() - 1)).astype(F32)
    score = jnp.where(blk >= 1.0, jnp.where(blk <= cur - 2.0, imp, -jnp.inf), -jnp.inf)
    sel = jnp.where(blk == 0.0, 1.0, jnp.where(blk == cur, 1.0, jnp.where(blk == cur - 1.0, 1.0, 0.0)))
    for _ in range(min(NSA_N_SEL, n_sb) - 3):
        pick, m = _first_max_row(score, blk)
        sel = jnp.where(pick, jnp.where(m > -jnp.inf, 1.0, sel), sel)
        score = jnp.where(pick, -jnp.inf, score)
    bias_ref[0] = jnp.where(sel > 0.0, 0.0, NEG_INF).T.astype(BF16)
    span = NSA_WINDOW + tq
    start = pl.multiple_of(jnp.maximum(i * tq - NSA_WINDOW, 0), tq)
    sw = _dot_nt(q, kw_ref[0, pl.ds(start, span), :])
    rw = lax.broadcasted_iota(jnp.int32, (rows, span), 0)
    back = (i * tq - start) + (rw & (tq - 1)) - lax.broadcasted_iota(jnp.int32, (rows, span), 1)
    sw = jnp.where(lax.bitcast_convert_type(back, jnp.uint32) < NSA_WINDOW, sw, NEG_INF)
    ew = jnp.exp2(sw - jnp.max(sw, axis=-1, keepdims=True)).astype(BF16)
    vw1 = jnp.concatenate([vw_ref[0, pl.ds(start, span), :], jnp.ones((span, LANES), BF16)], axis=-1)
    ow = _dot(ew, vw1)
    o_win = (ow[:, :LANES] / ow[:, LANES:]).reshape(hn, tq, LANES)
    for hh in range(hn):
        ocw_ref[0, hh] = (_lane_col(gates, hh * NSA_N_BRANCH) * o_cmp[hh]
                          + _lane_col(gates, hh * NSA_N_BRANCH + 2) * o_win[hh])


def _nsa_local_call(q, kvc, overlap, kw, vw, small, *, tq):
    b, hn, s, _ = q.shape
    nc = kvc.shape[2]
    n_sb = s // NSA_SEL_BLOCK
    assert n_sb <= LANES and tq & (tq - 1) == 0 and NSA_WINDOW % tq == 0
    seq = lambda bb, i: (bb, 0, 0)
    return pl.pallas_call(
        functools.partial(_nsa_local_kernel, hn=hn, tq=tq, n_sb=n_sb),
        grid=(b, s // tq),
        in_specs=[pl.BlockSpec((1, hn, tq, LANES), lambda bb, i: (bb, 0, i, 0)),
                  pl.BlockSpec((1, 1, nc, LANES), lambda bb, i: (bb, 0, 0, 0)),
                  pl.BlockSpec((1, 1, nc, LANES), lambda bb, i: (bb, 1, 0, 0)),
                  pl.BlockSpec((LANES, nc), lambda bb, i: (0, 0)),
                  pl.BlockSpec((1, s, LANES), seq),
                  pl.BlockSpec((1, s, LANES), seq),
                  pl.BlockSpec((1, tq, LANES), lambda bb, i: (bb, i, 0))],
        out_specs=[pl.BlockSpec((1, hn, tq, LANES), lambda bb, i: (bb, 0, i, 0)),
                   pl.BlockSpec((1, tq, LANES), lambda bb, i: (bb, i, 0))],
        out_shape=[jax.ShapeDtypeStruct((b, hn, s, LANES), F32),
                   jax.ShapeDtypeStruct((b, s, LANES), BF16)],
        compiler_params=_params(2),
        name="nsa_local",
    )(q, kvc, kvc, overlap, kw, vw, small)


def _outproj_kernel(om_ref, ocw_ref, osl_ref, small_ref, of_ref, w_ref, x_ref,
                    gpost_ref, gate_ref, gpre_ref, sh_ref, sc_ref, xo_ref, h_ref,
                    *, hm, hn, hf):
    gates = _sigmoid(small_ref[0])
    parts = [om_ref[0, hh] for hh in range(hm)]
    for hh in range(hn):
        o = ocw_ref[0, hh] + _lane_col(gates, hh * NSA_N_BRANCH + 1) * osl_ref[0, hh]
        parts.append(o.astype(BF16))
    parts += [of_ref[0, hh] for hh in range(hf)]
    y = _dot(jnp.concatenate(parts, axis=-1), w_ref[0])
    xn = x_ref[0] + gate_ref[0, 0, 0] * _rms(y, gpost_ref[0])
    xo_ref[0] = xn
    h_ref[0] = (_rms(xn, gpre_ref[0]) * (1.0 + sc_ref[0, 0, 0]) + sh_ref[0, 0, 0]).astype(BF16)


def _outproj_call(o_moba, o_cw, o_slc, small, o_fox, w_out, x, g_post, g_pre, mod6, *, l, tm):
    b, s, d = x.shape
    hm, hn, hf = o_moba.shape[1], o_cw.shape[1], o_fox.shape[1]
    head_spec = lambda n: pl.BlockSpec((1, n, tm, LANES), lambda bb, i: (bb, 0, i, 0))
    row = lambda bb, i: (bb, i, 0)
    return pl.pallas_call(
        functools.partial(_outproj_kernel, hm=hm, hn=hn, hf=hf),
        grid=(b, s // tm),
        in_specs=[head_spec(hm), head_spec(hn), head_spec(hn),
                  pl.BlockSpec((1, tm, LANES), row), head_spec(hf),
                  _resident((1,) + w_out.shape[1:], lambda bb, i: (l, 0, 0)),
                  pl.BlockSpec((1, tm, d), row), _layer_vec(l, d), _mod_vec(l, 2, d),
                  _layer_vec(l, d), _mod_vec(l, 3, d), _mod_vec(l, 4, d)],
        out_specs=[pl.BlockSpec((1, tm, d), row), pl.BlockSpec((1, tm, d), row)],
        out_shape=[jax.ShapeDtypeStruct((b, s, d), F32), jax.ShapeDtypeStruct((b, s, d), BF16)],
        compiler_params=_params(2),
        name="outproj",
    )(o_moba, o_cw, o_slc, small, o_fox, w_out, x, g_post, mod6, g_pre, mod6, mod6)


def _mlp_kernel(h_ref, wu_ref, wd_ref, x_ref, g_ref, gate_ref, o_ref, acc_ref):
    f = pl.program_id(2)

    @pl.when(f == 0)
    def _():
        acc_ref[...] = jnp.zeros_like(acc_ref)

    u = jnp.maximum(_dot(h_ref[0], wu_ref[0]), 0.0)
    acc_ref[...] += _dot((u * u).astype(BF16), wd_ref[0])

    @pl.when(f == pl.num_programs(2) - 1)
    def _():
        o_ref[0] = x_ref[0] + gate_ref[0, 0, 0] * _rms(acc_ref[...], g_ref[0])


def _mlp_call(h, w_up, w_down, x, g_post, mod6, *, l, tm, tf):
    b, s, d = x.shape
    dff = w_up.shape[2]
    row = lambda bb, i, f: (bb, i, 0)
    return pl.pallas_call(
        _mlp_kernel,
        grid=(b, s // tm, dff // tf),
        in_specs=[pl.BlockSpec((1, tm, d), row),
                  pl.BlockSpec((1, d, tf), lambda bb, i, f: (l, 0, f)),
                  pl.BlockSpec((1, tf, d), lambda bb, i, f: (l, f, 0)),
                  pl.BlockSpec((1, tm, d), row),
                  _layer_vec(l, d), _mod_vec(l, 5, d)],
        out_specs=pl.BlockSpec((1, tm, d), row),
        out_shape=jax.ShapeDtypeStruct((b, s, d), F32),
        scratch_shapes=[pltpu.VMEM((tm, d), F32)],
        compiler_params=_params(3),
        name="mlp",
    )(h, w_up, w_down, x, g_post, mod6)


def _tile(n, pref):
    t = min(n, pref)
    assert n % t == 0
    return t


def _rope_tables(positions):
    inv_freq = ROPE_THETA ** (-jnp.arange(ROPE_HALF, dtype=F32) / ROPE_HALF)
    ang = positions.astype(F32)[..., None] * inv_freq
    cos, sin = jnp.cos(ang), jnp.sin(ang)
    ones = jnp.ones(ang.shape[:-1] + (LANES - ROPE_DIMS,), F32)
    cos_t = jnp.concatenate([cos, cos, ones], axis=-1)
    sin_t = jnp.concatenate([-sin, sin, 0.0 * ones], axis=-1)
    return cos_t, sin_t


def _block_onehot(s, block, future_tiles=False):
    cols = np.arange(LANES)[None, :]
    onehot = (np.arange(s) // block)[:, None] == cols
    if future_tiles:
        assert s // block <= FUTURE_COL0
        onehot = onehot | ((np.arange(s) // LANES)[:, None] == cols - FUTURE_COL0)
    return jnp.asarray(onehot.astype(np.float32), dtype=BF16)[None]


def _overlap_matrix(s):
    nc = s // NSA_CMP_STRIDE
    n_cmp = (s - NSA_CMP_LEN) // NSA_CMP_STRIDE + 1
    c = np.arange(nc)
    first = (c * NSA_CMP_STRIDE) // NSA_SEL_BLOCK
    last = (c * NSA_CMP_STRIDE + NSA_CMP_LEN - 1) // NSA_SEL_BLOCK
    sb = np.arange(LANES)
    ov = (sb[None, :] >= first[:, None]) & (sb[None, :] <= last[:, None]) & (c[:, None] < n_cmp)
    return jnp.asarray(ov.T.astype(np.float32), dtype=BF16)


def kernel(x, c, positions, w_mod, b_mod, g_pre_mix, g_post_mix, g_pre_mlp, g_post_mlp, w_in,
           b_forget, cmp_pe_k, cmp_pe_v, cmp_w1_k, cmp_w2_k, cmp_w1_v, cmp_w2_v, w_out, w_up, w_down):
    b, s, d = x.shape
    depth = w_mod.shape[0]
    n_heads = d // HEAD_DIM
    hm = n_heads // 4
    hn = n_heads // 4
    hf = n_heads - hm - hn
    mw, nw, fw = hm * HEAD_DIM, hn * HEAD_DIM, hf * HEAD_DIM
    n_gate = hn * NSA_N_BRANCH
    o_g = 3 * mw + nw + NSA_N_KV * HEAD_DIM
    o_f = o_g + n_gate
    o_ff = o_f + 3 * fw
    assert n_gate + hf <= LANES and s % MOBA_BLOCK == 0 and s // LANES <= LANES - FUTURE_COL0

    c_pad = jnp.zeros((8, d), F32).at[:b].set(c)
    mod = _mod_call(c_pad, w_mod, b_mod)
    mod6 = mod[:, :b].reshape(depth, b, 6, 1, d).transpose(0, 2, 1, 3, 4)
    cos_t, sin_t = _rope_tables(positions)
    moba_ka = _block_onehot(s, MOBA_BLOCK, future_tiles=True)
    slc_ka = _block_onehot(s, NSA_SEL_BLOCK)
    overlap = _overlap_matrix(s)
    nb = s // MOBA_BLOCK

    w_cat = _wcat_call(jnp.swapaxes(w_in, 1, 2), o_g=o_g, o_f=o_f, o_ff=o_ff, tc=_tile(d, 256))
    w_out_b, w_up_b, w_down_b = w_out.astype(BF16), w_up.astype(BF16), w_down.astype(BF16)
    cmp_pe = jnp.stack([cmp_pe_k, cmp_pe_v], axis=1).reshape(depth, 2, 1, NSA_CMP_LEN * HEAD_DIM)
    cmp_w1 = jnp.stack([cmp_w1_k, cmp_w1_v], axis=1).astype(BF16)
    cmp_w2 = jnp.stack([cmp_w2_k, cmp_w2_v], axis=1).astype(BF16)
    bf_rows = jnp.zeros((depth, 1, LANES), F32).at[:, 0, n_gate:n_gate + hf].set(b_forget)
    gains = [g.reshape(depth, 1, d) for g in (g_pre_mix, g_post_mix, g_pre_mlp, g_post_mlp)]

    tm_in = _tile(s, 512)
    for l in range(depth):
        (mq, mk, mv, kmean, nq, cv, ks, vs, kw, vw, small, fq, fk, fv) = _inproj_call(
            x, gains[0], mod6, cos_t, sin_t, w_cat, l=l, hm=hm, hn=hn, hf=hf, tm=tm_in)

        fqa, fka = _fox_prep_call(small, bf_rows, l=l, hf=hf, col0=n_gate, ts=_tile(s, 512))
        fold = lambda t: t.reshape(b * t.shape[1], 1, s, LANES)
        flat = lambda t: t.reshape(b * t.shape[1], s, LANES)
        o_fox = _attn_call(fold(fq), flat(fqa), flat(fk), flat(fka), flat(fv),
                           tq=_tile(s, 512), tk=_tile(s, 1024), out_dtype=BF16).reshape(b, hf, s, LANES)

        kmean = kmean.reshape(b, s // tm_in, hm, tm_in // MOBA_BLOCK, LANES)
        kmean = kmean.transpose(0, 2, 1, 3, 4).reshape(b * hm, nb, LANES)
        kmean = jnp.pad(kmean, ((0, 0), (0, LANES - nb), (0, 0)))
        mqa = _moba_select_call(fold(mq), kmean, nb=nb, tq=_tile(s, 512))
        o_moba = _attn_call(fold(mq), mqa, flat(mk), moba_ka, flat(mv), tq=_tile(s, 512),
                            tk=_tile(s, 1024), out_dtype=BF16, ka_shared=True).reshape(b, hm, s, LANES)

        kvc = _compress_call(cv, cmp_pe, cmp_w1, cmp_w2, l=l)
        o_cw, sbias = _nsa_local_call(nq, kvc, overlap, kw, vw, small, tq=_tile(s, 256))
        slc_split = 2 if hn % 2 == 0 else 1
        o_slc = _attn_call(nq.reshape(b * slc_split, hn // slc_split, s, LANES), sbias, ks, slc_ka, vs,
                           tq=_tile(s, 256), tk=_tile(s, 1024), out_dtype=BF16,
                           kv_groups=slc_split, ka_shared=True).reshape(b, hn, s, LANES)

        x, h_mlp = _outproj_call(o_moba, o_cw, o_slc, small, o_fox, w_out_b, x,
                                 gains[1], gains[2], mod6, l=l, tm=_tile(s, 512))
        x = _mlp_call(h_mlp, w_up_b, w_down_b, x, gains[3], mod6, l=l,
                      tm=_tile(s, 512), tf=1024)
    return x
```

```python
import functools

import jax
import jax.numpy as jnp
import numpy as np
from jax import lax
from jax.experimental import pallas as pl
from jax.experimental.pallas import tpu as pltpu

HEAD_DIM = 128
ROPE_THETA = 500000.0
ROPE_DIMS = HEAD_DIM // 4
ROPE_HALF = ROPE_DIMS // 2
MOBA_BLOCK = 256
MOBA_TOPK = 3
NSA_CMP_LEN = 32
NSA_CMP_STRIDE = 16
NSA_CMP_HIDDEN = 256
NSA_SEL_BLOCK = 64
NSA_N_SEL = 16
NSA_WINDOW = 512
NSA_N_BRANCH = 3
NSA_N_KV = 6
NORM_EPS = 1e-6
NEG_INF = -1e30
ATTN_SCALE = HEAD_DIM ** -0.5
LOG2E = 1.4426950408889634
Q_SCALE = ATTN_SCALE * LOG2E

LANES = 128
VMEM_LIMIT_BYTES = 56 * 1024 * 1024

F32 = jnp.float32
BF16 = jnp.bfloat16


def _params(n_axes):
    return pltpu.CompilerParams(dimension_semantics=("arbitrary",) * n_axes,
                                vmem_limit_bytes=VMEM_LIMIT_BYTES)


def _resident(block_shape, index_map):
    return pl.BlockSpec(block_shape, index_map, pipeline_mode=pl.Buffered(1))


def _split3(x):
    p1 = x.astype(BF16)
    r1 = x - p1.astype(F32)
    p2 = r1.astype(BF16)
    p3 = (r1 - p2.astype(F32)).astype(BF16)
    return p1, p2, p3


def _dot(a, b):
    return jnp.dot(a, b, preferred_element_type=F32)


def _dot_nt(a, b):
    return lax.dot_general(a, b, (((1,), (1,)), ((), ())), preferred_element_type=F32)


def _rms(x, g):
    return x * lax.rsqrt(jnp.mean(x * x, axis=-1, keepdims=True) + NORM_EPS) * g


def _sigmoid(z):
    return 1.0 / (1.0 + jnp.exp(-z))


FUTURE_COL0 = LANES // 2


def _future_tile_columns(col, tile):
    c = col - FUTURE_COL0
    qa = jnp.where(c > tile, NEG_INF, 0.0)
    ka = jnp.where(c == tile, 1.0, 0.0)
    return qa, ka


def _lane_col(x, c):
    return jnp.broadcast_to(x[:, c:c + 1], x.shape)


def _mod_kernel(c_ref, w_ref, b_ref, o_ref):
    c = c_ref[...]
    ca = c * _sigmoid(c)
    o_ref[0] = jnp.dot(ca, w_ref[0], precision=lax.Precision.HIGHEST,
                       preferred_element_type=F32) + b_ref[0]


def _mod_call(c_pad, w_mod, b_mod):
    depth, d, n = w_mod.shape
    rows = c_pad.shape[0]
    tn = 1024
    assert n % tn == 0
    return pl.pallas_call(
        _mod_kernel,
        grid=(depth, n // tn),
        in_specs=[pl.BlockSpec((rows, d), lambda l, j: (0, 0)),
                  pl.BlockSpec((1, d, tn), lambda l, j: (l, 0, j)),
                  pl.BlockSpec((1, 1, tn), lambda l, j: (l, 0, j))],
        out_specs=pl.BlockSpec((1, rows, tn), lambda l, j: (l, 0, j)),
        out_shape=jax.ShapeDtypeStruct((depth, rows, n), F32),
        compiler_params=_params(2),
        name="adaln_mod",
    )(c_pad, w_mod, b_mod.reshape(depth, 1, n))


def _wcat_kernel(w_ref, o_ref, *, o_g, o_f, o_ff):
    n_in = w_ref.shape[1]
    n_fox = o_ff - o_f
    n_small = (o_f - o_g) + (n_in - o_ff)
    tc = o_ref.shape[2]
    o_ref[0, :o_g, :] = w_ref[0, :o_g, :].astype(BF16)
    o_ref[0, o_g:o_g + n_fox, :] = w_ref[0, o_f:o_ff, :].astype(BF16)
    small = jnp.concatenate([w_ref[0, o_g:o_f, :], w_ref[0, o_ff:, :],
                             jnp.zeros((LANES - n_small, tc), F32)], axis=0)
    o_ref[0, o_g + n_fox:, :] = small.astype(BF16)


def _wcat_call(w_in_t, *, o_g, o_f, o_ff, tc):
    depth, n_in, d = w_in_t.shape
    npad = o_g + (o_ff - o_f) + LANES
    assert o_g % LANES == 0 and (o_ff - o_f) % LANES == 0
    return pl.pallas_call(
        functools.partial(_wcat_kernel, o_g=o_g, o_f=o_f, o_ff=o_ff),
        grid=(depth, d // tc),
        in_specs=[pl.BlockSpec((1, n_in, tc), lambda l, i: (l, 0, i))],
        out_specs=pl.BlockSpec((1, npad, tc), lambda l, i: (l, 0, i)),
        out_shape=jax.ShapeDtypeStruct((depth, npad, d), BF16),
        compiler_params=_params(2),
        name="w_in_layout",
    )(w_in_t)


def _inproj_kernel(x_ref, g_ref, sh_ref, sc_ref, cos_ref, sin_ref, w_ref,
                   mq_ref, mk_ref, mv_ref, kmean_ref, nq_ref, cv_ref,
                   ks_ref, vs_ref, kw_ref, vw_ref, small_ref, fq_ref, fk_ref, fv_ref,
                   *, hm, hn, hf):
    x = x_ref[0]
    h = _rms(x, g_ref[0]) * (1.0 + sc_ref[0, 0, 0]) + sh_ref[0, 0, 0]
    hb = h.astype(BF16)
    tm = x.shape[0]
    cos = cos_ref[0]
    sin = sin_ref[0]
    lane = lax.broadcasted_iota(jnp.int32, (tm, LANES), 1)

    def rope(y):
        swapped = jnp.where(lane < ROPE_HALF, pltpu.roll(y, LANES - ROPE_HALF, 1),
                            pltpu.roll(y, ROPE_HALF, 1))
        return y * cos + swapped * sin

    def heads(col0, n):
        outs = []
        j = 0
        while j < n:
            w = 2 if j + 1 < n else 1
            y = _dot_nt(hb, w_ref[0, (col0 + j) * LANES:(col0 + j + w) * LANES, :])
            for t in range(w):
                outs.append(y[:, t * LANES:(t + 1) * LANES])
            j += w
        return outs

    col = 0
    for hh, y in enumerate(heads(col, hm)):
        mq_ref[0, hh] = (rope(y) * Q_SCALE).astype(BF16)
    col += hm
    nblk = tm // MOBA_BLOCK
    means = []
    for hh, y in enumerate(heads(col, hm)):
        yr = rope(y)
        mk_ref[0, hh] = yr.astype(BF16)
        means.append(jnp.mean(yr.reshape(nblk, MOBA_BLOCK, LANES), axis=1))
    kmean_ref[0, 0] = jnp.concatenate(means, axis=0)
    col += hm
    for hh, y in enumerate(heads(col, hm)):
        mv_ref[0, hh] = y.astype(BF16)
    col += hm
    for hh, y in enumerate(heads(col, hn)):
        nq_ref[0, hh] = (rope(y) * Q_SCALE).astype(BF16)
    col += hn
    kc, vc, ks, vs, kw, vw = heads(col, NSA_N_KV)
    cv_ref[0, 0] = rope(kc)
    cv_ref[0, 1] = vc
    ks_ref[0] = rope(ks).astype(BF16)
    vs_ref[0] = vs.astype(BF16)
    kw_ref[0] = rope(kw).astype(BF16)
    vw_ref[0] = vw.astype(BF16)
    col += NSA_N_KV
    for hh, y in enumerate(heads(col, hf)):
        fq_ref[0, hh] = (y * Q_SCALE).astype(BF16)
    col += hf
    for hh, y in enumerate(heads(col, hf)):
        fk_ref[0, hh] = y.astype(BF16)
    col += hf
    for hh, y in enumerate(heads(col, hf)):
        fv_ref[0, hh] = y.astype(BF16)
    col += hf
    small_ref[0] = heads(col, 1)[0]


def _layer_vec(l, d):
    return pl.BlockSpec((1, 1, d), lambda *_: (l, 0, 0))


def _mod_vec(l, k, d):
    return pl.BlockSpec((1, 1, 1, 1, d), lambda bb, *_: (l, k, bb, 0, 0))


def _inproj_call(x, g, mod6, cos_t, sin_t, w_cat, *, l, hm, hn, hf, tm):
    b, s, d = x.shape
    npad = w_cat.shape[1]
    nblk = tm // MOBA_BLOCK
    row = lambda bb, i: (bb, i, 0)
    head_spec = lambda n: pl.BlockSpec((1, n, tm, LANES), lambda bb, i: (bb, 0, i, 0))
    head_shape = lambda n, dt=BF16: jax.ShapeDtypeStruct((b, n, s, LANES), dt)
    tok_spec = pl.BlockSpec((1, tm, LANES), row)
    tok_shape = lambda dt=BF16: jax.ShapeDtypeStruct((b, s, LANES), dt)
    out_specs = [head_spec(hm), head_spec(hm), head_spec(hm),
                 pl.BlockSpec((1, 1, hm * nblk, LANES), lambda bb, i: (bb, i, 0, 0)),
                 head_spec(hn), head_spec(2),
                 tok_spec, tok_spec, tok_spec, tok_spec, tok_spec,
                 head_spec(hf), head_spec(hf), head_spec(hf)]
    out_shape = [head_shape(hm), head_shape(hm), head_shape(hm),
                 jax.ShapeDtypeStruct((b, s // tm, hm * nblk, LANES), F32),
                 head_shape(hn), head_shape(2, F32),
                 tok_shape(), tok_shape(), tok_shape(), tok_shape(), tok_shape(F32),
                 head_shape(hf), head_shape(hf), head_shape(hf)]
    return pl.pallas_call(
        functools.partial(_inproj_kernel, hm=hm, hn=hn, hf=hf),
        grid=(b, s // tm),
        in_specs=[pl.BlockSpec((1, tm, d), row),
                  _layer_vec(l, d), _mod_vec(l, 0, d), _mod_vec(l, 1, d),
                  pl.BlockSpec((1, tm, LANES), row),
                  pl.BlockSpec((1, tm, LANES), row),
                  _resident((1, npad, d), lambda bb, i: (l, 0, 0))],
        out_specs=out_specs,
        out_shape=out_shape,
        compiler_params=_params(2),
        name="inproj",
    )(x, g, mod6, mod6, cos_t, sin_t, w_cat)


def _fox_prep_kernel(small_ref, bf_ref, qa_ref, ka_ref, carry_ref, *, hf, col0):
    i = pl.program_id(1)

    @pl.when(i == 0)
    def _():
        carry_ref[...] = jnp.zeros_like(carry_ref)

    z = small_ref[0] + bf_ref[0]
    logf = jnp.minimum(z, 0.0) - jnp.log1p(jnp.exp(-jnp.abs(z)))
    ts = z.shape[0]
    r = lax.broadcasted_iota(jnp.int32, (ts, ts), 0)
    c = lax.broadcasted_iota(jnp.int32, (ts, ts), 1)
    tri = jnp.where(r >= c, 1.0, 0.0).astype(BF16)
    p1, p2, p3 = _split3(logf)
    cum = _dot(tri, p1) + _dot(tri, p2) + _dot(tri, p3) + carry_ref[...]
    carry_ref[...] = cum[ts - 1:ts, :]
    lane = lax.broadcasted_iota(jnp.int32, (ts, LANES), 1)
    tile = (i * ts + lax.broadcasted_iota(jnp.int32, (ts, LANES), 0)) >> (LANES.bit_length() - 1)
    qa_future, ka_future = _future_tile_columns(lane, tile)
    for hh in range(hf):
        cb = _lane_col(cum, col0 + hh) * LOG2E
        c1, c2, c3 = (p.astype(F32) for p in _split3(cb))
        qa = jnp.where(lane == 0, c1, jnp.where(lane == 1, c2, jnp.where(
            lane == 2, c3, jnp.where(lane < 6, 1.0, qa_future))))
        ka = jnp.where(lane < 3, 1.0, jnp.where(lane == 3, -c1, jnp.where(
            lane == 4, -c2, jnp.where(lane == 5, -c3, ka_future))))
        qa_ref[0, hh] = qa.astype(BF16)
        ka_ref[0, hh] = ka.astype(BF16)


def _fox_prep_call(small, bf_rows, *, l, hf, col0, ts):
    b, s, _ = small.shape
    spec = pl.BlockSpec((1, hf, ts, LANES), lambda bb, i: (bb, 0, i, 0))
    shape = jax.ShapeDtypeStruct((b, hf, s, LANES), BF16)
    return pl.pallas_call(
        functools.partial(_fox_prep_kernel, hf=hf, col0=col0),
        grid=(b, s // ts),
        in_specs=[pl.BlockSpec((1, ts, LANES), lambda bb, i: (bb, i, 0)),
                  _layer_vec(l, LANES)],
        out_specs=[spec, spec],
        out_shape=[shape, shape],
        scratch_shapes=[pltpu.VMEM((1, LANES), F32)],
        compiler_params=_params(2),
        name="fox_prep",
    )(small, bf_rows)


def _first_max_row(score, blk):
    m = jnp.max(score, axis=0, keepdims=True)
    idx = jnp.min(jnp.where(score == m, blk, float(score.shape[0])), axis=0, keepdims=True)
    return blk == idx, m


def _moba_select_kernel(q_ref, km_ref, o_ref, *, nb, top):
    i = pl.program_id(1)
    q = q_ref[0, 0]
    tq = q.shape[0]
    nbp = -(-nb // 8) * 8
    k1, k2, k3 = _split3(km_ref[0, :nbp, :])
    gate = _dot_nt(k1, q) + _dot_nt(k2, q) + _dot_nt(k3, q)
    blk = lax.broadcasted_iota(jnp.int32, (nbp, tq), 0).astype(F32)
    tok = i * tq + lax.broadcasted_iota(jnp.int32, (nbp, tq), 1)
    own = (tok >> (MOBA_BLOCK.bit_length() - 1)).astype(F32)
    past = blk < own
    score = jnp.where(blk < nb, jnp.where(past, gate, NEG_INF), -jnp.inf)
    sel = jnp.where(blk == own, 1.0, 0.0)
    for _ in range(top):
        pick, _m = _first_max_row(score, blk)
        sel = jnp.where(pick, jnp.where(past, 1.0, sel), sel)
        score = jnp.where(pick, -jnp.inf, score)
    bias = jnp.where(sel > 0.0, 0.0, jnp.where(blk < nb, NEG_INF, 0.0))
    n_fut = LANES - FUTURE_COL0
    fut_col = FUTURE_COL0 + lax.broadcasted_iota(jnp.int32, (n_fut, tq), 0)
    fut_tile = (i * tq + lax.broadcasted_iota(jnp.int32, (n_fut, tq), 1)) >> (LANES.bit_length() - 1)
    future, _ = _future_tile_columns(fut_col, fut_tile)
    bias = jnp.concatenate([bias, jnp.zeros((FUTURE_COL0 - nbp, tq), F32), future], axis=0)
    o_ref[0] = bias.T.astype(BF16)


def _moba_select_call(q, kmean_pad, *, nb, tq):
    g, _, s, _ = q.shape
    top = min(MOBA_TOPK, max(nb - 1, 1))
    return pl.pallas_call(
        functools.partial(_moba_select_kernel, nb=nb, top=top),
        grid=(g, s // tq),
        in_specs=[pl.BlockSpec((1, 1, tq, LANES), lambda gg, i: (gg, 0, i, 0)),
                  pl.BlockSpec((1, LANES, LANES), lambda gg, i: (gg, 0, 0))],
        out_specs=pl.BlockSpec((1, tq, LANES), lambda gg, i: (gg, i, 0)),
        out_shape=jax.ShapeDtypeStruct((g, s, LANES), BF16),
        compiler_params=_params(2),
        name="moba_select",
    )(q, kmean_pad)


TRIPS_PER_BODY = 8


def _attn_kernel(q_ref, qa_ref, k_ref, ka_ref, v_ref, o_ref,
                 kc_ref, vc_ref, s_ref, smax_ref, m_ref, acc_ref, *, hs, tq, tk):
    rows = hs * tq
    seq = k_ref.shape[1]
    kc_ref[:, :LANES] = k_ref[0]
    kc_ref[:, LANES:] = ka_ref[0]
    vc_ref[:, :LANES] = v_ref[0]
    vc_ref[:, LANES:] = jnp.ones((seq, LANES), BF16)
    key_minus_query = (lax.broadcasted_iota(jnp.int32, (LANES, LANES), 1)
                       - lax.broadcasted_iota(jnp.int32, (LANES, LANES), 0))

    def logits(i, j, buf):
        qs = pl.multiple_of(i * tq, tq)
        qc = jnp.concatenate([q_ref[0, :, pl.ds(qs, tq), :].reshape(rows, LANES),
                              jnp.concatenate([qa_ref[0, pl.ds(qs, tq), :]] * hs, axis=0)], axis=-1)
        s = causal(i, j, _dot_nt(qc, kc_ref[pl.ds(pl.multiple_of(j * tk, tk), tk), :]))
        s_ref[buf] = s
        smax_ref[buf] = jnp.broadcast_to(jnp.max(s, axis=-1, keepdims=True), (rows, LANES))

    def causal(i, j, s):
        delta = i * tq - j * tk
        per_tile = tq // LANES
        row_groups = []
        for rg in range(rows // LANES):
            rq = rg % per_tile
            blocks = []
            for kt in range(tk // LANES):
                blk = s[rg * LANES:(rg + 1) * LANES, kt * LANES:(kt + 1) * LANES]
                if kt >= rq and (kt - rq) % per_tile == 0:
                    blk = jnp.where(key_minus_query <= delta + (rq - kt) * LANES, blk, NEG_INF)
                blocks.append(blk)
            row_groups.append(jnp.concatenate(blocks, axis=1))
        return jnp.concatenate(row_groups, axis=0)

    def update(i, j, buf):
        s = s_ref[buf]
        m_prev = jnp.where(j == 0, -jnp.inf, m_ref[...])
        m_new = jnp.maximum(m_prev, smax_ref[buf])
        alpha = jnp.exp2(m_prev - m_new)
        p = jnp.exp2(s - jnp.tile(m_new, (1, tk // LANES))).astype(BF16)
        m_ref[...] = m_new
        acc = (jnp.tile(alpha, (1, 2)) * acc_ref[...]
               + _dot(p, vc_ref[pl.ds(pl.multiple_of(j * tk, tk), tk), :]))
        acc_ref[...] = acc
        out = (acc[:, :LANES] / acc[:, LANES:]).astype(o_ref.dtype)
        qs = pl.multiple_of(i * tq, tq)
        for hh in range(hs):
            o_ref[0, hh, pl.ds(qs, tq), :] = out[hh * tq:(hh + 1) * tq]

    def advance(i, j):
        wrap = j == (i * tq) // tk
        return jnp.where(wrap, i + 1, i), jnp.where(wrap, 0, j + 1)

    n_pairs = sum((i * tq) // tk + 1 for i in range(seq // tq))
    zero = jnp.int32(0)
    acc_ref[...] = jnp.zeros_like(acc_ref)
    m_ref[...] = jnp.zeros_like(m_ref)
    logits(zero, zero, 0)

    def trip(carry, cur, nxt):
        ib, jb, ia, ja = carry
        logits(ia, ja, nxt)
        update(ib, jb, cur)
        return (ia, ja) + advance(ia, ja)

    def trips(carry, n):
        for t in range(n):
            carry = trip(carry, t % 2, (t + 1) % 2)
        return carry

    carry = lax.fori_loop(0, (n_pairs - 1) // TRIPS_PER_BODY, lambda _, c: trips(c, TRIPS_PER_BODY),
                          (zero, zero) + advance(zero, zero))
    rest = (n_pairs - 1) % TRIPS_PER_BODY
    carry = trips(carry, rest)
    update(carry[0], carry[1], rest % 2)


def _attn_call(q, qa, k, ka, v, *, tq, tk, out_dtype, kv_groups=1, ka_shared=False):
    g, hs, s, _ = q.shape
    assert tq & (tq - 1) == 0 and tk % tq == 0 and s % tk == 0
    rows = hs * tq
    seq = lambda gg: (gg // kv_groups, 0, 0)
    ka_map = (lambda gg: (0, 0, 0)) if ka_shared else seq
    return pl.pallas_call(
        functools.partial(_attn_kernel, hs=hs, tq=tq, tk=tk),
        grid=(g,),
        in_specs=[pl.BlockSpec((1, hs, s, LANES), lambda gg: (gg, 0, 0, 0)),
                  pl.BlockSpec((1, s, LANES), seq),
                  pl.BlockSpec((1, s, LANES), seq),
                  pl.BlockSpec((1, s, LANES), ka_map),
                  pl.BlockSpec((1, s, LANES), seq)],
        out_specs=pl.BlockSpec((1, hs, s, LANES), lambda gg: (gg, 0, 0, 0)),
        out_shape=jax.ShapeDtypeStruct((g, hs, s, LANES), out_dtype),
        scratch_shapes=[pltpu.VMEM((s, 2 * LANES), BF16),
                        pltpu.VMEM((s, 2 * LANES), BF16),
                        pltpu.VMEM((2, rows, tk), F32),
                        pltpu.VMEM((2, rows, LANES), F32),
                        pltpu.VMEM((rows, LANES), F32),
                        pltpu.VMEM((rows, 2 * LANES), F32)],
        compiler_params=_params(1),
        name="flash_attn",
    )(q, qa, k, ka, v)


def _compress_kernel(x_ref, pe_ref, w1_ref, w2_ref, o_ref):
    x = x_ref[0, 0]
    half = x.shape[1]
    pe = pe_ref[0, 0]
    top = _dot((x + pe[:, :half]).astype(BF16), w1_ref[0, 0, :half, :])
    bot = _dot((x + pe[:, half:]).astype(BF16), w1_ref[0, 0, half:, :])
    n = x.shape[0]
    hid = top + pltpu.roll(bot, n - 1, 0)
    act = 0.5 * hid * (1.0 + jnp.tanh(0.7978845608028654 * (hid + 0.044715 * hid * hid * hid)))
    o_ref[0, 0] = _dot(act.astype(BF16), w2_ref[0, 0]).astype(o_ref.dtype)


def _compress_call(cv, pe, w1, w2, *, l):
    b, _, s, _ = cv.shape
    n = s // NSA_CMP_STRIDE
    width = NSA_CMP_STRIDE * LANES
    x2 = cv.reshape(b, 2, n, width)
    return pl.pallas_call(
        _compress_kernel,
        grid=(b, 2),
        in_specs=[pl.BlockSpec((1, 1, n, width), lambda bb, t: (bb, t, 0, 0)),
                  pl.BlockSpec((1, 1, 1, 2 * width), lambda bb, t: (l, t, 0, 0)),
                  pl.BlockSpec((1, 1, 2 * width, NSA_CMP_HIDDEN), lambda bb, t: (l, t, 0, 0)),
                  pl.BlockSpec((1, 1, NSA_CMP_HIDDEN, LANES), lambda bb, t: (l, t, 0, 0))],
        out_specs=pl.BlockSpec((1, 1, n, LANES), lambda bb, t: (bb, t, 0, 0)),
        out_shape=jax.ShapeDtypeStruct((b, 2, n, LANES), BF16),
        compiler_params=_params(2),
        name="nsa_compress",
    )(x2, pe, w1, w2)


def _nsa_local_kernel(q_ref, kc_ref, vc_ref, ovt_ref, kw_ref, vw_ref, small_ref,
                      ocw_ref, bias_ref, *, hn, tq, n_sb):
    i = pl.program_id(1)
    rows = hn * tq
    q = q_ref[0].reshape(rows, LANES)
    gates = _sigmoid(small_ref[0])
    nc = kc_ref.shape[2]
    s = _dot_nt(q, kc_ref[0, 0])
    r = lax.broadcasted_iota(jnp.int32, (rows, nc), 0)
    tok = i * tq + (r & (tq - 1))
    cend = lax.broadcasted_iota(jnp.int32, (rows, nc), 1) * NSA_CMP_STRIDE + (NSA_CMP_LEN - 1)
    s = jnp.where(cend <= tok, s, NEG_INF)
    e = jnp.exp2(s - jnp.max(s, axis=-1, keepdims=True))
    tok1 = i * tq + (lax.broadcasted_iota(jnp.int32, (rows, 1), 0) & (tq - 1))
    rinv = jnp.where(tok1 >= NSA_CMP_LEN - 1, 1.0 / jnp.sum(e, axis=-1, keepdims=True), 0.0)
    o_cmp = (_dot(e.astype(BF16), vc_ref[0, 0]) * rinv).reshape(hn, tq, LANES)
    psum = jnp.sum((e * rinv).reshape(hn, tq, nc), axis=0)
    hi = psum.astype(BF16)
    lo = (psum - hi.astype(F32)).astype(BF16)
    imp = _dot_nt(ovt_ref[...], hi) + _dot_nt(ovt_ref[...], lo)
    blk = lax.broadcasted_iota(jnp.int32, (LANES, tq), 0).astype(F32)
    cur = ((i * tq + lax.broadcasted_iota(jnp.int32, (LANES, tq), 1))
           >> (NSA_SEL_BLOCK.bit_length() - 1)).astype(F32)
    score = jnp.where(blk >= 1.0, jnp.where(blk <= cur - 2.0, imp, -jnp.inf), -jnp.inf)
    sel = jnp.where(blk == 0.0, 1.0, jnp.where(blk == cur, 1.0, jnp.where(blk == cur - 1.0, 1.0, 0.0)))
    for _ in range(min(NSA_N_SEL, n_sb) - 3):
        pick, m = _first_max_row(score, blk)
        sel = jnp.where(pick, jnp.where(m > -jnp.inf, 1.0, sel), sel)
        score = jnp.where(pick, -jnp.inf, score)
    bias_ref[0] = jnp.where(sel > 0.0, 0.0, NEG_INF).T.astype(BF16)
    span = NSA_WINDOW + tq
    start = pl.multiple_of(jnp.maximum(i * tq - NSA_WINDOW, 0), tq)
    sw = _dot_nt(q, kw_ref[0, pl.ds(start, span), :])
    rw = lax.broadcasted_iota(jnp.int32, (rows, span), 0)
    back = (i * tq - start) + (rw & (tq - 1)) - lax.broadcasted_iota(jnp.int32, (rows, span), 1)
    sw = jnp.where(lax.bitcast_convert_type(back, jnp.uint32) < NSA_WINDOW, sw, NEG_INF)
    ew = jnp.exp2(sw - jnp.max(sw, axis=-1, keepdims=True)).astype(BF16)
    vw1 = jnp.concatenate([vw_ref[0, pl.ds(start, span), :], jnp.ones((span, LANES), BF16)], axis=-1)
    ow = _dot(ew, vw1)
    o_win = (ow[:, :LANES] / ow[:, LANES:]).reshape(hn, tq, LANES)
    for hh in range(hn):
        ocw_ref[0, hh] = (_lane_col(gates, hh * NSA_N_BRANCH) * o_cmp[hh]
                          + _lane_col(gates, hh * NSA_N_BRANCH + 2) * o_win[hh])


def _nsa_local_call(q, kvc, overlap, kw, vw, small, *, tq):
    b, hn, s, _ = q.shape
    nc = kvc.shape[2]
    n_sb = s // NSA_SEL_BLOCK
    assert n_sb <= LANES and tq & (tq - 1) == 0 and NSA_WINDOW % tq == 0
    seq = lambda bb, i: (bb, 0, 0)
    return pl.pallas_call(
        functools.partial(_nsa_local_kernel, hn=hn, tq=tq, n_sb=n_sb),
        grid=(b, s // tq),
        in_specs=[pl.BlockSpec((1, hn, tq, LANES), lambda bb, i: (bb, 0, i, 0)),
                  pl.BlockSpec((1, 1, nc, LANES), lambda bb, i: (bb, 0, 0, 0)),
                  pl.BlockSpec((1, 1, nc, LANES), lambda bb, i: (bb, 1, 0, 0)),
                  pl.BlockSpec((LANES, nc), lambda bb, i: (0, 0)),
                  pl.BlockSpec((1, s, LANES), seq),
                  pl.BlockSpec((1, s, LANES), seq),
                  pl.BlockSpec((1, tq, LANES), lambda bb, i: (bb, i, 0))],
        out_specs=[pl.BlockSpec((1, hn, tq, LANES), lambda bb, i: (bb, 0, i, 0)),
                   pl.BlockSpec((1, tq, LANES), lambda bb, i: (bb, i, 0))],
        out_shape=[jax.ShapeDtypeStruct((b, hn, s, LANES), F32),
                   jax.ShapeDtypeStruct((b, s, LANES), BF16)],
        compiler_params=_params(2),
        name="nsa_local",
    )(q, kvc, kvc, overlap, kw, vw, small)


def _outproj_kernel(om_ref, ocw_ref, osl_ref, small_ref, of_ref, w_ref, x_ref,
                    gpost_ref, gate_ref, gpre_ref, sh_ref, sc_ref, xo_ref, h_ref,
                    *, hm, hn, hf):
    tm = x_ref.shape[1]
    n_split = 2 if tm % 32 == 0 else 1
    for r in range(n_split):
        rs = pl.ds(r * (tm // n_split), tm // n_split)
        gates = _sigmoid(small_ref[0, rs])
        parts = [om_ref[0, hh, rs] for hh in range(hm)]
        for hh in range(hn):
            o = ocw_ref[0, hh, rs] + _lane_col(gates, hh * NSA_N_BRANCH + 1) * osl_ref[0, hh, rs]
            parts.append(o.astype(BF16))
        parts += [of_ref[0, hh, rs] for hh in range(hf)]
        y = _dot(jnp.concatenate(parts, axis=-1), w_ref[0])
        xn = x_ref[0, rs] + gate_ref[0, 0, 0] * _rms(y, gpost_ref[0])
        xo_ref[0, rs] = xn
        h_ref[0, rs] = (_rms(xn, gpre_ref[0]) * (1.0 + sc_ref[0, 0, 0]) + sh_ref[0, 0, 0]).astype(BF16)


def _outproj_call(o_moba, o_cw, o_slc, small, o_fox, w_out, x, g_post, g_pre, mod6, *, l, tm):
    b, s, d = x.shape
    hm, hn, hf = o_moba.shape[1], o_cw.shape[1], o_fox.shape[1]
    head_spec = lambda n: pl.BlockSpec((1, n, tm, LANES), lambda bb, i: (bb, 0, i, 0))
    row = lambda bb, i: (bb, i, 0)
    return pl.pallas_call(
        functools.partial(_outproj_kernel, hm=hm, hn=hn, hf=hf),
        grid=(b, s // tm),
        in_specs=[head_spec(hm), head_spec(hn), head_spec(hn),
                  pl.BlockSpec((1, tm, LANES), row), head_spec(hf),
                  _resident((1,) + w_out.shape[1:], lambda bb, i: (l, 0, 0)),
                  pl.BlockSpec((1, tm, d), row), _layer_vec(l, d), _mod_vec(l, 2, d),
                  _layer_vec(l, d), _mod_vec(l, 3, d), _mod_vec(l, 4, d)],
        out_specs=[pl.BlockSpec((1, tm, d), row), pl.BlockSpec((1, tm, d), row)],
        out_shape=[jax.ShapeDtypeStruct((b, s, d), F32), jax.ShapeDtypeStruct((b, s, d), BF16)],
        compiler_params=_params(2),
        name="outproj",
    )(o_moba, o_cw, o_slc, small, o_fox, w_out, x, g_post, mod6, g_pre, mod6, mod6)


def _mlp_kernel(h_ref, wu_ref, wd_ref, x_ref, g_ref, gate_ref, o_ref, acc_ref):
    f = pl.program_id(2)

    @pl.when(f == 0)
    def _():
        acc_ref[...] = jnp.zeros_like(acc_ref)

    u = jnp.maximum(_dot(h_ref[0], wu_ref[0]), 0.0)
    acc_ref[...] += _dot((u * u).astype(BF16), wd_ref[0])

    @pl.when(f == pl.num_programs(2) - 1)
    def _():
        o_ref[0] = x_ref[0] + gate_ref[0, 0, 0] * _rms(acc_ref[...], g_ref[0])


def _mlp_call(h, w_up, w_down, x, g_post, mod6, *, l, tm, tf):
    b, s, d = x.shape
    dff = w_up.shape[2]
    row = lambda bb, i, f: (bb, i, 0)
    return pl.pallas_call(
        _mlp_kernel,
        grid=(b, s // tm, dff // tf),
        in_specs=[pl.BlockSpec((1, tm, d), row),
                  pl.BlockSpec((1, d, tf), lambda bb, i, f: (l, 0, f)),
                  pl.BlockSpec((1, tf, d), lambda bb, i, f: (l, f, 0)),
                  pl.BlockSpec((1, tm, d), row),
                  _layer_vec(l, d), _mod_vec(l, 5, d)],
        out_specs=pl.BlockSpec((1, tm, d), row),
        out_shape=jax.ShapeDtypeStruct((b, s, d), F32),
        scratch_shapes=[pltpu.VMEM((tm, d), F32)],
        compiler_params=_params(3),
        name="mlp",
    )(h, w_up, w_down, x, g_post, mod6)


def _tile(n, pref):
    t = min(n, pref)
    assert n % t == 0
    return t


def _rope_tables(positions):
    inv_freq = ROPE_THETA ** (-jnp.arange(ROPE_HALF, dtype=F32) / ROPE_HALF)
    ang = positions.astype(F32)[..., None] * inv_freq
    cos, sin = jnp.cos(ang), jnp.sin(ang)
    ones = jnp.ones(ang.shape[:-1] + (LANES - ROPE_DIMS,), F32)
    cos_t = jnp.concatenate([cos, cos, ones], axis=-1)
    sin_t = jnp.concatenate([-sin, sin, 0.0 * ones], axis=-1)
    return cos_t, sin_t


def _block_onehot(s, block, future_tiles=False):
    cols = np.arange(LANES)[None, :]
    onehot = (np.arange(s) // block)[:, None] == cols
    if future_tiles:
        assert s // block <= FUTURE_COL0
        onehot = onehot | ((np.arange(s) // LANES)[:, None] == cols - FUTURE_COL0)
    return jnp.asarray(onehot.astype(np.float32), dtype=BF16)[None]


def _overlap_matrix(s):
    nc = s // NSA_CMP_STRIDE
    n_cmp = (s - NSA_CMP_LEN) // NSA_CMP_STRIDE + 1
    c = np.arange(nc)
    first = (c * NSA_CMP_STRIDE) // NSA_SEL_BLOCK
    last = (c * NSA_CMP_STRIDE + NSA_CMP_LEN - 1) // NSA_SEL_BLOCK
    sb = np.arange(LANES)
    ov = (sb[None, :] >= first[:, None]) & (sb[None, :] <= last[:, None]) & (c[:, None] < n_cmp)
    return jnp.asarray(ov.T.astype(np.float32), dtype=BF16)


def kernel(x, c, positions, w_mod, b_mod, g_pre_mix, g_post_mix, g_pre_mlp, g_post_mlp, w_in,
           b_forget, cmp_pe_k, cmp_pe_v, cmp_w1_k, cmp_w2_k, cmp_w1_v, cmp_w2_v, w_out, w_up, w_down):
    b, s, d = x.shape
    depth = w_mod.shape[0]
    n_heads = d // HEAD_DIM
    hm = n_heads // 4
    hn = n_heads // 4
    hf = n_heads - hm - hn
    mw, nw, fw = hm * HEAD_DIM, hn * HEAD_DIM, hf * HEAD_DIM
    n_gate = hn * NSA_N_BRANCH
    o_g = 3 * mw + nw + NSA_N_KV * HEAD_DIM
    o_f = o_g + n_gate
    o_ff = o_f + 3 * fw
    assert n_gate + hf <= LANES and s % MOBA_BLOCK == 0 and s // LANES <= LANES - FUTURE_COL0

    c_pad = jnp.zeros((8, d), F32).at[:b].set(c)
    mod = _mod_call(c_pad, w_mod, b_mod)
    mod6 = mod[:, :b].reshape(depth, b, 6, 1, d).transpose(0, 2, 1, 3, 4)
    cos_t, sin_t = _rope_tables(positions)
    moba_ka = _block_onehot(s, MOBA_BLOCK, future_tiles=True)
    slc_ka = _block_onehot(s, NSA_SEL_BLOCK)
    overlap = _overlap_matrix(s)
    nb = s // MOBA_BLOCK

    w_cat = _wcat_call(jnp.swapaxes(w_in, 1, 2), o_g=o_g, o_f=o_f, o_ff=o_ff, tc=_tile(d, 256))
    w_out_b, w_up_b, w_down_b = w_out.astype(BF16), w_up.astype(BF16), w_down.astype(BF16)
    cmp_pe = jnp.stack([cmp_pe_k, cmp_pe_v], axis=1).reshape(depth, 2, 1, NSA_CMP_LEN * HEAD_DIM)
    cmp_w1 = jnp.stack([cmp_w1_k, cmp_w1_v], axis=1).astype(BF16)
    cmp_w2 = jnp.stack([cmp_w2_k, cmp_w2_v], axis=1).astype(BF16)
    bf_rows = jnp.zeros((depth, 1, LANES), F32).at[:, 0, n_gate:n_gate + hf].set(b_forget)
    gains = [g.reshape(depth, 1, d) for g in (g_pre_mix, g_post_mix, g_pre_mlp, g_post_mlp)]

    tm_in = _tile(s, 512)
    for l in range(depth):
        (mq, mk, mv, kmean, nq, cv, ks, vs, kw, vw, small, fq, fk, fv) = _inproj_call(
            x, gains[0], mod6, cos_t, sin_t, w_cat, l=l, hm=hm, hn=hn, hf=hf, tm=tm_in)

        fqa, fka = _fox_prep_call(small, bf_rows, l=l, hf=hf, col0=n_gate, ts=_tile(s, 512))
        fold = lambda t: t.reshape(b * t.shape[1], 1, s, LANES)
        flat = lambda t: t.reshape(b * t.shape[1], s, LANES)
        o_fox = _attn_call(fold(fq), flat(fqa), flat(fk), flat(fka), flat(fv),
                           tq=_tile(s, 512), tk=_tile(s, 1024), out_dtype=BF16).reshape(b, hf, s, LANES)

        kmean = kmean.reshape(b, s // tm_in, hm, tm_in // MOBA_BLOCK, LANES)
        kmean = kmean.transpose(0, 2, 1, 3, 4).reshape(b * hm, nb, LANES)
        kmean = jnp.pad(kmean, ((0, 0), (0, LANES - nb), (0, 0)))
        mqa = _moba_select_call(fold(mq), kmean, nb=nb, tq=_tile(s, 2048))
        o_moba = _attn_call(fold(mq), mqa, flat(mk), moba_ka, flat(mv), tq=_tile(s, 512),
                            tk=_tile(s, 1024), out_dtype=BF16, ka_shared=True).reshape(b, hm, s, LANES)

        kvc = _compress_call(cv, cmp_pe, cmp_w1, cmp_w2, l=l)
        o_cw, sbias = _nsa_local_call(nq, kvc, overlap, kw, vw, small, tq=_tile(s, 256))
        slc_split = 2 if hn % 2 == 0 else 1
        o_slc = _attn_call(nq.reshape(b * slc_split, hn // slc_split, s, LANES), sbias, ks, slc_ka, vs,
                           tq=_tile(s, 256), tk=_tile(s, 1024), out_dtype=BF16,
                           kv_groups=slc_split, ka_shared=True).reshape(b, hn, s, LANES)

        x, h_mlp = _outproj_call(o_moba, o_cw, o_slc, small, o_fox, w_out_b, x,
                                 gains[1], gains[2], mod6, l=l, tm=_tile(s, 512))
        x = _mlp_call(h_mlp, w_up_b, w_down_b, x, gains[3], mod6, l=l,
                      tm=_tile(s, 512), tf=1024)
    return x
```

```python
import functools

import jax
import jax.numpy as jnp
import numpy as np
from jax import lax
from jax.experimental import pallas as pl
from jax.experimental.pallas import tpu as pltpu

HEAD_DIM = 128
ROPE_THETA = 500000.0
ROPE_DIMS = HEAD_DIM // 4
ROPE_HALF = ROPE_DIMS // 2
MOBA_BLOCK = 256
MOBA_TOPK = 3
NSA_CMP_LEN = 32
NSA_CMP_STRIDE = 16
NSA_CMP_HIDDEN = 256
NSA_SEL_BLOCK = 64
NSA_N_SEL = 16
NSA_WINDOW = 512
NSA_N_BRANCH = 3
NSA_N_KV = 6
NORM_EPS = 1e-6
NEG_INF = -1e30
ATTN_SCALE = HEAD_DIM ** -0.5
LOG2E = 1.4426950408889634
Q_SCALE = ATTN_SCALE * LOG2E

LANES = 128
SUBLANES = 8
BF16_ROWS = 2 * SUBLANES
VMEM_LIMIT_BYTES = 56 * 1024 * 1024

TILE_MOD_COLS = 1024
TILE_WCAT_COLS = 256
TILE_INPROJ_ROWS = 512
TILE_FOX_PREP_ROWS = 512
TILE_MOBA_SELECT_ROWS = 2048
TILE_ATTN_Q = 512
TILE_SLC_Q = 256
TILE_ATTN_K = 1024
TILE_NSA_LOCAL_ROWS = 256
TILE_OUTPROJ_ROWS = 512
TILE_MLP_ROWS = 512
TILE_MLP_HIDDEN = 1024

F32 = jnp.float32
BF16 = jnp.bfloat16


def _params(n_axes):
    return pltpu.CompilerParams(dimension_semantics=("arbitrary",) * n_axes,
                                vmem_limit_bytes=VMEM_LIMIT_BYTES)


def _resident(block_shape, index_map):
    return pl.BlockSpec(block_shape, index_map, pipeline_mode=pl.Buffered(1))


def _split3(x):
    p1 = x.astype(BF16)
    r1 = x - p1.astype(F32)
    p2 = r1.astype(BF16)
    p3 = (r1 - p2.astype(F32)).astype(BF16)
    return p1, p2, p3


def _dot(a, b):
    return jnp.dot(a, b, preferred_element_type=F32)


def _dot_nt(a, b):
    return lax.dot_general(a, b, (((1,), (1,)), ((), ())), preferred_element_type=F32)


def _rms(x, g):
    return x * lax.rsqrt(jnp.mean(x * x, axis=-1, keepdims=True) + NORM_EPS) * g


def _sigmoid(z):
    return 1.0 / (1.0 + jnp.exp(-z))


FUTURE_COL0 = LANES // 2


def _future_tile_columns(col, tile):
    c = col - FUTURE_COL0
    qa = jnp.where(c > tile, NEG_INF, 0.0)
    ka = jnp.where(c == tile, 1.0, 0.0)
    return qa, ka


def _lane_col(x, c):
    return jnp.broadcast_to(x[:, c:c + 1], x.shape)


def _mod_kernel(c_ref, w_ref, b_ref, o_ref):
    c = c_ref[...]
    ca = c * _sigmoid(c)
    o_ref[0] = jnp.dot(ca, w_ref[0], precision=lax.Precision.HIGHEST,
                       preferred_element_type=F32) + b_ref[0]


def _mod_call(c_pad, w_mod, b_mod):
    depth, d, n = w_mod.shape
    rows = c_pad.shape[0]
    tn = _tile(n, TILE_MOD_COLS)
    return pl.pallas_call(
        _mod_kernel,
        grid=(depth, n // tn),
        in_specs=[pl.BlockSpec((rows, d), lambda l, j: (0, 0)),
                  pl.BlockSpec((1, d, tn), lambda l, j: (l, 0, j)),
                  pl.BlockSpec((1, 1, tn), lambda l, j: (l, 0, j))],
        out_specs=pl.BlockSpec((1, rows, tn), lambda l, j: (l, 0, j)),
        out_shape=jax.ShapeDtypeStruct((depth, rows, n), F32),
        compiler_params=_params(2),
        name="adaln_mod",
    )(c_pad, w_mod, b_mod.reshape(depth, 1, n))


def _wcat_kernel(w_ref, o_ref, *, o_g, o_f, o_ff):
    n_in = w_ref.shape[1]
    n_fox = o_ff - o_f
    n_small = (o_f - o_g) + (n_in - o_ff)
    tc = o_ref.shape[2]
    o_ref[0, :o_g, :] = w_ref[0, :o_g, :].astype(BF16)
    o_ref[0, o_g:o_g + n_fox, :] = w_ref[0, o_f:o_ff, :].astype(BF16)
    small = jnp.concatenate([w_ref[0, o_g:o_f, :], w_ref[0, o_ff:, :],
                             jnp.zeros((LANES - n_small, tc), F32)], axis=0)
    o_ref[0, o_g + n_fox:, :] = small.astype(BF16)


def _wcat_call(w_in_t, *, o_g, o_f, o_ff, tc):
    depth, n_in, d = w_in_t.shape
    npad = o_g + (o_ff - o_f) + LANES
    assert o_g % LANES == 0 and (o_ff - o_f) % LANES == 0
    return pl.pallas_call(
        functools.partial(_wcat_kernel, o_g=o_g, o_f=o_f, o_ff=o_ff),
        grid=(depth, d // tc),
        in_specs=[pl.BlockSpec((1, n_in, tc), lambda l, i: (l, 0, i))],
        out_specs=pl.BlockSpec((1, npad, tc), lambda l, i: (l, 0, i)),
        out_shape=jax.ShapeDtypeStruct((depth, npad, d), BF16),
        compiler_params=_params(2),
        name="w_in_layout",
    )(w_in_t)


def _inproj_kernel(x_ref, g_ref, sh_ref, sc_ref, cos_ref, sin_ref, w_ref,
                   mq_ref, mk_ref, mv_ref, kmean_ref, nq_ref, cv_ref,
                   ks_ref, vs_ref, kw_ref, vw_ref, small_ref, fq_ref, fk_ref, fv_ref,
                   *, hm, hn, hf):
    x = x_ref[0]
    h = _rms(x, g_ref[0]) * (1.0 + sc_ref[0, 0, 0]) + sh_ref[0, 0, 0]
    hb = h.astype(BF16)
    tm = x.shape[0]
    cos = cos_ref[0]
    sin = sin_ref[0]
    lane = lax.broadcasted_iota(jnp.int32, (tm, LANES), 1)

    def rope(y):
        swapped = jnp.where(lane < ROPE_HALF, pltpu.roll(y, LANES - ROPE_HALF, 1),
                            pltpu.roll(y, ROPE_HALF, 1))
        return y * cos + swapped * sin

    def heads(col0, n):
        outs = []
        j = 0
        while j < n:
            w = 2 if j + 1 < n else 1
            y = _dot_nt(hb, w_ref[0, (col0 + j) * LANES:(col0 + j + w) * LANES, :])
            for t in range(w):
                outs.append(y[:, t * LANES:(t + 1) * LANES])
            j += w
        return outs

    col = 0
    for hh, y in enumerate(heads(col, hm)):
        mq_ref[0, hh] = (rope(y) * Q_SCALE).astype(BF16)
    col += hm
    nblk = tm // MOBA_BLOCK
    means = []
    for hh, y in enumerate(heads(col, hm)):
        yr = rope(y)
        mk_ref[0, hh] = yr.astype(BF16)
        means.append(jnp.mean(yr.reshape(nblk, MOBA_BLOCK, LANES), axis=1))
    kmean_ref[0, 0] = jnp.concatenate(means, axis=0)
    col += hm
    for hh, y in enumerate(heads(col, hm)):
        mv_ref[0, hh] = y.astype(BF16)
    col += hm
    for hh, y in enumerate(heads(col, hn)):
        nq_ref[0, hh] = (rope(y) * Q_SCALE).astype(BF16)
    col += hn
    kc, vc, ks, vs, kw, vw = heads(col, NSA_N_KV)
    cv_ref[0, 0] = rope(kc)
    cv_ref[0, 1] = vc
    ks_ref[0] = rope(ks).astype(BF16)
    vs_ref[0] = vs.astype(BF16)
    kw_ref[0] = rope(kw).astype(BF16)
    vw_ref[0] = vw.astype(BF16)
    col += NSA_N_KV
    for hh, y in enumerate(heads(col, hf)):
        fq_ref[0, hh] = (y * Q_SCALE).astype(BF16)
    col += hf
    for hh, y in enumerate(heads(col, hf)):
        fk_ref[0, hh] = y.astype(BF16)
    col += hf
    for hh, y in enumerate(heads(col, hf)):
        fv_ref[0, hh] = y.astype(BF16)
    col += hf
    small_ref[0] = heads(col, 1)[0]


def _layer_vec(l, d):
    return pl.BlockSpec((1, 1, d), lambda *_: (l, 0, 0))


def _mod_vec(l, k, d):
    return pl.BlockSpec((1, 1, 1, 1, d), lambda bb, *_: (l, k, bb, 0, 0))


def _inproj_call(x, g, mod6, cos_t, sin_t, w_cat, *, l, hm, hn, hf, tm):
    b, s, d = x.shape
    npad = w_cat.shape[1]
    nblk = tm // MOBA_BLOCK
    row = lambda bb, i: (bb, i, 0)
    head_spec = lambda n: pl.BlockSpec((1, n, tm, LANES), lambda bb, i: (bb, 0, i, 0))
    head_shape = lambda n, dt=BF16: jax.ShapeDtypeStruct((b, n, s, LANES), dt)
    tok_spec = pl.BlockSpec((1, tm, LANES), row)
    tok_shape = lambda dt=BF16: jax.ShapeDtypeStruct((b, s, LANES), dt)
    out_specs = [head_spec(hm), head_spec(hm), head_spec(hm),
                 pl.BlockSpec((1, 1, hm * nblk, LANES), lambda bb, i: (bb, i, 0, 0)),
                 head_spec(hn), head_spec(2),
                 tok_spec, tok_spec, tok_spec, tok_spec, tok_spec,
                 head_spec(hf), head_spec(hf), head_spec(hf)]
    out_shape = [head_shape(hm), head_shape(hm), head_shape(hm),
                 jax.ShapeDtypeStruct((b, s // tm, hm * nblk, LANES), F32),
                 head_shape(hn), head_shape(2, F32),
                 tok_shape(), tok_shape(), tok_shape(), tok_shape(), tok_shape(F32),
                 head_shape(hf), head_shape(hf), head_shape(hf)]
    return pl.pallas_call(
        functools.partial(_inproj_kernel, hm=hm, hn=hn, hf=hf),
        grid=(b, s // tm),
        in_specs=[pl.BlockSpec((1, tm, d), row),
                  _layer_vec(l, d), _mod_vec(l, 0, d), _mod_vec(l, 1, d),
                  pl.BlockSpec((1, tm, LANES), row),
                  pl.BlockSpec((1, tm, LANES), row),
                  _resident((1, npad, d), lambda bb, i: (l, 0, 0))],
        out_specs=out_specs,
        out_shape=out_shape,
        compiler_params=_params(2),
        name="inproj",
    )(x, g, mod6, mod6, cos_t, sin_t, w_cat)


def _fox_prep_kernel(small_ref, bf_ref, qa_ref, ka_ref, carry_ref, *, hf, col0):
    i = pl.program_id(1)

    @pl.when(i == 0)
    def _():
        carry_ref[...] = jnp.zeros_like(carry_ref)

    z = small_ref[0] + bf_ref[0]
    logf = jnp.minimum(z, 0.0) - jnp.log1p(jnp.exp(-jnp.abs(z)))
    ts = z.shape[0]
    r = lax.broadcasted_iota(jnp.int32, (ts, ts), 0)
    c = lax.broadcasted_iota(jnp.int32, (ts, ts), 1)
    tri = jnp.where(r >= c, 1.0, 0.0).astype(BF16)
    p1, p2, p3 = _split3(logf)
    cum = _dot(tri, p1) + _dot(tri, p2) + _dot(tri, p3) + carry_ref[...]
    carry_ref[...] = cum[ts - 1:ts, :]
    lane = lax.broadcasted_iota(jnp.int32, (ts, LANES), 1)
    tile = (i * ts + lax.broadcasted_iota(jnp.int32, (ts, LANES), 0)) >> (LANES.bit_length() - 1)
    qa_future, ka_future = _future_tile_columns(lane, tile)
    pieces = [p.astype(F32) for p in _split3(cum * LOG2E)]
    for hh in range(hf):
        c1, c2, c3 = (_lane_col(p, col0 + hh) for p in pieces)
        qa = jnp.where(lane == 0, c1, jnp.where(lane == 1, c2, jnp.where(
            lane == 2, c3, jnp.where(lane < 6, 1.0, qa_future))))
        ka = jnp.where(lane < 3, 1.0, jnp.where(lane == 3, -c1, jnp.where(
            lane == 4, -c2, jnp.where(lane == 5, -c3, ka_future))))
        qa_ref[0, hh] = qa.astype(BF16)
        ka_ref[0, hh] = ka.astype(BF16)


def _fox_prep_call(small, bf_rows, *, l, hf, col0, ts):
    b, s, _ = small.shape
    spec = pl.BlockSpec((1, hf, ts, LANES), lambda bb, i: (bb, 0, i, 0))
    shape = jax.ShapeDtypeStruct((b, hf, s, LANES), BF16)
    return pl.pallas_call(
        functools.partial(_fox_prep_kernel, hf=hf, col0=col0),
        grid=(b, s // ts),
        in_specs=[pl.BlockSpec((1, ts, LANES), lambda bb, i: (bb, i, 0)),
                  _layer_vec(l, LANES)],
        out_specs=[spec, spec],
        out_shape=[shape, shape],
        scratch_shapes=[pltpu.VMEM((1, LANES), F32)],
        compiler_params=_params(2),
        name="fox_prep",
    )(small, bf_rows)


def _first_max_row(score, blk):
    m = jnp.max(score, axis=0, keepdims=True)
    idx = jnp.min(jnp.where(score == m, blk, float(score.shape[0])), axis=0, keepdims=True)
    return blk == idx, m


def _moba_select_kernel(q_ref, km_ref, o_ref, *, nb, top):
    i = pl.program_id(1)
    q = q_ref[0, 0]
    tq = q.shape[0]
    nbp = -(-nb // SUBLANES) * SUBLANES
    k1, k2, k3 = _split3(km_ref[0, :nbp, :])
    gate = _dot_nt(k1, q) + _dot_nt(k2, q) + _dot_nt(k3, q)
    blk = lax.broadcasted_iota(jnp.int32, (nbp, tq), 0).astype(F32)
    tok = i * tq + lax.broadcasted_iota(jnp.int32, (nbp, tq), 1)
    own = (tok >> (MOBA_BLOCK.bit_length() - 1)).astype(F32)
    past = blk < own
    score = jnp.where(blk < nb, jnp.where(past, gate, NEG_INF), -jnp.inf)
    sel = jnp.where(blk == own, 1.0, 0.0)
    for _ in range(top):
        pick, _m = _first_max_row(score, blk)
        sel = jnp.where(pick, jnp.where(past, 1.0, sel), sel)
        score = jnp.where(pick, -jnp.inf, score)
    bias = jnp.where(sel > 0.0, 0.0, jnp.where(blk < nb, NEG_INF, 0.0))
    n_fut = LANES - FUTURE_COL0
    fut_col = FUTURE_COL0 + lax.broadcasted_iota(jnp.int32, (n_fut, tq), 0)
    fut_tile = (i * tq + lax.broadcasted_iota(jnp.int32, (n_fut, tq), 1)) >> (LANES.bit_length() - 1)
    future, _ = _future_tile_columns(fut_col, fut_tile)
    bias = jnp.concatenate([bias, jnp.zeros((FUTURE_COL0 - nbp, tq), F32), future], axis=0)
    o_ref[0] = bias.T.astype(BF16)


def _moba_select_call(q, kmean_pad, *, nb, tq):
    g, _, s, _ = q.shape
    top = min(MOBA_TOPK, max(nb - 1, 1))
    return pl.pallas_call(
        functools.partial(_moba_select_kernel, nb=nb, top=top),
        grid=(g, s // tq),
        in_specs=[pl.BlockSpec((1, 1, tq, LANES), lambda gg, i: (gg, 0, i, 0)),
                  pl.BlockSpec((1, LANES, LANES), lambda gg, i: (gg, 0, 0))],
        out_specs=pl.BlockSpec((1, tq, LANES), lambda gg, i: (gg, i, 0)),
        out_shape=jax.ShapeDtypeStruct((g, s, LANES), BF16),
        compiler_params=_params(2),
        name="moba_select",
    )(q, kmean_pad)


TRIPS_PER_BODY = 8


def _attn_kernel(q_ref, qa_ref, k_ref, ka_ref, v_ref, o_ref,
                 kc_ref, vc_ref, s_ref, smax_ref, m_ref, acc_ref, *, hs, tq, tk):
    rows = hs * tq
    seq = k_ref.shape[1]
    kc_ref[:, :LANES] = k_ref[0]
    kc_ref[:, LANES:] = ka_ref[0]
    vc_ref[:, :LANES] = v_ref[0]
    vc_ref[:, LANES:] = jnp.ones((seq, LANES), BF16)
    key_minus_query = (lax.broadcasted_iota(jnp.int32, (LANES, LANES), 1)
                       - lax.broadcasted_iota(jnp.int32, (LANES, LANES), 0))

    def logits(i, j, buf):
        qs = pl.multiple_of(i * tq, tq)
        qc = jnp.concatenate([q_ref[0, :, pl.ds(qs, tq), :].reshape(rows, LANES),
                              jnp.concatenate([qa_ref[0, pl.ds(qs, tq), :]] * hs, axis=0)], axis=-1)
        s = causal(i, j, _dot_nt(qc, kc_ref[pl.ds(pl.multiple_of(j * tk, tk), tk), :]))
        s_ref[buf] = s
        smax_ref[buf] = jnp.broadcast_to(jnp.max(s, axis=-1, keepdims=True), (rows, LANES))

    def causal(i, j, s):
        delta = i * tq - j * tk
        per_tile = tq // LANES
        row_groups = []
        for rg in range(rows // LANES):
            rq = rg % per_tile
            blocks = []
            for kt in range(tk // LANES):
                blk = s[rg * LANES:(rg + 1) * LANES, kt * LANES:(kt + 1) * LANES]
                if kt >= rq and (kt - rq) % per_tile == 0:
                    blk = jnp.where(key_minus_query <= delta + (rq - kt) * LANES, blk, NEG_INF)
                blocks.append(blk)
            row_groups.append(jnp.concatenate(blocks, axis=1))
        return jnp.concatenate(row_groups, axis=0)

    def update(i, j, buf):
        s = s_ref[buf]
        m_prev = jnp.where(j == 0, -jnp.inf, m_ref[...])
        m_new = jnp.maximum(m_prev, smax_ref[buf])
        alpha = jnp.exp2(m_prev - m_new)
        p = jnp.exp2(s - jnp.tile(m_new, (1, tk // LANES))).astype(BF16)
        m_ref[...] = m_new
        acc = (jnp.tile(alpha, (1, 2)) * acc_ref[...]
               + _dot(p, vc_ref[pl.ds(pl.multiple_of(j * tk, tk), tk), :]))
        acc_ref[...] = acc
        out = (acc[:, :LANES] / acc[:, LANES:]).astype(o_ref.dtype)
        qs = pl.multiple_of(i * tq, tq)
        for hh in range(hs):
            o_ref[0, hh, pl.ds(qs, tq), :] = out[hh * tq:(hh + 1) * tq]

    def advance(i, j):
        wrap = j == (i * tq) // tk
        return jnp.where(wrap, i + 1, i), jnp.where(wrap, 0, j + 1)

    n_pairs = sum((i * tq) // tk + 1 for i in range(seq // tq))
    zero = jnp.int32(0)
    acc_ref[...] = jnp.zeros_like(acc_ref)
    m_ref[...] = jnp.zeros_like(m_ref)
    logits(zero, zero, 0)

    def trip(carry, cur, nxt):
        ib, jb, ia, ja = carry
        logits(ia, ja, nxt)
        update(ib, jb, cur)
        return (ia, ja) + advance(ia, ja)

    def trips(carry, n):
        for t in range(n):
            carry = trip(carry, t % 2, (t + 1) % 2)
        return carry

    carry = lax.fori_loop(0, (n_pairs - 1) // TRIPS_PER_BODY, lambda _, c: trips(c, TRIPS_PER_BODY),
                          (zero, zero) + advance(zero, zero))
    rest = (n_pairs - 1) % TRIPS_PER_BODY
    carry = trips(carry, rest)
    update(carry[0], carry[1], rest % 2)


def _attn_call(q, qa, k, ka, v, *, tq, tk, out_dtype, kv_groups=1, ka_shared=False):
    g, hs, s, _ = q.shape
    assert tq & (tq - 1) == 0 and tk % tq == 0 and s % tk == 0
    rows = hs * tq
    seq = lambda gg: (gg // kv_groups, 0, 0)
    ka_map = (lambda gg: (0, 0, 0)) if ka_shared else seq
    return pl.pallas_call(
        functools.partial(_attn_kernel, hs=hs, tq=tq, tk=tk),
        grid=(g,),
        in_specs=[pl.BlockSpec((1, hs, s, LANES), lambda gg: (gg, 0, 0, 0)),
                  pl.BlockSpec((1, s, LANES), seq),
                  pl.BlockSpec((1, s, LANES), seq),
                  pl.BlockSpec((1, s, LANES), ka_map),
                  pl.BlockSpec((1, s, LANES), seq)],
        out_specs=pl.BlockSpec((1, hs, s, LANES), lambda gg: (gg, 0, 0, 0)),
        out_shape=jax.ShapeDtypeStruct((g, hs, s, LANES), out_dtype),
        scratch_shapes=[pltpu.VMEM((s, 2 * LANES), BF16),
                        pltpu.VMEM((s, 2 * LANES), BF16),
                        pltpu.VMEM((2, rows, tk), F32),
                        pltpu.VMEM((2, rows, LANES), F32),
                        pltpu.VMEM((rows, LANES), F32),
                        pltpu.VMEM((rows, 2 * LANES), F32)],
        compiler_params=_params(1),
        name="flash_attn",
    )(q, qa, k, ka, v)


def _compress_kernel(x_ref, pe_ref, w1_ref, w2_ref, o_ref):
    x = x_ref[0, 0]
    half = x.shape[1]
    pe = pe_ref[0, 0]
    top = _dot((x + pe[:, :half]).astype(BF16), w1_ref[0, 0, :half, :])
    bot = _dot((x + pe[:, half:]).astype(BF16), w1_ref[0, 0, half:, :])
    n = x.shape[0]
    hid = top + pltpu.roll(bot, n - 1, 0)
    act = 0.5 * hid * (1.0 + jnp.tanh(0.7978845608028654 * (hid + 0.044715 * hid * hid * hid)))
    o_ref[0, 0] = _dot(act.astype(BF16), w2_ref[0, 0]).astype(o_ref.dtype)


def _compress_call(cv, pe, w1, w2, *, l):
    b, _, s, _ = cv.shape
    n = s // NSA_CMP_STRIDE
    width = NSA_CMP_STRIDE * LANES
    x2 = cv.reshape(b, 2, n, width)
    return pl.pallas_call(
        _compress_kernel,
        grid=(b, 2),
        in_specs=[pl.BlockSpec((1, 1, n, width), lambda bb, t: (bb, t, 0, 0)),
                  pl.BlockSpec((1, 1, 1, 2 * width), lambda bb, t: (l, t, 0, 0)),
                  pl.BlockSpec((1, 1, 2 * width, NSA_CMP_HIDDEN), lambda bb, t: (l, t, 0, 0)),
                  pl.BlockSpec((1, 1, NSA_CMP_HIDDEN, LANES), lambda bb, t: (l, t, 0, 0))],
        out_specs=pl.BlockSpec((1, 1, n, LANES), lambda bb, t: (bb, t, 0, 0)),
        out_shape=jax.ShapeDtypeStruct((b, 2, n, LANES), BF16),
        compiler_params=_params(2),
        name="nsa_compress",
    )(x2, pe, w1, w2)


def _nsa_local_kernel(q_ref, kc_ref, vc_ref, ovt_ref, kw_ref, vw_ref, small_ref,
                      ocw_ref, bias_ref, *, hn, tq, n_sb):
    i = pl.program_id(1)
    rows = hn * tq
    q = q_ref[0].reshape(rows, LANES)
    gates = _sigmoid(small_ref[0])
    nc = kc_ref.shape[2]
    s = _dot_nt(q, kc_ref[0, 0])
    r = lax.broadcasted_iota(jnp.int32, (rows, nc), 0)
    tok = i * tq + (r & (tq - 1))
    cend = lax.broadcasted_iota(jnp.int32, (rows, nc), 1) * NSA_CMP_STRIDE + (NSA_CMP_LEN - 1)
    s = jnp.where(cend <= tok, s, NEG_INF)
    e = jnp.exp2(s - jnp.max(s, axis=-1, keepdims=True))
    tok1 = i * tq + (lax.broadcasted_iota(jnp.int32, (rows, 1), 0) & (tq - 1))
    rinv = jnp.where(tok1 >= NSA_CMP_LEN - 1, 1.0 / jnp.sum(e, axis=-1, keepdims=True), 0.0)
    o_cmp = (_dot(e.astype(BF16), vc_ref[0, 0]) * rinv).reshape(hn, tq, LANES)
    psum = jnp.sum((e * rinv).reshape(hn, tq, nc), axis=0)
    hi = psum.astype(BF16)
    lo = (psum - hi.astype(F32)).astype(BF16)
    imp = _dot_nt(ovt_ref[...], hi) + _dot_nt(ovt_ref[...], lo)
    blk = lax.broadcasted_iota(jnp.int32, (LANES, tq), 0).astype(F32)
    cur = ((i * tq + lax.broadcasted_iota(jnp.int32, (LANES, tq), 1))
           >> (NSA_SEL_BLOCK.bit_length() - 1)).astype(F32)
    score = jnp.where(blk >= 1.0, jnp.where(blk <= cur - 2.0, imp, -jnp.inf), -jnp.inf)
    sel = jnp.where(blk == 0.0, 1.0, jnp.where(blk == cur, 1.0, jnp.where(blk == cur - 1.0, 1.0, 0.0)))
    for _ in range(min(NSA_N_SEL, n_sb) - 3):
        pick, m = _first_max_row(score, blk)
        sel = jnp.where(pick, jnp.where(m > -jnp.inf, 1.0, sel), sel)
        score = jnp.where(pick, -jnp.inf, score)
    bias_ref[0] = jnp.where(sel > 0.0, 0.0, NEG_INF).T.astype(BF16)
    span = NSA_WINDOW + tq
    start = pl.multiple_of(jnp.maximum(i * tq - NSA_WINDOW, 0), tq)
    sw = _dot_nt(q, kw_ref[0, pl.ds(start, span), :])
    rw = lax.broadcasted_iota(jnp.int32, (rows, span), 0)
    back = (i * tq - start) + (rw & (tq - 1)) - lax.broadcasted_iota(jnp.int32, (rows, span), 1)
    sw = jnp.where(lax.bitcast_convert_type(back, jnp.uint32) < NSA_WINDOW, sw, NEG_INF)
    ew = jnp.exp2(sw - jnp.max(sw, axis=-1, keepdims=True)).astype(BF16)
    vw1 = jnp.concatenate([vw_ref[0, pl.ds(start, span), :], jnp.ones((span, LANES), BF16)], axis=-1)
    ow = _dot(ew, vw1)
    o_win = (ow[:, :LANES] / ow[:, LANES:]).reshape(hn, tq, LANES)
    for hh in range(hn):
        ocw_ref[0, hh] = (_lane_col(gates, hh * NSA_N_BRANCH) * o_cmp[hh]
                          + _lane_col(gates, hh * NSA_N_BRANCH + 2) * o_win[hh])


def _nsa_local_call(q, kvc, overlap, kw, vw, small, *, tq):
    b, hn, s, _ = q.shape
    nc = kvc.shape[2]
    n_sb = s // NSA_SEL_BLOCK
    assert n_sb <= LANES and tq & (tq - 1) == 0 and NSA_WINDOW % tq == 0
    seq = lambda bb, i: (bb, 0, 0)
    return pl.pallas_call(
        functools.partial(_nsa_local_kernel, hn=hn, tq=tq, n_sb=n_sb),
        grid=(b, s // tq),
        in_specs=[pl.BlockSpec((1, hn, tq, LANES), lambda bb, i: (bb, 0, i, 0)),
                  pl.BlockSpec((1, 1, nc, LANES), lambda bb, i: (bb, 0, 0, 0)),
                  pl.BlockSpec((1, 1, nc, LANES), lambda bb, i: (bb, 1, 0, 0)),
                  pl.BlockSpec((LANES, nc), lambda bb, i: (0, 0)),
                  pl.BlockSpec((1, s, LANES), seq),
                  pl.BlockSpec((1, s, LANES), seq),
                  pl.BlockSpec((1, tq, LANES), lambda bb, i: (bb, i, 0))],
        out_specs=[pl.BlockSpec((1, hn, tq, LANES), lambda bb, i: (bb, 0, i, 0)),
                   pl.BlockSpec((1, tq, LANES), lambda bb, i: (bb, i, 0))],
        out_shape=[jax.ShapeDtypeStruct((b, hn, s, LANES), F32),
                   jax.ShapeDtypeStruct((b, s, LANES), BF16)],
        compiler_params=_params(2),
        name="nsa_local",
    )(q, kvc, kvc, overlap, kw, vw, small)


def _outproj_kernel(om_ref, ocw_ref, osl_ref, small_ref, of_ref, w_ref, x_ref,
                    gpost_ref, gate_ref, gpre_ref, sh_ref, sc_ref, xo_ref, h_ref,
                    *, hm, hn, hf):
    tm = x_ref.shape[1]
    n_split = 2 if tm % (2 * BF16_ROWS) == 0 else 1
    for r in range(n_split):
        rs = pl.ds(r * (tm // n_split), tm // n_split)
        gates = _sigmoid(small_ref[0, rs])
        parts = [om_ref[0, hh, rs] for hh in range(hm)]
        for hh in range(hn):
            o = ocw_ref[0, hh, rs] + _lane_col(gates, hh * NSA_N_BRANCH + 1) * osl_ref[0, hh, rs]
            parts.append(o.astype(BF16))
        parts += [of_ref[0, hh, rs] for hh in range(hf)]
        y = _dot(jnp.concatenate(parts, axis=-1), w_ref[0])
        xn = x_ref[0, rs] + gate_ref[0, 0, 0] * _rms(y, gpost_ref[0])
        xo_ref[0, rs] = xn
        h_ref[0, rs] = (_rms(xn, gpre_ref[0]) * (1.0 + sc_ref[0, 0, 0]) + sh_ref[0, 0, 0]).astype(BF16)


def _outproj_call(o_moba, o_cw, o_slc, small, o_fox, w_out, x, g_post, g_pre, mod6, *, l, tm):
    b, s, d = x.shape
    hm, hn, hf = o_moba.shape[1], o_cw.shape[1], o_fox.shape[1]
    head_spec = lambda n: pl.BlockSpec((1, n, tm, LANES), lambda bb, i: (bb, 0, i, 0))
    row = lambda bb, i: (bb, i, 0)
    return pl.pallas_call(
        functools.partial(_outproj_kernel, hm=hm, hn=hn, hf=hf),
        grid=(b, s // tm),
        in_specs=[head_spec(hm), head_spec(hn), head_spec(hn),
                  pl.BlockSpec((1, tm, LANES), row), head_spec(hf),
                  _resident((1,) + w_out.shape[1:], lambda bb, i: (l, 0, 0)),
                  pl.BlockSpec((1, tm, d), row), _layer_vec(l, d), _mod_vec(l, 2, d),
                  _layer_vec(l, d), _mod_vec(l, 3, d), _mod_vec(l, 4, d)],
        out_specs=[pl.BlockSpec((1, tm, d), row), pl.BlockSpec((1, tm, d), row)],
        out_shape=[jax.ShapeDtypeStruct((b, s, d), F32), jax.ShapeDtypeStruct((b, s, d), BF16)],
        compiler_params=_params(2),
        name="outproj",
    )(o_moba, o_cw, o_slc, small, o_fox, w_out, x, g_post, mod6, g_pre, mod6, mod6)


def _mlp_kernel(h_ref, wu_ref, wd_ref, x_ref, g_ref, gate_ref, o_ref, acc_ref):
    f = pl.program_id(2)

    @pl.when(f == 0)
    def _():
        acc_ref[...] = jnp.zeros_like(acc_ref)

    u = jnp.maximum(_dot(h_ref[0], wu_ref[0]), 0.0)
    acc_ref[...] += _dot((u * u).astype(BF16), wd_ref[0])

    @pl.when(f == pl.num_programs(2) - 1)
    def _():
        o_ref[0] = x_ref[0] + gate_ref[0, 0, 0] * _rms(acc_ref[...], g_ref[0])


def _mlp_call(h, w_up, w_down, x, g_post, mod6, *, l, tm, tf):
    b, s, d = x.shape
    dff = w_up.shape[2]
    row = lambda bb, i, f: (bb, i, 0)
    return pl.pallas_call(
        _mlp_kernel,
        grid=(b, s // tm, dff // tf),
        in_specs=[pl.BlockSpec((1, tm, d), row),
                  pl.BlockSpec((1, d, tf), lambda bb, i, f: (l, 0, f)),
                  pl.BlockSpec((1, tf, d), lambda bb, i, f: (l, f, 0)),
                  pl.BlockSpec((1, tm, d), row),
                  _layer_vec(l, d), _mod_vec(l, 5, d)],
        out_specs=pl.BlockSpec((1, tm, d), row),
        out_shape=jax.ShapeDtypeStruct((b, s, d), F32),
        scratch_shapes=[pltpu.VMEM((tm, d), F32)],
        compiler_params=_params(3),
        name="mlp",
    )(h, w_up, w_down, x, g_post, mod6)


def _tile(n, pref):
    t = min(n, pref)
    assert n % t == 0
    return t


def _rope_tables(positions):
    inv_freq = ROPE_THETA ** (-jnp.arange(ROPE_HALF, dtype=F32) / ROPE_HALF)
    ang = positions.astype(F32)[..., None] * inv_freq
    cos, sin = jnp.cos(ang), jnp.sin(ang)
    ones = jnp.ones(ang.shape[:-1] + (LANES - ROPE_DIMS,), F32)
    cos_t = jnp.concatenate([cos, cos, ones], axis=-1)
    sin_t = jnp.concatenate([-sin, sin, 0.0 * ones], axis=-1)
    return cos_t, sin_t


def _block_onehot(s, block, future_tiles=False):
    cols = np.arange(LANES)[None, :]
    onehot = (np.arange(s) // block)[:, None] == cols
    if future_tiles:
        assert s // block <= FUTURE_COL0
        onehot = onehot | ((np.arange(s) // LANES)[:, None] == cols - FUTURE_COL0)
    return jnp.asarray(onehot.astype(np.float32), dtype=BF16)[None]


def _overlap_matrix(s):
    nc = s // NSA_CMP_STRIDE
    n_cmp = (s - NSA_CMP_LEN) // NSA_CMP_STRIDE + 1
    c = np.arange(nc)
    first = (c * NSA_CMP_STRIDE) // NSA_SEL_BLOCK
    last = (c * NSA_CMP_STRIDE + NSA_CMP_LEN - 1) // NSA_SEL_BLOCK
    sb = np.arange(LANES)
    ov = (sb[None, :] >= first[:, None]) & (sb[None, :] <= last[:, None]) & (c[:, None] < n_cmp)
    return jnp.asarray(ov.T.astype(np.float32), dtype=BF16)


def kernel(x, c, positions, w_mod, b_mod, g_pre_mix, g_post_mix, g_pre_mlp, g_post_mlp, w_in,
           b_forget, cmp_pe_k, cmp_pe_v, cmp_w1_k, cmp_w2_k, cmp_w1_v, cmp_w2_v, w_out, w_up, w_down):
    b, s, d = x.shape
    depth = w_mod.shape[0]
    n_heads = d // HEAD_DIM
    hm = n_heads // 4
    hn = n_heads // 4
    hf = n_heads - hm - hn
    mw, nw, fw = hm * HEAD_DIM, hn * HEAD_DIM, hf * HEAD_DIM
    n_gate = hn * NSA_N_BRANCH
    o_g = 3 * mw + nw + NSA_N_KV * HEAD_DIM
    o_f = o_g + n_gate
    o_ff = o_f + 3 * fw
    assert n_gate + hf <= LANES and s % MOBA_BLOCK == 0 and s // LANES <= LANES - FUTURE_COL0

    c_pad = jnp.zeros((8, d), F32).at[:b].set(c)
    mod = _mod_call(c_pad, w_mod, b_mod)
    mod6 = mod[:, :b].reshape(depth, b, 6, 1, d).transpose(0, 2, 1, 3, 4)
    cos_t, sin_t = _rope_tables(positions)
    moba_ka = _block_onehot(s, MOBA_BLOCK, future_tiles=True)
    slc_ka = _block_onehot(s, NSA_SEL_BLOCK)
    overlap = _overlap_matrix(s)
    nb = s // MOBA_BLOCK

    w_cat = _wcat_call(jnp.swapaxes(w_in, 1, 2), o_g=o_g, o_f=o_f, o_ff=o_ff,
                       tc=_tile(d, TILE_WCAT_COLS))
    w_out_b, w_up_b, w_down_b = w_out.astype(BF16), w_up.astype(BF16), w_down.astype(BF16)
    cmp_pe = jnp.stack([cmp_pe_k, cmp_pe_v], axis=1).reshape(depth, 2, 1, NSA_CMP_LEN * HEAD_DIM)
    cmp_w1 = jnp.stack([cmp_w1_k, cmp_w1_v], axis=1).astype(BF16)
    cmp_w2 = jnp.stack([cmp_w2_k, cmp_w2_v], axis=1).astype(BF16)
    bf_rows = jnp.zeros((depth, 1, LANES), F32).at[:, 0, n_gate:n_gate + hf].set(b_forget)
    gains = [g.reshape(depth, 1, d) for g in (g_pre_mix, g_post_mix, g_pre_mlp, g_post_mlp)]

    tm_in = _tile(s, TILE_INPROJ_ROWS)
    tq_attn, tk_attn = _tile(s, TILE_ATTN_Q), _tile(s, TILE_ATTN_K)
    for l in range(depth):
        (mq, mk, mv, kmean, nq, cv, ks, vs, kw, vw, small, fq, fk, fv) = _inproj_call(
            x, gains[0], mod6, cos_t, sin_t, w_cat, l=l, hm=hm, hn=hn, hf=hf, tm=tm_in)

        fqa, fka = _fox_prep_call(small, bf_rows, l=l, hf=hf, col0=n_gate,
                                  ts=_tile(s, TILE_FOX_PREP_ROWS))
        fold = lambda t: t.reshape(b * t.shape[1], 1, s, LANES)
        flat = lambda t: t.reshape(b * t.shape[1], s, LANES)
        o_fox = _attn_call(fold(fq), flat(fqa), flat(fk), flat(fka), flat(fv),
                           tq=tq_attn, tk=tk_attn, out_dtype=BF16).reshape(b, hf, s, LANES)

        kmean = kmean.reshape(b, s // tm_in, hm, tm_in // MOBA_BLOCK, LANES)
        kmean = kmean.transpose(0, 2, 1, 3, 4).reshape(b * hm, nb, LANES)
        kmean = jnp.pad(kmean, ((0, 0), (0, LANES - nb), (0, 0)))
        mqa = _moba_select_call(fold(mq), kmean, nb=nb, tq=_tile(s, TILE_MOBA_SELECT_ROWS))
        o_moba = _attn_call(fold(mq), mqa, flat(mk), moba_ka, flat(mv), tq=tq_attn, tk=tk_attn,
                            out_dtype=BF16, ka_shared=True).reshape(b, hm, s, LANES)

        kvc = _compress_call(cv, cmp_pe, cmp_w1, cmp_w2, l=l)
        o_cw, sbias = _nsa_local_call(nq, kvc, overlap, kw, vw, small,
                                      tq=_tile(s, TILE_NSA_LOCAL_ROWS))
        slc_split = 2 if hn % 2 == 0 else 1
        o_slc = _attn_call(nq.reshape(b * slc_split, hn // slc_split, s, LANES), sbias, ks, slc_ka, vs,
                           tq=_tile(s, TILE_SLC_Q), tk=tk_attn, out_dtype=BF16,
                           kv_groups=slc_split, ka_shared=True).reshape(b, hn, s, LANES)

        x, h_mlp = _outproj_call(o_moba, o_cw, o_slc, small, o_fox, w_out_b, x,
                                 gains[1], gains[2], mod6, l=l, tm=_tile(s, TILE_OUTPROJ_ROWS))
        x = _mlp_call(h_mlp, w_up_b, w_down_b, x, gains[3], mod6, l=l,
                      tm=_tile(s, TILE_MLP_ROWS), tf=_tile(w_up.shape[2], TILE_MLP_HIDDEN))
    return x
```

```python
import functools

import jax
import jax.numpy as jnp
import numpy as np
from jax import lax
from jax.experimental import pallas as pl
from jax.experimental.pallas import tpu as pltpu

HEAD_DIM = 128
ROPE_THETA = 500000.0
ROPE_DIMS = HEAD_DIM // 4
ROPE_HALF = ROPE_DIMS // 2
MOBA_BLOCK = 256
MOBA_TOPK = 3
NSA_CMP_LEN = 32
NSA_CMP_STRIDE = 16
NSA_CMP_HIDDEN = 256
NSA_SEL_BLOCK = 64
NSA_N_SEL = 16
NSA_WINDOW = 512
NSA_N_BRANCH = 3
NSA_N_KV = 6
NORM_EPS = 1e-6
NEG_INF = -1e30
ATTN_SCALE = HEAD_DIM ** -0.5
LOG2E = 1.4426950408889634
Q_SCALE = ATTN_SCALE * LOG2E

LANES = 128
SUBLANES = 8
BF16_ROWS = 2 * SUBLANES
VMEM_LIMIT_BYTES = 56 * 1024 * 1024

TILE_MOD_COLS = 1024
TILE_WCAT_COLS = 256
TILE_INPROJ_ROWS = 512
TILE_FOX_PREP_ROWS = 512
TILE_MOBA_SELECT_ROWS = 2048
TILE_ATTN_Q = 512
TILE_SLC_Q = 256
TILE_ATTN_K = 1024
TILE_NSA_LOCAL_ROWS = 256
TILE_OUTPROJ_ROWS = 512
TILE_MLP_ROWS = 512
TILE_MLP_HIDDEN = 1024

F32 = jnp.float32
BF16 = jnp.bfloat16


def _params(n_axes):
    return pltpu.CompilerParams(dimension_semantics=("arbitrary",) * n_axes,
                                vmem_limit_bytes=VMEM_LIMIT_BYTES)


def _resident(block_shape, index_map):
    return pl.BlockSpec(block_shape, index_map, pipeline_mode=pl.Buffered(1))


def _split3(x):
    p1 = x.astype(BF16)
    r1 = x - p1.astype(F32)
    p2 = r1.astype(BF16)
    p3 = (r1 - p2.astype(F32)).astype(BF16)
    return p1, p2, p3


def _dot(a, b):
    return jnp.dot(a, b, preferred_element_type=F32)


def _dot_nt(a, b):
    return lax.dot_general(a, b, (((1,), (1,)), ((), ())), preferred_element_type=F32)


def _rms(x, g):
    return x * lax.rsqrt(jnp.mean(x * x, axis=-1, keepdims=True) + NORM_EPS) * g


def _sigmoid(z):
    return 1.0 / (1.0 + jnp.exp(-z))


FUTURE_COL0 = LANES // 2


def _future_tile_columns(col, tile):
    c = col - FUTURE_COL0
    qa = jnp.where(c > tile, NEG_INF, 0.0)
    ka = jnp.where(c == tile, 1.0, 0.0)
    return qa, ka


def _lane_col(x, c):
    return jnp.broadcast_to(x[:, c:c + 1], x.shape)


def _mod_kernel(c_ref, w_ref, b_ref, o_ref):
    c = c_ref[...]
    ca = c * _sigmoid(c)
    o_ref[0] = jnp.dot(ca, w_ref[0], precision=lax.Precision.HIGHEST,
                       preferred_element_type=F32) + b_ref[0]


def _mod_call(c_pad, w_mod, b_mod):
    depth, d, n = w_mod.shape
    rows = c_pad.shape[0]
    tn = _tile(n, TILE_MOD_COLS)
    return pl.pallas_call(
        _mod_kernel,
        grid=(depth, n // tn),
        in_specs=[pl.BlockSpec((rows, d), lambda l, j: (0, 0)),
                  pl.BlockSpec((1, d, tn), lambda l, j: (l, 0, j)),
                  pl.BlockSpec((1, 1, tn), lambda l, j: (l, 0, j))],
        out_specs=pl.BlockSpec((1, rows, tn), lambda l, j: (l, 0, j)),
        out_shape=jax.ShapeDtypeStruct((depth, rows, n), F32),
        compiler_params=_params(2),
        name="adaln_mod",
    )(c_pad, w_mod, b_mod.reshape(depth, 1, n))


def _wcat_kernel(w_ref, o_ref, *, o_g, o_f, o_ff):
    n_in = w_ref.shape[1]
    n_fox = o_ff - o_f
    n_small = (o_f - o_g) + (n_in - o_ff)
    tc = o_ref.shape[2]
    o_ref[0, :o_g, :] = w_ref[0, :o_g, :].astype(BF16)
    o_ref[0, o_g:o_g + n_fox, :] = w_ref[0, o_f:o_ff, :].astype(BF16)
    small = jnp.concatenate([w_ref[0, o_g:o_f, :], w_ref[0, o_ff:, :],
                             jnp.zeros((LANES - n_small, tc), F32)], axis=0)
    o_ref[0, o_g + n_fox:, :] = small.astype(BF16)


def _wcat_call(w_in_t, *, o_g, o_f, o_ff, tc):
    depth, n_in, d = w_in_t.shape
    npad = o_g + (o_ff - o_f) + LANES
    assert o_g % LANES == 0 and (o_ff - o_f) % LANES == 0
    return pl.pallas_call(
        functools.partial(_wcat_kernel, o_g=o_g, o_f=o_f, o_ff=o_ff),
        grid=(depth, d // tc),
        in_specs=[pl.BlockSpec((1, n_in, tc), lambda l, i: (l, 0, i))],
        out_specs=pl.BlockSpec((1, npad, tc), lambda l, i: (l, 0, i)),
        out_shape=jax.ShapeDtypeStruct((depth, npad, d), BF16),
        compiler_params=_params(2),
        name="w_in_layout",
    )(w_in_t)


def _inproj_kernel(x_ref, g_ref, sh_ref, sc_ref, cos_ref, sin_ref, w_ref,
                   mq_ref, mk_ref, mv_ref, kmean_ref, nq_ref, cv_ref,
                   ks_ref, vs_ref, kw_ref, vw_ref, small_ref, fq_ref, fk_ref, fv_ref,
                   *, hm, hn, hf):
    x = x_ref[0]
    h = _rms(x, g_ref[0]) * (1.0 + sc_ref[0, 0, 0]) + sh_ref[0, 0, 0]
    hb = h.astype(BF16)
    tm = x.shape[0]
    cos = cos_ref[0]
    sin = sin_ref[0]
    lane = lax.broadcasted_iota(jnp.int32, (tm, LANES), 1)

    def rope(y):
        swapped = jnp.where(lane < ROPE_HALF, pltpu.roll(y, LANES - ROPE_HALF, 1),
                            pltpu.roll(y, ROPE_HALF, 1))
        return y * cos + swapped * sin

    def heads(col0, n):
        outs = []
        j = 0
        while j < n:
            w = 2 if j + 1 < n else 1
            y = _dot_nt(hb, w_ref[0, (col0 + j) * LANES:(col0 + j + w) * LANES, :])
            for t in range(w):
                outs.append(y[:, t * LANES:(t + 1) * LANES])
            j += w
        return outs

    col = 0
    for hh, y in enumerate(heads(col, hm)):
        mq_ref[0, hh] = (rope(y) * Q_SCALE).astype(BF16)
    col += hm
    nblk = tm // MOBA_BLOCK
    means = []
    for hh, y in enumerate(heads(col, hm)):
        yr = rope(y)
        mk_ref[0, hh] = yr.astype(BF16)
        means.append(jnp.mean(yr.reshape(nblk, MOBA_BLOCK, LANES), axis=1))
    kmean_ref[0, 0] = jnp.concatenate(means, axis=0)
    col += hm
    for hh, y in enumerate(heads(col, hm)):
        mv_ref[0, hh] = y.astype(BF16)
    col += hm
    for hh, y in enumerate(heads(col, hn)):
        nq_ref[0, hh] = (rope(y) * Q_SCALE).astype(BF16)
    col += hn
    kc, vc, ks, vs, kw, vw = heads(col, NSA_N_KV)
    cv_ref[0, 0] = rope(kc)
    cv_ref[0, 1] = vc
    ks_ref[0] = rope(ks).astype(BF16)
    vs_ref[0] = vs.astype(BF16)
    kw_ref[0] = rope(kw).astype(BF16)
    vw_ref[0] = vw.astype(BF16)
    col += NSA_N_KV
    for hh, y in enumerate(heads(col, hf)):
        fq_ref[0, hh] = (y * Q_SCALE).astype(BF16)
    col += hf
    for hh, y in enumerate(heads(col, hf)):
        fk_ref[0, hh] = y.astype(BF16)
    col += hf
    for hh, y in enumerate(heads(col, hf)):
        fv_ref[0, hh] = y.astype(BF16)
    col += hf
    small_ref[0] = heads(col, 1)[0]


def _layer_vec(l, d):
    return pl.BlockSpec((1, 1, d), lambda *_: (l, 0, 0))


def _mod_vec(l, k, d):
    return pl.BlockSpec((1, 1, 1, 1, d), lambda bb, *_: (l, k, bb, 0, 0))


def _inproj_call(x, g, mod6, cos_t, sin_t, w_cat, *, l, hm, hn, hf, tm):
    b, s, d = x.shape
    npad = w_cat.shape[1]
    nblk = tm // MOBA_BLOCK
    row = lambda bb, i: (bb, i, 0)
    head_spec = lambda n: pl.BlockSpec((1, n, tm, LANES), lambda bb, i: (bb, 0, i, 0))
    head_shape = lambda n, dt=BF16: jax.ShapeDtypeStruct((b, n, s, LANES), dt)
    tok_spec = pl.BlockSpec((1, tm, LANES), row)
    tok_shape = lambda dt=BF16: jax.ShapeDtypeStruct((b, s, LANES), dt)
    out_specs = [head_spec(hm), head_spec(hm), head_spec(hm),
                 pl.BlockSpec((1, 1, hm * nblk, LANES), lambda bb, i: (bb, i, 0, 0)),
                 head_spec(hn), head_spec(2),
                 tok_spec, tok_spec, tok_spec, tok_spec, tok_spec,
                 head_spec(hf), head_spec(hf), head_spec(hf)]
    out_shape = [head_shape(hm), head_shape(hm), head_shape(hm),
                 jax.ShapeDtypeStruct((b, s // tm, hm * nblk, LANES), F32),
                 head_shape(hn), head_shape(2, F32),
                 tok_shape(), tok_shape(), tok_shape(), tok_shape(), tok_shape(F32),
                 head_shape(hf), head_shape(hf), head_shape(hf)]
    return pl.pallas_call(
        functools.partial(_inproj_kernel, hm=hm, hn=hn, hf=hf),
        grid=(b, s // tm),
        in_specs=[pl.BlockSpec((1, tm, d), row),
                  _layer_vec(l, d), _mod_vec(l, 0, d), _mod_vec(l, 1, d),
                  pl.BlockSpec((1, tm, LANES), row),
                  pl.BlockSpec((1, tm, LANES), row),
                  _resident((1, npad, d), lambda bb, i: (l, 0, 0))],
        out_specs=out_specs,
        out_shape=out_shape,
        compiler_params=_params(2),
        name="inproj",
    )(x, g, mod6, mod6, cos_t, sin_t, w_cat)


def _fox_prep_kernel(small_ref, bf_ref, qa_ref, ka_ref, carry_ref, *, hf, col0):
    i = pl.program_id(1)

    @pl.when(i == 0)
    def _():
        carry_ref[...] = jnp.zeros_like(carry_ref)

    z = small_ref[0] + bf_ref[0]
    logf = jnp.minimum(z, 0.0) - jnp.log1p(jnp.exp(-jnp.abs(z)))
    ts = z.shape[0]
    r = lax.broadcasted_iota(jnp.int32, (ts, ts), 0)
    c = lax.broadcasted_iota(jnp.int32, (ts, ts), 1)
    tri = jnp.where(r >= c, 1.0, 0.0).astype(BF16)
    p1, p2, p3 = _split3(logf)
    cum = _dot(tri, p1) + _dot(tri, p2) + _dot(tri, p3) + carry_ref[...]
    carry_ref[...] = cum[ts - 1:ts, :]
    lane = lax.broadcasted_iota(jnp.int32, (ts, LANES), 1)
    tile = (i * ts + lax.broadcasted_iota(jnp.int32, (ts, LANES), 0)) >> (LANES.bit_length() - 1)
    qa_future, ka_future = _future_tile_columns(lane, tile)
    pieces = [p.astype(F32) for p in _split3(cum * LOG2E)]
    for hh in range(hf):
        c1, c2, c3 = (_lane_col(p, col0 + hh) for p in pieces)
        qa = jnp.where(lane == 0, c1, jnp.where(lane == 1, c2, jnp.where(
            lane == 2, c3, jnp.where(lane < 6, 1.0, qa_future))))
        ka = jnp.where(lane < 3, 1.0, jnp.where(lane == 3, -c1, jnp.where(
            lane == 4, -c2, jnp.where(lane == 5, -c3, ka_future))))
        qa_ref[0, hh] = qa.astype(BF16)
        ka_ref[0, hh] = ka.astype(BF16)


def _fox_prep_call(small, bf_rows, *, l, hf, col0, ts):
    b, s, _ = small.shape
    spec = pl.BlockSpec((1, hf, ts, LANES), lambda bb, i: (bb, 0, i, 0))
    shape = jax.ShapeDtypeStruct((b, hf, s, LANES), BF16)
    return pl.pallas_call(
        functools.partial(_fox_prep_kernel, hf=hf, col0=col0),
        grid=(b, s // ts),
        in_specs=[pl.BlockSpec((1, ts, LANES), lambda bb, i: (bb, i, 0)),
                  _layer_vec(l, LANES)],
        out_specs=[spec, spec],
        out_shape=[shape, shape],
        scratch_shapes=[pltpu.VMEM((1, LANES), F32)],
        compiler_params=_params(2),
        name="fox_prep",
    )(small, bf_rows)


def _first_max_row(score, blk):
    m = jnp.max(score, axis=0, keepdims=True)
    idx = jnp.min(jnp.where(score == m, blk, float(score.shape[0])), axis=0, keepdims=True)
    return blk == idx, m


def _moba_select_kernel(q_ref, km_ref, o_ref, *, nb, top):
    i = pl.program_id(1)
    q = q_ref[0, 0]
    tq = q.shape[0]
    nbp = -(-nb // SUBLANES) * SUBLANES
    k1, k2, k3 = _split3(km_ref[0, :nbp, :])
    gate = _dot_nt(k1, q) + _dot_nt(k2, q) + _dot_nt(k3, q)
    blk = lax.broadcasted_iota(jnp.int32, (nbp, tq), 0).astype(F32)
    tok = i * tq + lax.broadcasted_iota(jnp.int32, (nbp, tq), 1)
    own = (tok >> (MOBA_BLOCK.bit_length() - 1)).astype(F32)
    past = blk < own
    score = jnp.where(blk < nb, jnp.where(past, gate, NEG_INF), -jnp.inf)
    sel = jnp.where(blk == own, 1.0, 0.0)
    for _ in range(top):
        pick, _m = _first_max_row(score, blk)
        sel = jnp.where(pick, jnp.where(past, 1.0, sel), sel)
        score = jnp.where(pick, -jnp.inf, score)
    bias = jnp.where(sel > 0.0, 0.0, jnp.where(blk < nb, NEG_INF, 0.0))
    n_fut = LANES - FUTURE_COL0
    fut_col = FUTURE_COL0 + lax.broadcasted_iota(jnp.int32, (n_fut, tq), 0)
    fut_tile = (i * tq + lax.broadcasted_iota(jnp.int32, (n_fut, tq), 1)) >> (LANES.bit_length() - 1)
    future, _ = _future_tile_columns(fut_col, fut_tile)
    bias = jnp.concatenate([bias, jnp.zeros((FUTURE_COL0 - nbp, tq), F32), future], axis=0)
    o_ref[0] = bias.T.astype(BF16)


def _moba_select_call(q, kmean_pad, *, nb, tq):
    g, _, s, _ = q.shape
    top = min(MOBA_TOPK, max(nb - 1, 1))
    return pl.pallas_call(
        functools.partial(_moba_select_kernel, nb=nb, top=top),
        grid=(g, s // tq),
        in_specs=[pl.BlockSpec((1, 1, tq, LANES), lambda gg, i: (gg, 0, i, 0)),
                  pl.BlockSpec((1, LANES, LANES), lambda gg, i: (gg, 0, 0))],
        out_specs=pl.BlockSpec((1, tq, LANES), lambda gg, i: (gg, i, 0)),
        out_shape=jax.ShapeDtypeStruct((g, s, LANES), BF16),
        compiler_params=_params(2),
        name="moba_select",
    )(q, kmean_pad)


TRIPS_PER_BODY = 8


def _attn_kernel(q_ref, qa_ref, k_ref, ka_ref, v_ref, o_ref,
                 kc_ref, vc_ref, s_ref, smax_ref, m_ref, acc_ref, *, hs, tq, tk):
    rows = hs * tq
    seq = k_ref.shape[1]
    kc_ref[:, :LANES] = k_ref[0]
    kc_ref[:, LANES:] = ka_ref[0]
    vc_ref[:, :LANES] = v_ref[0]
    vc_ref[:, LANES:] = jnp.ones((seq, LANES), BF16)
    key_minus_query = (lax.broadcasted_iota(jnp.int32, (LANES, LANES), 1)
                       - lax.broadcasted_iota(jnp.int32, (LANES, LANES), 0))

    def logits(i, j, buf):
        qs = pl.multiple_of(i * tq, tq)
        qc = jnp.concatenate([q_ref[0, :, pl.ds(qs, tq), :].reshape(rows, LANES),
                              jnp.concatenate([qa_ref[0, pl.ds(qs, tq), :]] * hs, axis=0)], axis=-1)
        s = causal(i, j, _dot_nt(qc, kc_ref[pl.ds(pl.multiple_of(j * tk, tk), tk), :]))
        s_ref[buf] = s
        smax_ref[buf] = jnp.broadcast_to(jnp.max(s, axis=-1, keepdims=True), (rows, LANES))

    def causal(i, j, s):
        delta = i * tq - j * tk
        per_tile = tq // LANES
        row_groups = []
        for rg in range(rows // LANES):
            rq = rg % per_tile
            blocks = []
            for kt in range(tk // LANES):
                blk = s[rg * LANES:(rg + 1) * LANES, kt * LANES:(kt + 1) * LANES]
                if kt >= rq and (kt - rq) % per_tile == 0:
                    blk = jnp.where(key_minus_query <= delta + (rq - kt) * LANES, blk, NEG_INF)
                blocks.append(blk)
            row_groups.append(jnp.concatenate(blocks, axis=1))
        return jnp.concatenate(row_groups, axis=0)

    def update(i, j, buf):
        s = s_ref[buf]
        m_prev = jnp.where(j == 0, -jnp.inf, m_ref[...])
        m_new = jnp.maximum(m_prev, smax_ref[buf])
        alpha = jnp.exp2(m_prev - m_new)
        p = jnp.exp2(s - jnp.tile(m_new, (1, tk // LANES))).astype(BF16)
        m_ref[...] = m_new
        acc = (jnp.tile(alpha, (1, 2)) * acc_ref[...]
               + _dot(p, vc_ref[pl.ds(pl.multiple_of(j * tk, tk), tk), :]))
        acc_ref[...] = acc
        out = (acc[:, :LANES] / acc[:, LANES:]).astype(o_ref.dtype)
        qs = pl.multiple_of(i * tq, tq)
        for hh in range(hs):
            o_ref[0, hh, pl.ds(qs, tq), :] = out[hh * tq:(hh + 1) * tq]

    def advance(i, j):
        wrap = j == (i * tq) // tk
        return jnp.where(wrap, i + 1, i), jnp.where(wrap, 0, j + 1)

    n_pairs = sum((i * tq) // tk + 1 for i in range(seq // tq))
    zero = jnp.int32(0)
    acc_ref[...] = jnp.zeros_like(acc_ref)
    m_ref[...] = jnp.zeros_like(m_ref)
    logits(zero, zero, 0)

    def trip(carry, cur, nxt):
        ib, jb, ia, ja = carry
        logits(ia, ja, nxt)
        update(ib, jb, cur)
        return (ia, ja) + advance(ia, ja)

    def trips(carry, n):
        for t in range(n):
            carry = trip(carry, t % 2, (t + 1) % 2)
        return carry

    carry = lax.fori_loop(0, (n_pairs - 1) // TRIPS_PER_BODY, lambda _, c: trips(c, TRIPS_PER_BODY),
                          (zero, zero) + advance(zero, zero))
    rest = (n_pairs - 1) % TRIPS_PER_BODY
    carry = trips(carry, rest)
    update(carry[0], carry[1], rest % 2)


def _attn_call(q, qa, k, ka, v, *, tq, tk, out_dtype, kv_groups=1, ka_shared=False):
    g, hs, s, _ = q.shape
    assert tq & (tq - 1) == 0 and tk % tq == 0 and s % tk == 0
    rows = hs * tq
    seq = lambda gg: (gg // kv_groups, 0, 0)
    ka_map = (lambda gg: (0, 0, 0)) if ka_shared else seq
    return pl.pallas_call(
        functools.partial(_attn_kernel, hs=hs, tq=tq, tk=tk),
        grid=(g,),
        in_specs=[pl.BlockSpec((1, hs, s, LANES), lambda gg: (gg, 0, 0, 0)),
                  pl.BlockSpec((1, s, LANES), seq),
                  pl.BlockSpec((1, s, LANES), seq),
                  pl.BlockSpec((1, s, LANES), ka_map),
                  pl.BlockSpec((1, s, LANES), seq)],
        out_specs=pl.BlockSpec((1, hs, s, LANES), lambda gg: (gg, 0, 0, 0)),
        out_shape=jax.ShapeDtypeStruct((g, hs, s, LANES), out_dtype),
        scratch_shapes=[pltpu.VMEM((s, 2 * LANES), BF16),
                        pltpu.VMEM((s, 2 * LANES), BF16),
                        pltpu.VMEM((2, rows, tk), F32),
                        pltpu.VMEM((2, rows, LANES), F32),
                        pltpu.VMEM((rows, LANES), F32),
                        pltpu.VMEM((rows, 2 * LANES), F32)],
        compiler_params=_params(1),
        name="flash_attn",
    )(q, qa, k, ka, v)


def _compress_kernel(x_ref, pe_ref, w1_ref, w2_ref, o_ref):
    x = x_ref[0, 0]
    half = x.shape[1]
    pe = pe_ref[0, 0]
    top = _dot((x + pe[:, :half]).astype(BF16), w1_ref[0, 0, :half, :])
    bot = _dot((x + pe[:, half:]).astype(BF16), w1_ref[0, 0, half:, :])
    n = x.shape[0]
    hid = top + pltpu.roll(bot, n - 1, 0)
    act = 0.5 * hid * (1.0 + jnp.tanh(0.7978845608028654 * (hid + 0.044715 * hid * hid * hid)))
    o_ref[0, 0] = _dot(act.astype(BF16), w2_ref[0, 0]).astype(o_ref.dtype)


def _compress_call(cv, pe, w1, w2, *, l):
    b, _, s, _ = cv.shape
    n = s // NSA_CMP_STRIDE
    width = NSA_CMP_STRIDE * LANES
    x2 = cv.reshape(b, 2, n, width)
    return pl.pallas_call(
        _compress_kernel,
        grid=(b, 2),
        in_specs=[pl.BlockSpec((1, 1, n, width), lambda bb, t: (bb, t, 0, 0)),
                  pl.BlockSpec((1, 1, 1, 2 * width), lambda bb, t: (l, t, 0, 0)),
                  pl.BlockSpec((1, 1, 2 * width, NSA_CMP_HIDDEN), lambda bb, t: (l, t, 0, 0)),
                  pl.BlockSpec((1, 1, NSA_CMP_HIDDEN, LANES), lambda bb, t: (l, t, 0, 0))],
        out_specs=pl.BlockSpec((1, 1, n, LANES), lambda bb, t: (bb, t, 0, 0)),
        out_shape=jax.ShapeDtypeStruct((b, 2, n, LANES), BF16),
        compiler_params=_params(2),
        name="nsa_compress",
    )(x2, pe, w1, w2)


def _nsa_local_kernel(q_ref, kc_ref, vc_ref, ovt_ref, kw_ref, vw_ref, small_ref,
                      ocw_ref, bias_ref, *, hn, tq, n_sb):
    i = pl.program_id(1)
    rows = hn * tq
    q = q_ref[0].reshape(rows, LANES)
    gates = _sigmoid(small_ref[0])
    nc = kc_ref.shape[2]
    s = _dot_nt(q, kc_ref[0, 0])
    r = lax.broadcasted_iota(jnp.int32, (rows, nc), 0)
    tok = i * tq + (r & (tq - 1))
    cend = lax.broadcasted_iota(jnp.int32, (rows, nc), 1) * NSA_CMP_STRIDE + (NSA_CMP_LEN - 1)
    s = jnp.where(cend <= tok, s, NEG_INF)
    e = jnp.exp2(s - jnp.max(s, axis=-1, keepdims=True))
    tok1 = i * tq + (lax.broadcasted_iota(jnp.int32, (rows, 1), 0) & (tq - 1))
    rinv = jnp.where(tok1 >= NSA_CMP_LEN - 1, 1.0 / jnp.sum(e, axis=-1, keepdims=True), 0.0)
    o_cmp = (_dot(e.astype(BF16), vc_ref[0, 0]) * rinv).reshape(hn, tq, LANES)
    psum = jnp.sum((e * rinv).reshape(hn, tq, nc), axis=0)
    hi = psum.astype(BF16)
    lo = (psum - hi.astype(F32)).astype(BF16)
    imp = _dot_nt(ovt_ref[...], hi) + _dot_nt(ovt_ref[...], lo)
    blk = lax.broadcasted_iota(jnp.int32, (LANES, tq), 0).astype(F32)
    cur = ((i * tq + lax.broadcasted_iota(jnp.int32, (LANES, tq), 1))
           >> (NSA_SEL_BLOCK.bit_length() - 1)).astype(F32)
    score = jnp.where(blk >= 1.0, jnp.where(blk <= cur - 2.0, imp, -jnp.inf), -jnp.inf)
    sel = jnp.where(blk == 0.0, 1.0, jnp.where(blk == cur, 1.0, jnp.where(blk == cur - 1.0, 1.0, 0.0)))
    for _ in range(min(NSA_N_SEL, n_sb) - 3):
        pick, m = _first_max_row(score, blk)
        sel = jnp.where(pick, jnp.where(m > -jnp.inf, 1.0, sel), sel)
        score = jnp.where(pick, -jnp.inf, score)
    bias_ref[0] = jnp.where(sel > 0.0, 0.0, NEG_INF).T.astype(BF16)
    span = NSA_WINDOW + tq
    start = pl.multiple_of(jnp.maximum(i * tq - NSA_WINDOW, 0), tq)
    sw = _dot_nt(q, kw_ref[0, pl.ds(start, span), :])
    rw = lax.broadcasted_iota(jnp.int32, (rows, span), 0)
    back = (i * tq - start) + (rw & (tq - 1)) - lax.broadcasted_iota(jnp.int32, (rows, span), 1)
    sw = jnp.where(lax.bitcast_convert_type(back, jnp.uint32) < NSA_WINDOW, sw, NEG_INF)
    ew = jnp.exp2(sw - jnp.max(sw, axis=-1, keepdims=True)).astype(BF16)
    vw1 = jnp.concatenate([vw_ref[0, pl.ds(start, span), :], jnp.ones((span, LANES), BF16)], axis=-1)
    ow = _dot(ew, vw1)
    o_win = (ow[:, :LANES] / ow[:, LANES:]).reshape(hn, tq, LANES)
    for hh in range(hn):
        ocw_ref[0, hh] = (_lane_col(gates, hh * NSA_N_BRANCH) * o_cmp[hh]
                          + _lane_col(gates, hh * NSA_N_BRANCH + 2) * o_win[hh])


def _nsa_local_call(q, kvc, overlap, kw, vw, small, *, tq):
    b, hn, s, _ = q.shape
    nc = kvc.shape[2]
    n_sb = s // NSA_SEL_BLOCK
    assert n_sb <= LANES and tq & (tq - 1) == 0 and NSA_WINDOW % tq == 0
    seq = lambda bb, i: (bb, 0, 0)
    return pl.pallas_call(
        functools.partial(_nsa_local_kernel, hn=hn, tq=tq, n_sb=n_sb),
        grid=(b, s // tq),
        in_specs=[pl.BlockSpec((1, hn, tq, LANES), lambda bb, i: (bb, 0, i, 0)),
                  pl.BlockSpec((1, 1, nc, LANES), lambda bb, i: (bb, 0, 0, 0)),
                  pl.BlockSpec((1, 1, nc, LANES), lambda bb, i: (bb, 1, 0, 0)),
                  pl.BlockSpec((LANES, nc), lambda bb, i: (0, 0)),
                  pl.BlockSpec((1, s, LANES), seq),
                  pl.BlockSpec((1, s, LANES), seq),
                  pl.BlockSpec((1, tq, LANES), lambda bb, i: (bb, i, 0))],
        out_specs=[pl.BlockSpec((1, hn, tq, LANES), lambda bb, i: (bb, 0, i, 0)),
                   pl.BlockSpec((1, tq, LANES), lambda bb, i: (bb, i, 0))],
        out_shape=[jax.ShapeDtypeStruct((b, hn, s, LANES), F32),
                   jax.ShapeDtypeStruct((b, s, LANES), BF16)],
        compiler_params=_params(2),
        name="nsa_local",
    )(q, kvc, kvc, overlap, kw, vw, small)


def _outproj_kernel(om_ref, ocw_ref, osl_ref, small_ref, of_ref, w_ref, x_ref,
                    gpost_ref, gate_ref, gpre_ref, sh_ref, sc_ref, xo_ref, h_ref,
                    *, hm, hn, hf):
    tm = x_ref.shape[1]
    n_split = 2 if tm % (2 * BF16_ROWS) == 0 else 1
    for r in range(n_split):
        rs = pl.ds(r * (tm // n_split), tm // n_split)
        gates = _sigmoid(small_ref[0, rs])
        parts = [om_ref[0, hh, rs] for hh in range(hm)]
        for hh in range(hn):
            o = ocw_ref[0, hh, rs] + _lane_col(gates, hh * NSA_N_BRANCH + 1) * osl_ref[0, hh, rs]
            parts.append(o.astype(BF16))
        parts += [of_ref[0, hh, rs] for hh in range(hf)]
        y = _dot(jnp.concatenate(parts, axis=-1), w_ref[0])
        xn = x_ref[0, rs] + gate_ref[0, 0, 0] * _rms(y, gpost_ref[0])
        xo_ref[0, rs] = xn
        h_ref[0, rs] = (_rms(xn, gpre_ref[0]) * (1.0 + sc_ref[0, 0, 0]) + sh_ref[0, 0, 0]).astype(BF16)


def _outproj_call(o_moba, o_cw, o_slc, small, o_fox, w_out, x, g_post, g_pre, mod6, *, l, tm):
    b, s, d = x.shape
    hm, hn, hf = o_moba.shape[1], o_cw.shape[1], o_fox.shape[1]
    head_spec = lambda n: pl.BlockSpec((1, n, tm, LANES), lambda bb, i: (bb, 0, i, 0))
    row = lambda bb, i: (bb, i, 0)
    return pl.pallas_call(
        functools.partial(_outproj_kernel, hm=hm, hn=hn, hf=hf),
        grid=(b, s // tm),
        in_specs=[head_spec(hm), head_spec(hn), head_spec(hn),
                  pl.BlockSpec((1, tm, LANES), row), head_spec(hf),
                  _resident((1,) + w_out.shape[1:], lambda bb, i: (l, 0, 0)),
                  pl.BlockSpec((1, tm, d), row), _layer_vec(l, d), _mod_vec(l, 2, d),
                  _layer_vec(l, d), _mod_vec(l, 3, d), _mod_vec(l, 4, d)],
        out_specs=[pl.BlockSpec((1, tm, d), row), pl.BlockSpec((1, tm, d), row)],
        out_shape=[jax.ShapeDtypeStruct((b, s, d), F32), jax.ShapeDtypeStruct((b, s, d), BF16)],
        compiler_params=_params(2),
        name="outproj",
    )(o_moba, o_cw, o_slc, small, o_fox, w_out, x, g_post, mod6, g_pre, mod6, mod6)


def _mlp_kernel(h_ref, wu_ref, wd_ref, x_ref, g_ref, gate_ref, o_ref, acc_ref):
    f = pl.program_id(2)

    @pl.when(f == 0)
    def _():
        acc_ref[...] = jnp.zeros_like(acc_ref)

    last = pl.num_programs(2) - 1
    tm = h_ref.shape[1]

    def partial_sum(rs):
        u = jnp.maximum(_dot(h_ref[0, rs], wu_ref[0]), 0.0)
        return _dot((u * u).astype(BF16), wd_ref[0])

    @pl.when(f < last)
    def _():
        acc_ref[...] += partial_sum(pl.ds(0, tm))

    @pl.when(f == last)
    def _():
        n_split = 2 if tm % (2 * BF16_ROWS) == 0 else 1
        for r in range(n_split):
            rs = pl.ds(r * (tm // n_split), tm // n_split)
            y = acc_ref[rs] + partial_sum(rs)
            o_ref[0, rs] = x_ref[0, rs] + gate_ref[0, 0, 0] * _rms(y, g_ref[0])


def _mlp_call(h, w_up, w_down, x, g_post, mod6, *, l, tm, tf):
    b, s, d = x.shape
    dff = w_up.shape[2]
    row = lambda bb, i, f: (bb, i, 0)
    return pl.pallas_call(
        _mlp_kernel,
        grid=(b, s // tm, dff // tf),
        in_specs=[pl.BlockSpec((1, tm, d), row),
                  pl.BlockSpec((1, d, tf), lambda bb, i, f: (l, 0, f)),
                  pl.BlockSpec((1, tf, d), lambda bb, i, f: (l, f, 0)),
                  pl.BlockSpec((1, tm, d), row),
                  _layer_vec(l, d), _mod_vec(l, 5, d)],
        out_specs=pl.BlockSpec((1, tm, d), row),
        out_shape=jax.ShapeDtypeStruct((b, s, d), F32),
        scratch_shapes=[pltpu.VMEM((tm, d), F32)],
        compiler_params=_params(3),
        name="mlp",
    )(h, w_up, w_down, x, g_post, mod6)


def _tile(n, pref):
    t = min(n, pref)
    assert n % t == 0
    return t


def _rope_tables(positions):
    inv_freq = ROPE_THETA ** (-jnp.arange(ROPE_HALF, dtype=F32) / ROPE_HALF)
    ang = positions.astype(F32)[..., None] * inv_freq
    cos, sin = jnp.cos(ang), jnp.sin(ang)
    ones = jnp.ones(ang.shape[:-1] + (LANES - ROPE_DIMS,), F32)
    cos_t = jnp.concatenate([cos, cos, ones], axis=-1)
    sin_t = jnp.concatenate([-sin, sin, 0.0 * ones], axis=-1)
    return cos_t, sin_t


def _block_onehot(s, block, future_tiles=False):
    cols = np.arange(LANES)[None, :]
    onehot = (np.arange(s) // block)[:, None] == cols
    if future_tiles:
        assert s // block <= FUTURE_COL0
        onehot = onehot | ((np.arange(s) // LANES)[:, None] == cols - FUTURE_COL0)
    return jnp.asarray(onehot.astype(np.float32), dtype=BF16)[None]


def _overlap_matrix(s):
    nc = s // NSA_CMP_STRIDE
    n_cmp = (s - NSA_CMP_LEN) // NSA_CMP_STRIDE + 1
    c = np.arange(nc)
    first = (c * NSA_CMP_STRIDE) // NSA_SEL_BLOCK
    last = (c * NSA_CMP_STRIDE + NSA_CMP_LEN - 1) // NSA_SEL_BLOCK
    sb = np.arange(LANES)
    ov = (sb[None, :] >= first[:, None]) & (sb[None, :] <= last[:, None]) & (c[:, None] < n_cmp)
    return jnp.asarray(ov.T.astype(np.float32), dtype=BF16)


def kernel(x, c, positions, w_mod, b_mod, g_pre_mix, g_post_mix, g_pre_mlp, g_post_mlp, w_in,
           b_forget, cmp_pe_k, cmp_pe_v, cmp_w1_k, cmp_w2_k, cmp_w1_v, cmp_w2_v, w_out, w_up, w_down):
    b, s, d = x.shape
    depth = w_mod.shape[0]
    n_heads = d // HEAD_DIM
    hm = n_heads // 4
    hn = n_heads // 4
    hf = n_heads - hm - hn
    mw, nw, fw = hm * HEAD_DIM, hn * HEAD_DIM, hf * HEAD_DIM
    n_gate = hn * NSA_N_BRANCH
    o_g = 3 * mw + nw + NSA_N_KV * HEAD_DIM
    o_f = o_g + n_gate
    o_ff = o_f + 3 * fw
    assert n_gate + hf <= LANES and s % MOBA_BLOCK == 0 and s // LANES <= LANES - FUTURE_COL0

    c_pad = jnp.zeros((8, d), F32).at[:b].set(c)
    mod = _mod_call(c_pad, w_mod, b_mod)
    mod6 = mod[:, :b].reshape(depth, b, 6, 1, d).transpose(0, 2, 1, 3, 4)
    cos_t, sin_t = _rope_tables(positions)
    moba_ka = _block_onehot(s, MOBA_BLOCK, future_tiles=True)
    slc_ka = _block_onehot(s, NSA_SEL_BLOCK)
    overlap = _overlap_matrix(s)
    nb = s // MOBA_BLOCK

    w_cat = _wcat_call(jnp.swapaxes(w_in, 1, 2), o_g=o_g, o_f=o_f, o_ff=o_ff,
                       tc=_tile(d, TILE_WCAT_COLS))
    w_out_b, w_up_b, w_down_b = w_out.astype(BF16), w_up.astype(BF16), w_down.astype(BF16)
    cmp_pe = jnp.stack([cmp_pe_k, cmp_pe_v], axis=1).reshape(depth, 2, 1, NSA_CMP_LEN * HEAD_DIM)
    cmp_w1 = jnp.stack([cmp_w1_k, cmp_w1_v], axis=1).astype(BF16)
    cmp_w2 = jnp.stack([cmp_w2_k, cmp_w2_v], axis=1).astype(BF16)
    bf_rows = jnp.zeros((depth, 1, LANES), F32).at[:, 0, n_gate:n_gate + hf].set(b_forget)
    gains = [g.reshape(depth, 1, d) for g in (g_pre_mix, g_post_mix, g_pre_mlp, g_post_mlp)]

    tm_in = _tile(s, TILE_INPROJ_ROWS)
    tq_attn, tk_attn = _tile(s, TILE_ATTN_Q), _tile(s, TILE_ATTN_K)
    for l in range(depth):
        (mq, mk, mv, kmean, nq, cv, ks, vs, kw, vw, small, fq, fk, fv) = _inproj_call(
            x, gains[0], mod6, cos_t, sin_t, w_cat, l=l, hm=hm, hn=hn, hf=hf, tm=tm_in)

        fqa, fka = _fox_prep_call(small, bf_rows, l=l, hf=hf, col0=n_gate,
                                  ts=_tile(s, TILE_FOX_PREP_ROWS))
        fold = lambda t: t.reshape(b * t.shape[1], 1, s, LANES)
        flat = lambda t: t.reshape(b * t.shape[1], s, LANES)
        o_fox = _attn_call(fold(fq), flat(fqa), flat(fk), flat(fka), flat(fv),
                           tq=tq_attn, tk=tk_attn, out_dtype=BF16).reshape(b, hf, s, LANES)

        kmean = kmean.reshape(b, s // tm_in, hm, tm_in // MOBA_BLOCK, LANES)
        kmean = kmean.transpose(0, 2, 1, 3, 4).reshape(b * hm, nb, LANES)
        kmean = jnp.pad(kmean, ((0, 0), (0, LANES - nb), (0, 0)))
        mqa = _moba_select_call(fold(mq), kmean, nb=nb, tq=_tile(s, TILE_MOBA_SELECT_ROWS))
        o_moba = _attn_call(fold(mq), mqa, flat(mk), moba_ka, flat(mv), tq=tq_attn, tk=tk_attn,
                            out_dtype=BF16, ka_shared=True).reshape(b, hm, s, LANES)

        kvc = _compress_call(cv, cmp_pe, cmp_w1, cmp_w2, l=l)
        o_cw, sbias = _nsa_local_call(nq, kvc, overlap, kw, vw, small,
                                      tq=_tile(s, TILE_NSA_LOCAL_ROWS))
        slc_split = 2 if hn % 2 == 0 else 1
        o_slc = _attn_call(nq.reshape(b * slc_split, hn // slc_split, s, LANES), sbias, ks, slc_ka, vs,
                           tq=_tile(s, TILE_SLC_Q), tk=tk_attn, out_dtype=BF16,
                           kv_groups=slc_split, ka_shared=True).reshape(b, hn, s, LANES)

        x, h_mlp = _outproj_call(o_moba, o_cw, o_slc, small, o_fox, w_out_b, x,
                                 gains[1], gains[2], mod6, l=l, tm=_tile(s, TILE_OUTPROJ_ROWS))
        x = _mlp_call(h_mlp, w_up_b, w_down_b, x, gains[3], mod6, l=l,
                      tm=_tile(s, TILE_MLP_ROWS), tf=_tile(w_up.shape[2], TILE_MLP_HIDDEN))
    return x
```

```python
import functools

import jax
import jax.numpy as jnp
import numpy as np
from jax import lax
from jax.experimental import pallas as pl
from jax.experimental.pallas import tpu as pltpu

HEAD_DIM = 128
ROPE_THETA = 500000.0
ROPE_DIMS = HEAD_DIM // 4
ROPE_HALF = ROPE_DIMS // 2
MOBA_BLOCK = 256
MOBA_TOPK = 3
NSA_CMP_LEN = 32
NSA_CMP_STRIDE = 16
NSA_CMP_HIDDEN = 256
NSA_SEL_BLOCK = 64
NSA_N_SEL = 16
NSA_WINDOW = 512
NSA_N_BRANCH = 3
NSA_N_KV = 6
NORM_EPS = 1e-6
NEG_INF = -1e30
ATTN_SCALE = HEAD_DIM ** -0.5
LOG2E = 1.4426950408889634
Q_SCALE = ATTN_SCALE * LOG2E

LANES = 128
SUBLANES = 8
BF16_ROWS = 2 * SUBLANES
VMEM_LIMIT_BYTES = 56 * 1024 * 1024

TILE_MOD_COLS = 1024
TILE_WCAT_COLS = 256
TILE_INPROJ_ROWS = 512
TILE_FOX_PREP_ROWS = 512
TILE_MOBA_SELECT_ROWS = 2048
TILE_ATTN_Q = 512
TILE_SLC_Q = 256
TILE_ATTN_K = 1024
TILE_NSA_LOCAL_ROWS = 256
TILE_OUTPROJ_ROWS = 512
TILE_MLP_ROWS = 512
TILE_MLP_HIDDEN = 1024

F32 = jnp.float32
BF16 = jnp.bfloat16


def _params(n_axes):
    return pltpu.CompilerParams(dimension_semantics=("arbitrary",) * n_axes,
                                vmem_limit_bytes=VMEM_LIMIT_BYTES)


def _resident(block_shape, index_map):
    return pl.BlockSpec(block_shape, index_map, pipeline_mode=pl.Buffered(1))


def _split3(x):
    p1 = x.astype(BF16)
    r1 = x - p1.astype(F32)
    p2 = r1.astype(BF16)
    p3 = (r1 - p2.astype(F32)).astype(BF16)
    return p1, p2, p3


def _dot(a, b):
    return jnp.dot(a, b, preferred_element_type=F32)


def _dot_nt(a, b):
    return lax.dot_general(a, b, (((1,), (1,)), ((), ())), preferred_element_type=F32)


def _rms(x, g):
    return x * lax.rsqrt(jnp.mean(x * x, axis=-1, keepdims=True) + NORM_EPS) * g


def _sigmoid(z):
    return 1.0 / (1.0 + jnp.exp(-z))


FUTURE_COL0 = LANES // 2


def _future_tile_columns(col, tile):
    c = col - FUTURE_COL0
    qa = jnp.where(c > tile, NEG_INF, 0.0)
    ka = jnp.where(c == tile, 1.0, 0.0)
    return qa, ka


def _lane_col(x, c):
    return jnp.broadcast_to(x[:, c:c + 1], x.shape)


def _mod_kernel(c_ref, w_ref, b_ref, o_ref):
    c = c_ref[...]
    rows = c.shape[0]
    pieces = [p.astype(F32) for p in _split3(c * _sigmoid(c))]
    lhs = jnp.concatenate(pieces + [jnp.zeros_like(c)], axis=0).astype(BF16)
    w = w_ref[0]
    w_hi = w.astype(BF16)
    w_lo = (w - w_hi.astype(F32)).astype(BF16)
    y = _dot(lhs, w_hi) + _dot(lhs, w_lo)
    o_ref[0] = y[:rows] + y[rows:2 * rows] + y[2 * rows:3 * rows] + b_ref[0]


def _mod_call(c_pad, w_mod, b_mod):
    depth, d, n = w_mod.shape
    rows = c_pad.shape[0]
    tn = _tile(n, TILE_MOD_COLS)
    return pl.pallas_call(
        _mod_kernel,
        grid=(depth, n // tn),
        in_specs=[pl.BlockSpec((rows, d), lambda l, j: (0, 0)),
                  pl.BlockSpec((1, d, tn), lambda l, j: (l, 0, j)),
                  pl.BlockSpec((1, 1, tn), lambda l, j: (l, 0, j))],
        out_specs=pl.BlockSpec((1, rows, tn), lambda l, j: (l, 0, j)),
        out_shape=jax.ShapeDtypeStruct((depth, rows, n), F32),
        compiler_params=_params(2),
        name="adaln_mod",
    )(c_pad, w_mod, b_mod.reshape(depth, 1, n))


def _wcat_kernel(w_ref, o_ref, *, o_g, o_f, o_ff):
    n_in = w_ref.shape[1]
    n_fox = o_ff - o_f
    n_small = (o_f - o_g) + (n_in - o_ff)
    tc = o_ref.shape[2]
    o_ref[0, :o_g, :] = w_ref[0, :o_g, :].astype(BF16)
    o_ref[0, o_g:o_g + n_fox, :] = w_ref[0, o_f:o_ff, :].astype(BF16)
    small = jnp.concatenate([w_ref[0, o_g:o_f, :], w_ref[0, o_ff:, :],
                             jnp.zeros((LANES - n_small, tc), F32)], axis=0)
    o_ref[0, o_g + n_fox:, :] = small.astype(BF16)


def _wcat_call(w_in_t, *, o_g, o_f, o_ff, tc):
    depth, n_in, d = w_in_t.shape
    npad = o_g + (o_ff - o_f) + LANES
    assert o_g % LANES == 0 and (o_ff - o_f) % LANES == 0
    return pl.pallas_call(
        functools.partial(_wcat_kernel, o_g=o_g, o_f=o_f, o_ff=o_ff),
        grid=(depth, d // tc),
        in_specs=[pl.BlockSpec((1, n_in, tc), lambda l, i: (l, 0, i))],
        out_specs=pl.BlockSpec((1, npad, tc), lambda l, i: (l, 0, i)),
        out_shape=jax.ShapeDtypeStruct((depth, npad, d), BF16),
        compiler_params=_params(2),
        name="w_in_layout",
    )(w_in_t)


def _inproj_kernel(x_ref, g_ref, sh_ref, sc_ref, cos_ref, sin_ref, w_ref,
                   mq_ref, mk_ref, mv_ref, kmean_ref, nq_ref, cv_ref,
                   ks_ref, vs_ref, kw_ref, vw_ref, small_ref, fq_ref, fk_ref, fv_ref,
                   *, hm, hn, hf):
    x = x_ref[0]
    h = _rms(x, g_ref[0]) * (1.0 + sc_ref[0, 0, 0]) + sh_ref[0, 0, 0]
    hb = h.astype(BF16)
    tm = x.shape[0]
    cos = cos_ref[0]
    sin = sin_ref[0]
    lane = lax.broadcasted_iota(jnp.int32, (tm, LANES), 1)

    def rope(y):
        swapped = jnp.where(lane < ROPE_HALF, pltpu.roll(y, LANES - ROPE_HALF, 1),
                            pltpu.roll(y, ROPE_HALF, 1))
        return y * cos + swapped * sin

    def heads(col0, n):
        outs = []
        j = 0
        while j < n:
            w = 2 if j + 1 < n else 1
            y = _dot_nt(hb, w_ref[0, (col0 + j) * LANES:(col0 + j + w) * LANES, :])
            for t in range(w):
                outs.append(y[:, t * LANES:(t + 1) * LANES])
            j += w
        return outs

    col = 0
    for hh, y in enumerate(heads(col, hm)):
        mq_ref[0, hh] = (rope(y) * Q_SCALE).astype(BF16)
    col += hm
    nblk = tm // MOBA_BLOCK
    means = []
    for hh, y in enumerate(heads(col, hm)):
        yr = rope(y)
        mk_ref[0, hh] = yr.astype(BF16)
        means.append(jnp.mean(yr.reshape(nblk, MOBA_BLOCK, LANES), axis=1))
    kmean_ref[0, 0] = jnp.concatenate(means, axis=0)
    col += hm
    for hh, y in enumerate(heads(col, hm)):
        mv_ref[0, hh] = y.astype(BF16)
    col += hm
    for hh, y in enumerate(heads(col, hn)):
        nq_ref[0, hh] = (rope(y) * Q_SCALE).astype(BF16)
    col += hn
    kc, vc, ks, vs, kw, vw = heads(col, NSA_N_KV)
    cv_ref[0, 0] = rope(kc)
    cv_ref[0, 1] = vc
    ks_ref[0] = rope(ks).astype(BF16)
    vs_ref[0] = vs.astype(BF16)
    kw_ref[0] = rope(kw).astype(BF16)
    vw_ref[0] = vw.astype(BF16)
    col += NSA_N_KV
    for hh, y in enumerate(heads(col, hf)):
        fq_ref[0, hh] = (y * Q_SCALE).astype(BF16)
    col += hf
    for hh, y in enumerate(heads(col, hf)):
        fk_ref[0, hh] = y.astype(BF16)
    col += hf
    for hh, y in enumerate(heads(col, hf)):
        fv_ref[0, hh] = y.astype(BF16)
    col += hf
    small_ref[0] = heads(col, 1)[0]


def _layer_vec(l, d):
    return pl.BlockSpec((1, 1, d), lambda *_: (l, 0, 0))


def _mod_vec(l, k, d):
    return pl.BlockSpec((1, 1, 1, 1, d), lambda bb, *_: (l, k, bb, 0, 0))


def _inproj_call(x, g, mod6, cos_t, sin_t, w_cat, *, l, hm, hn, hf, tm):
    b, s, d = x.shape
    npad = w_cat.shape[1]
    nblk = tm // MOBA_BLOCK
    row = lambda bb, i: (bb, i, 0)
    head_spec = lambda n: pl.BlockSpec((1, n, tm, LANES), lambda bb, i: (bb, 0, i, 0))
    head_shape = lambda n, dt=BF16: jax.ShapeDtypeStruct((b, n, s, LANES), dt)
    tok_spec = pl.BlockSpec((1, tm, LANES), row)
    tok_shape = lambda dt=BF16: jax.ShapeDtypeStruct((b, s, LANES), dt)
    out_specs = [head_spec(hm), head_spec(hm), head_spec(hm),
                 pl.BlockSpec((1, 1, hm * nblk, LANES), lambda bb, i: (bb, i, 0, 0)),
                 head_spec(hn), head_spec(2),
                 tok_spec, tok_spec, tok_spec, tok_spec, tok_spec,
                 head_spec(hf), head_spec(hf), head_spec(hf)]
    out_shape = [head_shape(hm), head_shape(hm), head_shape(hm),
                 jax.ShapeDtypeStruct((b, s // tm, hm * nblk, LANES), F32),
                 head_shape(hn), head_shape(2, F32),
                 tok_shape(), tok_shape(), tok_shape(), tok_shape(), tok_shape(F32),
                 head_shape(hf), head_shape(hf), head_shape(hf)]
    return pl.pallas_call(
        functools.partial(_inproj_kernel, hm=hm, hn=hn, hf=hf),
        grid=(b, s // tm),
        in_specs=[pl.BlockSpec((1, tm, d), row),
                  _layer_vec(l, d), _mod_vec(l, 0, d), _mod_vec(l, 1, d),
                  pl.BlockSpec((1, tm, LANES), row),
                  pl.BlockSpec((1, tm, LANES), row),
                  _resident((1, npad, d), lambda bb, i: (l, 0, 0))],
        out_specs=out_specs,
        out_shape=out_shape,
        compiler_params=_params(2),
        name="inproj",
    )(x, g, mod6, mod6, cos_t, sin_t, w_cat)


def _fox_prep_kernel(small_ref, bf_ref, qa_ref, ka_ref, carry_ref, *, hf, col0):
    i = pl.program_id(1)

    @pl.when(i == 0)
    def _():
        carry_ref[...] = jnp.zeros_like(carry_ref)

    z = small_ref[0] + bf_ref[0]
    logf = jnp.minimum(z, 0.0) - jnp.log1p(jnp.exp(-jnp.abs(z)))
    ts = z.shape[0]
    r = lax.broadcasted_iota(jnp.int32, (ts, ts), 0)
    c = lax.broadcasted_iota(jnp.int32, (ts, ts), 1)
    tri = jnp.where(r >= c, 1.0, 0.0).astype(BF16)
    p1, p2, p3 = _split3(logf)
    cum = _dot(tri, p1) + _dot(tri, p2) + _dot(tri, p3) + carry_ref[...]
    carry_ref[...] = cum[ts - 1:ts, :]
    lane = lax.broadcasted_iota(jnp.int32, (ts, LANES), 1)
    tile = (i * ts + lax.broadcasted_iota(jnp.int32, (ts, LANES), 0)) >> (LANES.bit_length() - 1)
    qa_future, ka_future = _future_tile_columns(lane, tile)
    pieces = [p.astype(F32) for p in _split3(cum * LOG2E)]
    for hh in range(hf):
        c1, c2, c3 = (_lane_col(p, col0 + hh) for p in pieces)
        qa = jnp.where(lane == 0, c1, jnp.where(lane == 1, c2, jnp.where(
            lane == 2, c3, jnp.where(lane < 6, 1.0, qa_future))))
        ka = jnp.where(lane < 3, 1.0, jnp.where(lane == 3, -c1, jnp.where(
            lane == 4, -c2, jnp.where(lane == 5, -c3, ka_future))))
        qa_ref[0, hh] = qa.astype(BF16)
        ka_ref[0, hh] = ka.astype(BF16)


def _fox_prep_call(small, bf_rows, *, l, hf, col0, ts):
    b, s, _ = small.shape
    spec = pl.BlockSpec((1, hf, ts, LANES), lambda bb, i: (bb, 0, i, 0))
    shape = jax.ShapeDtypeStruct((b, hf, s, LANES), BF16)
    return pl.pallas_call(
        functools.partial(_fox_prep_kernel, hf=hf, col0=col0),
        grid=(b, s // ts),
        in_specs=[pl.BlockSpec((1, ts, LANES), lambda bb, i: (bb, i, 0)),
                  _layer_vec(l, LANES)],
        out_specs=[spec, spec],
        out_shape=[shape, shape],
        scratch_shapes=[pltpu.VMEM((1, LANES), F32)],
        compiler_params=_params(2),
        name="fox_prep",
    )(small, bf_rows)


def _first_max_row(score, blk):
    m = jnp.max(score, axis=0, keepdims=True)
    idx = jnp.min(jnp.where(score == m, blk, float(score.shape[0])), axis=0, keepdims=True)
    return blk == idx, m


def _moba_select_kernel(q_ref, km_ref, o_ref, *, nb, top):
    i = pl.program_id(1)
    q = q_ref[0, 0]
    tq = q.shape[0]
    nbp = -(-nb // SUBLANES) * SUBLANES
    k1, k2, k3 = _split3(km_ref[0, :nbp, :])
    gate = _dot_nt(k1, q) + _dot_nt(k2, q) + _dot_nt(k3, q)
    blk = lax.broadcasted_iota(jnp.int32, (nbp, tq), 0).astype(F32)
    tok = i * tq + lax.broadcasted_iota(jnp.int32, (nbp, tq), 1)
    own = (tok >> (MOBA_BLOCK.bit_length() - 1)).astype(F32)
    past = blk < own
    score = jnp.where(blk < nb, jnp.where(past, gate, NEG_INF), -jnp.inf)
    sel = jnp.where(blk == own, 1.0, 0.0)
    for _ in range(top):
        pick, _m = _first_max_row(score, blk)
        sel = jnp.where(pick, jnp.where(past, 1.0, sel), sel)
        score = jnp.where(pick, -jnp.inf, score)
    bias = jnp.where(sel > 0.0, 0.0, jnp.where(blk < nb, NEG_INF, 0.0))
    n_fut = LANES - FUTURE_COL0
    fut_col = FUTURE_COL0 + lax.broadcasted_iota(jnp.int32, (n_fut, tq), 0)
    fut_tile = (i * tq + lax.broadcasted_iota(jnp.int32, (n_fut, tq), 1)) >> (LANES.bit_length() - 1)
    future, _ = _future_tile_columns(fut_col, fut_tile)
    bias = jnp.concatenate([bias, jnp.zeros((FUTURE_COL0 - nbp, tq), F32), future], axis=0)
    o_ref[0] = bias.T.astype(BF16)


def _moba_select_call(q, kmean_pad, *, nb, tq):
    g, _, s, _ = q.shape
    top = min(MOBA_TOPK, max(nb - 1, 1))
    return pl.pallas_call(
        functools.partial(_moba_select_kernel, nb=nb, top=top),
        grid=(g, s // tq),
        in_specs=[pl.BlockSpec((1, 1, tq, LANES), lambda gg, i: (gg, 0, i, 0)),
                  pl.BlockSpec((1, LANES, LANES), lambda gg, i: (gg, 0, 0))],
        out_specs=pl.BlockSpec((1, tq, LANES), lambda gg, i: (gg, i, 0)),
        out_shape=jax.ShapeDtypeStruct((g, s, LANES), BF16),
        compiler_params=_params(2),
        name="moba_select",
    )(q, kmean_pad)


TRIPS_PER_BODY = 8


def _attn_kernel(q_ref, qa_ref, k_ref, ka_ref, v_ref, o_ref,
                 kc_ref, vc_ref, s_ref, smax_ref, m_ref, acc_ref, *, hs, tq, tk):
    rows = hs * tq
    seq = k_ref.shape[1]
    kc_ref[:, :LANES] = k_ref[0]
    kc_ref[:, LANES:] = ka_ref[0]
    vc_ref[:, :LANES] = v_ref[0]
    vc_ref[:, LANES:] = jnp.ones((seq, LANES), BF16)
    key_minus_query = (lax.broadcasted_iota(jnp.int32, (LANES, LANES), 1)
                       - lax.broadcasted_iota(jnp.int32, (LANES, LANES), 0))

    def logits(i, j, buf):
        qs = pl.multiple_of(i * tq, tq)
        qc = jnp.concatenate([q_ref[0, :, pl.ds(qs, tq), :].reshape(rows, LANES),
                              jnp.concatenate([qa_ref[0, pl.ds(qs, tq), :]] * hs, axis=0)], axis=-1)
        s = causal(i, j, _dot_nt(qc, kc_ref[pl.ds(pl.multiple_of(j * tk, tk), tk), :]))
        s_ref[buf] = s
        smax_ref[buf] = jnp.broadcast_to(jnp.max(s, axis=-1, keepdims=True), (rows, LANES))

    def causal(i, j, s):
        delta = i * tq - j * tk
        per_tile = tq // LANES
        row_groups = []
        for rg in range(rows // LANES):
            rq = rg % per_tile
            blocks = []
            for kt in range(tk // LANES):
                blk = s[rg * LANES:(rg + 1) * LANES, kt * LANES:(kt + 1) * LANES]
                if kt >= rq and (kt - rq) % per_tile == 0:
                    blk = jnp.where(key_minus_query <= delta + (rq - kt) * LANES, blk, NEG_INF)
                blocks.append(blk)
            row_groups.append(jnp.concatenate(blocks, axis=1))
        return jnp.concatenate(row_groups, axis=0)

    def update(i, j, buf):
        s = s_ref[buf]
        m_prev = jnp.where(j == 0, -jnp.inf, m_ref[...])
        m_new = jnp.maximum(m_prev, smax_ref[buf])
        alpha = jnp.exp2(m_prev - m_new)
        p = jnp.exp2(s - jnp.tile(m_new, (1, tk // LANES))).astype(BF16)
        m_ref[...] = m_new
        acc = (jnp.tile(alpha, (1, 2)) * acc_ref[...]
               + _dot(p, vc_ref[pl.ds(pl.multiple_of(j * tk, tk), tk), :]))
        acc_ref[...] = acc
        out = (acc[:, :LANES] / acc[:, LANES:]).astype(o_ref.dtype)
        qs = pl.multiple_of(i * tq, tq)
        for hh in range(hs):
            o_ref[0, hh, pl.ds(qs, tq), :] = out[hh * tq:(hh + 1) * tq]

    def advance(i, j):
        wrap = j == (i * tq) // tk
        return jnp.where(wrap, i + 1, i), jnp.where(wrap, 0, j + 1)

    n_pairs = sum((i * tq) // tk + 1 for i in range(seq // tq))
    zero = jnp.int32(0)
    acc_ref[...] = jnp.zeros_like(acc_ref)
    m_ref[...] = jnp.zeros_like(m_ref)
    logits(zero, zero, 0)

    def trip(carry, cur, nxt):
        ib, jb, ia, ja = carry
        logits(ia, ja, nxt)
        update(ib, jb, cur)
        return (ia, ja) + advance(ia, ja)

    def trips(carry, n):
        for t in range(n):
            carry = trip(carry, t % 2, (t + 1) % 2)
        return carry

    carry = lax.fori_loop(0, (n_pairs - 1) // TRIPS_PER_BODY, lambda _, c: trips(c, TRIPS_PER_BODY),
                          (zero, zero) + advance(zero, zero))
    rest = (n_pairs - 1) % TRIPS_PER_BODY
    carry = trips(carry, rest)
    update(carry[0], carry[1], rest % 2)


def _attn_call(q, qa, k, ka, v, *, tq, tk, out_dtype, kv_groups=1, ka_shared=False):
    g, hs, s, _ = q.shape
    assert tq & (tq - 1) == 0 and tk % tq == 0 and s % tk == 0
    rows = hs * tq
    seq = lambda gg: (gg // kv_groups, 0, 0)
    ka_map = (lambda gg: (0, 0, 0)) if ka_shared else seq
    return pl.pallas_call(
        functools.partial(_attn_kernel, hs=hs, tq=tq, tk=tk),
        grid=(g,),
        in_specs=[pl.BlockSpec((1, hs, s, LANES), lambda gg: (gg, 0, 0, 0)),
                  pl.BlockSpec((1, s, LANES), seq),
                  pl.BlockSpec((1, s, LANES), seq),
                  pl.BlockSpec((1, s, LANES), ka_map),
                  pl.BlockSpec((1, s, LANES), seq)],
        out_specs=pl.BlockSpec((1, hs, s, LANES), lambda gg: (gg, 0, 0, 0)),
        out_shape=jax.ShapeDtypeStruct((g, hs, s, LANES), out_dtype),
        scratch_shapes=[pltpu.VMEM((s, 2 * LANES), BF16),
                        pltpu.VMEM((s, 2 * LANES), BF16),
                        pltpu.VMEM((2, rows, tk), F32),
                        pltpu.VMEM((2, rows, LANES), F32),
                        pltpu.VMEM((rows, LANES), F32),
                        pltpu.VMEM((rows, 2 * LANES), F32)],
        compiler_params=_params(1),
        name="flash_attn",
    )(q, qa, k, ka, v)


def _compress_kernel(x_ref, pe_ref, w1_ref, w2_ref, o_ref):
    x = x_ref[0, 0]
    half = x.shape[1]
    pe = pe_ref[0, 0]
    top = _dot((x + pe[:, :half]).astype(BF16), w1_ref[0, 0, :half, :])
    bot = _dot((x + pe[:, half:]).astype(BF16), w1_ref[0, 0, half:, :])
    n = x.shape[0]
    hid = top + pltpu.roll(bot, n - 1, 0)
    act = 0.5 * hid * (1.0 + jnp.tanh(0.7978845608028654 * (hid + 0.044715 * hid * hid * hid)))
    o_ref[0, 0] = _dot(act.astype(BF16), w2_ref[0, 0]).astype(o_ref.dtype)


def _compress_call(cv, pe, w1, w2, *, l):
    b, _, s, _ = cv.shape
    n = s // NSA_CMP_STRIDE
    width = NSA_CMP_STRIDE * LANES
    x2 = cv.reshape(b, 2, n, width)
    return pl.pallas_call(
        _compress_kernel,
        grid=(b, 2),
        in_specs=[pl.BlockSpec((1, 1, n, width), lambda bb, t: (bb, t, 0, 0)),
                  pl.BlockSpec((1, 1, 1, 2 * width), lambda bb, t: (l, t, 0, 0)),
                  pl.BlockSpec((1, 1, 2 * width, NSA_CMP_HIDDEN), lambda bb, t: (l, t, 0, 0)),
                  pl.BlockSpec((1, 1, NSA_CMP_HIDDEN, LANES), lambda bb, t: (l, t, 0, 0))],
        out_specs=pl.BlockSpec((1, 1, n, LANES), lambda bb, t: (bb, t, 0, 0)),
        out_shape=jax.ShapeDtypeStruct((b, 2, n, LANES), BF16),
        compiler_params=_params(2),
        name="nsa_compress",
    )(x2, pe, w1, w2)


def _nsa_local_kernel(q_ref, kc_ref, vc_ref, ovt_ref, kw_ref, vw_ref, small_ref,
                      ocw_ref, bias_ref, *, hn, tq, n_sb):
    i = pl.program_id(1)
    rows = hn * tq
    q = q_ref[0].reshape(rows, LANES)
    gates = _sigmoid(small_ref[0])
    nc = kc_ref.shape[2]
    s = _dot_nt(q, kc_ref[0, 0])
    r = lax.broadcasted_iota(jnp.int32, (rows, nc), 0)
    tok = i * tq + (r & (tq - 1))
    cend = lax.broadcasted_iota(jnp.int32, (rows, nc), 1) * NSA_CMP_STRIDE + (NSA_CMP_LEN - 1)
    s = jnp.where(cend <= tok, s, NEG_INF)
    e = jnp.exp2(s - jnp.max(s, axis=-1, keepdims=True))
    tok1 = i * tq + (lax.broadcasted_iota(jnp.int32, (rows, 1), 0) & (tq - 1))
    rinv = jnp.where(tok1 >= NSA_CMP_LEN - 1, 1.0 / jnp.sum(e, axis=-1, keepdims=True), 0.0)
    o_cmp = (_dot(e.astype(BF16), vc_ref[0, 0]) * rinv).reshape(hn, tq, LANES)
    psum = jnp.sum((e * rinv).reshape(hn, tq, nc), axis=0)
    hi = psum.astype(BF16)
    lo = (psum - hi.astype(F32)).astype(BF16)
    imp = _dot_nt(ovt_ref[...], hi) + _dot_nt(ovt_ref[...], lo)
    blk = lax.broadcasted_iota(jnp.int32, (LANES, tq), 0).astype(F32)
    cur = ((i * tq + lax.broadcasted_iota(jnp.int32, (LANES, tq), 1))
           >> (NSA_SEL_BLOCK.bit_length() - 1)).astype(F32)
    score = jnp.where(blk >= 1.0, jnp.where(blk <= cur - 2.0, imp, -jnp.inf), -jnp.inf)
    sel = jnp.where(blk == 0.0, 1.0, jnp.where(blk == cur, 1.0, jnp.where(blk == cur - 1.0, 1.0, 0.0)))
    for _ in range(min(NSA_N_SEL, n_sb) - 3):
        pick, m = _first_max_row(score, blk)
        sel = jnp.where(pick, jnp.where(m > -jnp.inf, 1.0, sel), sel)
        score = jnp.where(pick, -jnp.inf, score)
    bias_ref[0] = jnp.where(sel > 0.0, 0.0, NEG_INF).T.astype(BF16)
    span = NSA_WINDOW + tq
    start = pl.multiple_of(jnp.maximum(i * tq - NSA_WINDOW, 0), tq)
    sw = _dot_nt(q, kw_ref[0, pl.ds(start, span), :])
    rw = lax.broadcasted_iota(jnp.int32, (rows, span), 0)
    back = (i * tq - start) + (rw & (tq - 1)) - lax.broadcasted_iota(jnp.int32, (rows, span), 1)
    sw = jnp.where(lax.bitcast_convert_type(back, jnp.uint32) < NSA_WINDOW, sw, NEG_INF)
    ew = jnp.exp2(sw - jnp.max(sw, axis=-1, keepdims=True)).astype(BF16)
    vw1 = jnp.concatenate([vw_ref[0, pl.ds(start, span), :], jnp.ones((span, LANES), BF16)], axis=-1)
    ow = _dot(ew, vw1)
    o_win = (ow[:, :LANES] / ow[:, LANES:]).reshape(hn, tq, LANES)
    for hh in range(hn):
        ocw_ref[0, hh] = (_lane_col(gates, hh * NSA_N_BRANCH) * o_cmp[hh]
                          + _lane_col(gates, hh * NSA_N_BRANCH + 2) * o_win[hh])


def _nsa_local_call(q, kvc, overlap, kw, vw, small, *, tq):
    b, hn, s, _ = q.shape
    nc = kvc.shape[2]
    n_sb = s // NSA_SEL_BLOCK
    assert n_sb <= LANES and tq & (tq - 1) == 0 and NSA_WINDOW % tq == 0
    seq = lambda bb, i: (bb, 0, 0)
    return pl.pallas_call(
        functools.partial(_nsa_local_kernel, hn=hn, tq=tq, n_sb=n_sb),
        grid=(b, s // tq),
        in_specs=[pl.BlockSpec((1, hn, tq, LANES), lambda bb, i: (bb, 0, i, 0)),
                  pl.BlockSpec((1, 1, nc, LANES), lambda bb, i: (bb, 0, 0, 0)),
                  pl.BlockSpec((1, 1, nc, LANES), lambda bb, i: (bb, 1, 0, 0)),
                  pl.BlockSpec((LANES, nc), lambda bb, i: (0, 0)),
                  pl.BlockSpec((1, s, LANES), seq),
                  pl.BlockSpec((1, s, LANES), seq),
                  pl.BlockSpec((1, tq, LANES), lambda bb, i: (bb, i, 0))],
        out_specs=[pl.BlockSpec((1, hn, tq, LANES), lambda bb, i: (bb, 0, i, 0)),
                   pl.BlockSpec((1, tq, LANES), lambda bb, i: (bb, i, 0))],
        out_shape=[jax.ShapeDtypeStruct((b, hn, s, LANES), F32),
                   jax.ShapeDtypeStruct((b, s, LANES), BF16)],
        compiler_params=_params(2),
        name="nsa_local",
    )(q, kvc, kvc, overlap, kw, vw, small)


def _outproj_kernel(om_ref, ocw_ref, osl_ref, small_ref, of_ref, w_ref, x_ref,
                    gpost_ref, gate_ref, gpre_ref, sh_ref, sc_ref, xo_ref, h_ref,
                    *, hm, hn, hf):
    tm = x_ref.shape[1]
    n_split = 2 if tm % (2 * BF16_ROWS) == 0 else 1
    for r in range(n_split):
        rs = pl.ds(r * (tm // n_split), tm // n_split)
        gates = _sigmoid(small_ref[0, rs])
        parts = [om_ref[0, hh, rs] for hh in range(hm)]
        for hh in range(hn):
            o = ocw_ref[0, hh, rs] + _lane_col(gates, hh * NSA_N_BRANCH + 1) * osl_ref[0, hh, rs]
            parts.append(o.astype(BF16))
        parts += [of_ref[0, hh, rs] for hh in range(hf)]
        y = _dot(jnp.concatenate(parts, axis=-1), w_ref[0])
        xn = x_ref[0, rs] + gate_ref[0, 0, 0] * _rms(y, gpost_ref[0])
        xo_ref[0, rs] = xn
        h_ref[0, rs] = (_rms(xn, gpre_ref[0]) * (1.0 + sc_ref[0, 0, 0]) + sh_ref[0, 0, 0]).astype(BF16)


def _outproj_call(o_moba, o_cw, o_slc, small, o_fox, w_out, x, g_post, g_pre, mod6, *, l, tm):
    b, s, d = x.shape
    hm, hn, hf = o_moba.shape[1], o_cw.shape[1], o_fox.shape[1]
    head_spec = lambda n: pl.BlockSpec((1, n, tm, LANES), lambda bb, i: (bb, 0, i, 0))
    row = lambda bb, i: (bb, i, 0)
    return pl.pallas_call(
        functools.partial(_outproj_kernel, hm=hm, hn=hn, hf=hf),
        grid=(b, s // tm),
        in_specs=[head_spec(hm), head_spec(hn), head_spec(hn),
                  pl.BlockSpec((1, tm, LANES), row), head_spec(hf),
                  _resident((1,) + w_out.shape[1:], lambda bb, i: (l, 0, 0)),
                  pl.BlockSpec((1, tm, d), row), _layer_vec(l, d), _mod_vec(l, 2, d),
                  _layer_vec(l, d), _mod_vec(l, 3, d), _mod_vec(l, 4, d)],
        out_specs=[pl.BlockSpec((1, tm, d), row), pl.BlockSpec((1, tm, d), row)],
        out_shape=[jax.ShapeDtypeStruct((b, s, d), F32), jax.ShapeDtypeStruct((b, s, d), BF16)],
        compiler_params=_params(2),
        name="outproj",
    )(o_moba, o_cw, o_slc, small, o_fox, w_out, x, g_post, mod6, g_pre, mod6, mod6)


def _mlp_kernel(h_ref, wu_ref, wd_ref, x_ref, g_ref, gate_ref, o_ref, acc_ref):
    f = pl.program_id(2)

    @pl.when(f == 0)
    def _():
        acc_ref[...] = jnp.zeros_like(acc_ref)

    last = pl.num_programs(2) - 1
    tm = h_ref.shape[1]

    def partial_sum(rs):
        u = jnp.maximum(_dot(h_ref[0, rs], wu_ref[0]), 0.0)
        return _dot((u * u).astype(BF16), wd_ref[0])

    @pl.when(f < last)
    def _():
        acc_ref[...] += partial_sum(pl.ds(0, tm))

    @pl.when(f == last)
    def _():
        n_split = 2 if tm % (2 * BF16_ROWS) == 0 else 1
        for r in range(n_split):
            rs = pl.ds(r * (tm // n_split), tm // n_split)
            y = acc_ref[rs] + partial_sum(rs)
            o_ref[0, rs] = x_ref[0, rs] + gate_ref[0, 0, 0] * _rms(y, g_ref[0])


def _mlp_call(h, w_up, w_down, x, g_post, mod6, *, l, tm, tf):
    b, s, d = x.shape
    dff = w_up.shape[2]
    row = lambda bb, i, f: (bb, i, 0)
    return pl.pallas_call(
        _mlp_kernel,
        grid=(b, s // tm, dff // tf),
        in_specs=[pl.BlockSpec((1, tm, d), row),
                  pl.BlockSpec((1, d, tf), lambda bb, i, f: (l, 0, f)),
                  pl.BlockSpec((1, tf, d), lambda bb, i, f: (l, f, 0)),
                  pl.BlockSpec((1, tm, d), row),
                  _layer_vec(l, d), _mod_vec(l, 5, d)],
        out_specs=pl.BlockSpec((1, tm, d), row),
        out_shape=jax.ShapeDtypeStruct((b, s, d), F32),
        scratch_shapes=[pltpu.VMEM((tm, d), F32)],
        compiler_params=_params(3),
        name="mlp",
    )(h, w_up, w_down, x, g_post, mod6)


def _tile(n, pref):
    t = min(n, pref)
    assert n % t == 0
    return t


def _rope_tables(positions):
    inv_freq = ROPE_THETA ** (-jnp.arange(ROPE_HALF, dtype=F32) / ROPE_HALF)
    ang = positions.astype(F32)[..., None] * inv_freq
    cos, sin = jnp.cos(ang), jnp.sin(ang)
    ones = jnp.ones(ang.shape[:-1] + (LANES - ROPE_DIMS,), F32)
    cos_t = jnp.concatenate([cos, cos, ones], axis=-1)
    sin_t = jnp.concatenate([-sin, sin, 0.0 * ones], axis=-1)
    return cos_t, sin_t


def _block_onehot(s, block, future_tiles=False):
    cols = np.arange(LANES)[None, :]
    onehot = (np.arange(s) // block)[:, None] == cols
    if future_tiles:
        assert s // block <= FUTURE_COL0
        onehot = onehot | ((np.arange(s) // LANES)[:, None] == cols - FUTURE_COL0)
    return jnp.asarray(onehot.astype(np.float32), dtype=BF16)[None]


def _overlap_matrix(s):
    nc = s // NSA_CMP_STRIDE
    n_cmp = (s - NSA_CMP_LEN) // NSA_CMP_STRIDE + 1
    c = np.arange(nc)
    first = (c * NSA_CMP_STRIDE) // NSA_SEL_BLOCK
    last = (c * NSA_CMP_STRIDE + NSA_CMP_LEN - 1) // NSA_SEL_BLOCK
    sb = np.arange(LANES)
    ov = (sb[None, :] >= first[:, None]) & (sb[None, :] <= last[:, None]) & (c[:, None] < n_cmp)
    return jnp.asarray(ov.T.astype(np.float32), dtype=BF16)


def kernel(x, c, positions, w_mod, b_mod, g_pre_mix, g_post_mix, g_pre_mlp, g_post_mlp, w_in,
           b_forget, cmp_pe_k, cmp_pe_v, cmp_w1_k, cmp_w2_k, cmp_w1_v, cmp_w2_v, w_out, w_up, w_down):
    b, s, d = x.shape
    depth = w_mod.shape[0]
    n_heads = d // HEAD_DIM
    hm = n_heads // 4
    hn = n_heads // 4
    hf = n_heads - hm - hn
    mw, nw, fw = hm * HEAD_DIM, hn * HEAD_DIM, hf * HEAD_DIM
    n_gate = hn * NSA_N_BRANCH
    o_g = 3 * mw + nw + NSA_N_KV * HEAD_DIM
    o_f = o_g + n_gate
    o_ff = o_f + 3 * fw
    assert n_gate + hf <= LANES and s % MOBA_BLOCK == 0 and s // LANES <= LANES - FUTURE_COL0

    c_pad = jnp.zeros((8, d), F32).at[:b].set(c)
    mod = _mod_call(c_pad, w_mod, b_mod)
    mod6 = mod[:, :b].reshape(depth, b, 6, 1, d).transpose(0, 2, 1, 3, 4)
    cos_t, sin_t = _rope_tables(positions)
    moba_ka = _block_onehot(s, MOBA_BLOCK, future_tiles=True)
    slc_ka = _block_onehot(s, NSA_SEL_BLOCK)
    overlap = _overlap_matrix(s)
    nb = s // MOBA_BLOCK

    w_cat = _wcat_call(jnp.swapaxes(w_in, 1, 2), o_g=o_g, o_f=o_f, o_ff=o_ff,
                       tc=_tile(d, TILE_WCAT_COLS))
    w_out_b, w_up_b, w_down_b = w_out.astype(BF16), w_up.astype(BF16), w_down.astype(BF16)
    cmp_pe = jnp.stack([cmp_pe_k, cmp_pe_v], axis=1).reshape(depth, 2, 1, NSA_CMP_LEN * HEAD_DIM)
    cmp_w1 = jnp.stack([cmp_w1_k, cmp_w1_v], axis=1).astype(BF16)
    cmp_w2 = jnp.stack([cmp_w2_k, cmp_w2_v], axis=1).astype(BF16)
    bf_rows = jnp.zeros((depth, 1, LANES), F32).at[:, 0, n_gate:n_gate + hf].set(b_forget)
    gains = [g.reshape(depth, 1, d) for g in (g_pre_mix, g_post_mix, g_pre_mlp, g_post_mlp)]

    tm_in = _tile(s, TILE_INPROJ_ROWS)
    tq_attn, tk_attn = _tile(s, TILE_ATTN_Q), _tile(s, TILE_ATTN_K)
    for l in range(depth):
        (mq, mk, mv, kmean, nq, cv, ks, vs, kw, vw, small, fq, fk, fv) = _inproj_call(
            x, gains[0], mod6, cos_t, sin_t, w_cat, l=l, hm=hm, hn=hn, hf=hf, tm=tm_in)

        fqa, fka = _fox_prep_call(small, bf_rows, l=l, hf=hf, col0=n_gate,
                                  ts=_tile(s, TILE_FOX_PREP_ROWS))
        fold = lambda t: t.reshape(b * t.shape[1], 1, s, LANES)
        flat = lambda t: t.reshape(b * t.shape[1], s, LANES)
        o_fox = _attn_call(fold(fq), flat(fqa), flat(fk), flat(fka), flat(fv),
                           tq=tq_attn, tk=tk_attn, out_dtype=BF16).reshape(b, hf, s, LANES)

        kmean = kmean.reshape(b, s // tm_in, hm, tm_in // MOBA_BLOCK, LANES)
        kmean = kmean.transpose(0, 2, 1, 3, 4).reshape(b * hm, nb, LANES)
        kmean = jnp.pad(kmean, ((0, 0), (0, LANES - nb), (0, 0)))
        mqa = _moba_select_call(fold(mq), kmean, nb=nb, tq=_tile(s, TILE_MOBA_SELECT_ROWS))
        o_moba = _attn_call(fold(mq), mqa, flat(mk), moba_ka, flat(mv), tq=tq_attn, tk=tk_attn,
                            out_dtype=BF16, ka_shared=True).reshape(b, hm, s, LANES)

        kvc = _compress_call(cv, cmp_pe, cmp_w1, cmp_w2, l=l)
        o_cw, sbias = _nsa_local_call(nq, kvc, overlap, kw, vw, small,
                                      tq=_tile(s, TILE_NSA_LOCAL_ROWS))
        slc_split = 2 if hn % 2 == 0 else 1
        o_slc = _attn_call(nq.reshape(b * slc_split, hn // slc_split, s, LANES), sbias, ks, slc_ka, vs,
                           tq=_tile(s, TILE_SLC_Q), tk=tk_attn, out_dtype=BF16,
                           kv_groups=slc_split, ka_shared=True).reshape(b, hn, s, LANES)

        x, h_mlp = _outproj_call(o_moba, o_cw, o_slc, small, o_fox, w_out_b, x,
                                 gains[1], gains[2], mod6, l=l, tm=_tile(s, TILE_OUTPROJ_ROWS))
        x = _mlp_call(h_mlp, w_up_b, w_down_b, x, gains[3], mod6, l=l,
                      tm=_tile(s, TILE_MLP_ROWS), tf=_tile(w_up.shape[2], TILE_MLP_HIDDEN))
    return x
```
